```python
import math
import functools
import jax
import jax.numpy as jnp
from jax import lax
import numpy as np

D_MODEL = 1024
BATCH = 8
SEQ = 2048
DEPTH = 2
DEC_BATCH = 128
DEC_SEQ = 4
PAST_LEN = 2048
PAGE_SIZE = 128

N_META = 16
BLOCK = 128
D_FF = 2816
EPS = 1e-6
FOX_HEADS = 4
FOX_HD = 64
FOX_W = FOX_HEADS * FOX_HD
SSD_HEADS = 8
SSD_HD = 64
SSD_W = SSD_HEADS * SSD_HD
SSD_GROUPS = 2
SSD_STATE = 64
SSD_CONV = 4
SSD_CONV_DIM = SSD_W + 2 * SSD_GROUPS * SSD_STATE
GLA_HEADS = 4
GLA_DK = 32
GLA_DV = 64
GLA_KW = GLA_HEADS * GLA_DK
GLA_W = GLA_HEADS * GLA_DV
GLA_RANK = 16
GLA_TAU = 16.0
D_MIX = FOX_W + SSD_W + GLA_W
IN_SIZES = (FOX_W, FOX_W, FOX_W, FOX_HEADS, SSD_W, SSD_CONV_DIM, SSD_HEADS,
            GLA_KW, GLA_KW, GLA_W, GLA_RANK, GLA_W)
N_IN = sum(IN_SIZES)
IN_SPLITS = tuple(int(v) for v in np.cumsum(IN_SIZES)[:-1])

kernel_name = 'hybrid_fox_ssd_gla_decoder_step'


def rmsnorm(x, g):
    xf = x.astype(jnp.float32)
    y = xf * lax.rsqrt(jnp.mean(xf * xf, axis=-1, keepdims=True) + EPS)
    return (y * g.astype(jnp.float32)).astype(x.dtype)


def swiglu(x, w_in, w_out):
    g, u = jnp.split(x @ w_in, 2, axis=-1)
    return (jax.nn.silu(g) * u) @ w_out


def causal_conv(u, buf, w, bias):
    width = w.shape[0]
    full = jnp.concatenate([buf.astype(u.dtype), u], axis=1)
    out = lax.conv_general_dilated(full, w[:, None, :].astype(u.dtype), window_strides=(1,),
                                   padding='VALID', dimension_numbers=('NWC', 'WIO', 'NWC'),
                                   feature_group_count=u.shape[-1])
    return jax.nn.silu(out + bias), full[:, full.shape[1] - (width - 1):]


def fox_block(q, k, v, fq, fk, qpos, kpos):
    s = jnp.einsum('bqhd,bkhd->bhqk', q, k).astype(jnp.float32) * (FOX_HD ** -0.5)
    s = s + (jnp.transpose(fq, (0, 2, 1))[..., :, None] - jnp.transpose(fk, (0, 2, 1))[..., None, :])
    s = jnp.where(kpos[None, :] <= qpos[:, None], s, -jnp.inf)
    p = jax.nn.softmax(s, axis=-1).astype(v.dtype)
    return jnp.einsum('bhqk,bkhd->bqhd', p, v)


def fox_prompt(q, k, v, logf):
    b, t = q.shape[0], q.shape[1]
    F = jnp.cumsum(logf.astype(jnp.float32), axis=1)
    pos = jnp.arange(t)
    o_meta = fox_block(q[:, :N_META], k[:, :N_META], v[:, :N_META], F[:, :N_META], F[:, :N_META],
                       pos[:N_META], pos[:N_META])
    nb = (t - N_META) // BLOCK
    qb = jnp.swapaxes(q[:, N_META:].reshape(b, nb, BLOCK, FOX_HEADS, FOX_HD), 0, 1)
    fb = jnp.swapaxes(F[:, N_META:].reshape(b, nb, BLOCK, FOX_HEADS), 0, 1)
    pb = pos[N_META:].reshape(nb, BLOCK)
    o = lax.map(lambda a: fox_block(a[0], k, v, a[1], F, a[2], pos), (qb, fb, pb))
    o = jnp.swapaxes(o, 0, 1).reshape(b, nb * BLOCK, FOX_HEADS, FOX_HD)
    return jnp.concatenate([o_meta, o], axis=1)


def fox_sample(q, k, v, logf, k_past, v_past, logf_past):
    kk = jnp.concatenate([k_past.astype(k.dtype), k], axis=1)
    vv = jnp.concatenate([v_past.astype(v.dtype), v], axis=1)
    F = jnp.cumsum(jnp.concatenate([logf_past.astype(jnp.float32), logf.astype(jnp.float32)], axis=1), axis=1)
    p_len = k_past.shape[1]
    pos = jnp.arange(kk.shape[1])
    return fox_block(q, kk, vv, F[:, p_len:], F, pos[p_len:], pos)


def ssd_chunks(xdt, da, bh, ch, h, chunk):
    b, L = xdt.shape[0], xdt.shape[1]
    nc = L // chunk
    tri = jnp.tril(jnp.ones((chunk, chunk), dtype=bool))

    def blocks(a):
        return jnp.swapaxes(a.reshape((b, nc, chunk) + a.shape[2:]), 0, 1)

    def step(hc, inp):
        xc, dac, bc, cc = inp
        cs = jnp.cumsum(dac, axis=1)
        seg = jnp.where(tri[None, :, :, None], cs[:, :, None, :] - cs[:, None, :, :], -jnp.inf)
        w_ts = jnp.einsum('bthn,bshn->btsh', cc, bc) * jnp.exp(seg)
        y = jnp.einsum('btsh,bshp->bthp', w_ts, xc) + jnp.einsum('bthn,bhpn->bthp', cc, hc) * jnp.exp(cs)[..., None]
        end = cs[:, -1]
        hc = hc * jnp.exp(end)[:, :, None, None] + jnp.einsum('bshn,bsh,bshp->bhpn', bc, jnp.exp(end[:, None] - cs), xc)
        return hc, y

    h, ys = lax.scan(step, h, (blocks(xdt), blocks(da), blocks(bh), blocks(ch)))
    return jnp.swapaxes(ys, 0, 1).reshape(xdt.shape), h


def ssd_mix(x, dt, a, bm, cm, h0, segments):
    f32 = jnp.float32
    rep = SSD_HEADS // SSD_GROUPS
    xdt = x.astype(f32) * dt[..., None]
    da = dt * a
    bh = jnp.repeat(bm.astype(f32), rep, axis=2)
    ch = jnp.repeat(cm.astype(f32), rep, axis=2)
    h = h0.astype(f32)
    ys, start = [], 0
    for length, chunk in segments:
        sl = slice(start, start + length)
        y, h = ssd_chunks(xdt[:, sl], da[:, sl], bh[:, sl], ch[:, sl], h, chunk)
        ys.append(y)
        start += length
    return jnp.concatenate(ys, axis=1), h


def gla_chunks(q, k, v, g, s, chunk):
    b, L = q.shape[0], q.shape[1]
    nc = L // chunk
    tri = jnp.tril(jnp.ones((chunk, chunk), dtype=bool))

    def blocks(a):
        return jnp.swapaxes(a.reshape((b, nc, chunk) + a.shape[2:]), 0, 1)

    def step(sc, inp):
        qc, kc, vc, gc = inp
        bc = jnp.cumsum(gc, axis=1)
        diff = jnp.where(tri[None, :, :, None, None], bc[:, :, None] - bc[:, None, :], -jnp.inf)
        att = jnp.einsum('bthk,bshk,btshk->btsh', qc, kc, jnp.exp(diff))
        o = jnp.einsum('btsh,bshv->bthv', att, vc) + jnp.einsum('bthk,bhkv->bthv', qc * jnp.exp(bc), sc)
        last = bc[:, -1]
        sc = sc * jnp.exp(last)[..., None] + jnp.einsum('bshk,bshv->bhkv', kc * jnp.exp(last[:, None] - bc), vc)
        return sc, o

    s, os_ = lax.scan(step, s, (blocks(q), blocks(k), blocks(v), blocks(g)))
    return jnp.swapaxes(os_, 0, 1).reshape(v.shape), s


def gla_mix(q, k, v, g, s0, segments):
    f32 = jnp.float32
    q, k, v, g = q.astype(f32), k.astype(f32), v.astype(f32), g.astype(f32)
    s = s0.astype(f32)
    os_, start = [], 0
    for length, chunk in segments:
        sl = slice(start, start + length)
        o, s = gla_chunks(q[:, sl], k[:, sl], v[:, sl], g[:, sl], s, chunk)
        os_.append(o)
        start += length
    return jnp.concatenate(os_, axis=1), s


def trunk_layer(x, lp, conv_buf, h0, s0, segments, fox_fn):
    f32 = jnp.float32
    b, t = x.shape[0], x.shape[1]
    x = x + 0.5 * swiglu(rmsnorm(x, lp['ffn1_norm']), lp['ffn1_w_in'], lp['ffn1_w_out'])
    h = rmsnorm(x, lp['mix_norm'])
    fq, fk, fv, ff, sz, sxbc, sdt, gq, gk, gv, glr, gg = jnp.split(h @ lp['w_mix_in'], IN_SPLITS, axis=-1)
    fq = rmsnorm(fq.reshape(b, t, FOX_HEADS, FOX_HD), lp['fox_q_norm'])
    fk = rmsnorm(fk.reshape(b, t, FOX_HEADS, FOX_HD), lp['fox_k_norm'])
    fv = fv.reshape(b, t, FOX_HEADS, FOX_HD)
    logf = jax.nn.log_sigmoid((ff + lp['fox_f_bias']).astype(f32))
    fox_o = fox_fn(fq, fk, fv, logf).reshape(b, t, FOX_W)
    xbc, conv_new = causal_conv(sxbc, conv_buf, lp['ssd_conv_w'], lp['ssd_conv_b'])
    sx, sb, sc = jnp.split(xbc, (SSD_W, SSD_W + SSD_GROUPS * SSD_STATE), axis=-1)
    sx = sx.reshape(b, t, SSD_HEADS, SSD_HD)
    dt = jax.nn.softplus((sdt + lp['ssd_dt_bias']).astype(f32))
    a = -jnp.exp(lp['ssd_a_log'].astype(f32))
    y, h_new = ssd_mix(sx, dt, a, sb.reshape(b, t, SSD_GROUPS, SSD_STATE),
                       sc.reshape(b, t, SSD_GROUPS, SSD_STATE), h0, segments)
    y = (y + sx.astype(f32) * lp['ssd_d'].astype(f32)[:, None]).astype(x.dtype)
    y = y.reshape(b, t, SSD_W) * jax.nn.silu(sz)
    ssd_o = rmsnorm(y.reshape(b, t, SSD_GROUPS, SSD_W // SSD_GROUPS),
                    lp['ssd_norm'].reshape(SSD_GROUPS, SSD_W // SSD_GROUPS)).reshape(b, t, SSD_W)
    glog = jax.nn.log_sigmoid((glr @ lp['gla_w_gate'] + lp['gla_gate_bias']).astype(f32)) / GLA_TAU
    go, s_new = gla_mix(gq.reshape(b, t, GLA_HEADS, GLA_DK) * (GLA_DK ** -0.5),
                        gk.reshape(b, t, GLA_HEADS, GLA_DK), gv.reshape(b, t, GLA_HEADS, GLA_DV),
                        glog.reshape(b, t, GLA_HEADS, GLA_DK), s0, segments)
    gla_o = rmsnorm(go.astype(x.dtype), lp['gla_norm']).reshape(b, t, GLA_W) * jax.nn.silu(gg)
    x = x + jnp.concatenate([fox_o, ssd_o, gla_o], axis=-1) @ lp['w_mix_out']
    x = x + 0.5 * swiglu(rmsnorm(x, lp['ffn2_norm']), lp['ffn2_w_in'], lp['ffn2_w_out'])
    return x, (fk, fv, logf, conv_new, h_new, s_new)


def setup_inputs(seed: int = 0) -> dict:
    key = jax.random.key(seed)
    keys = iter(jax.random.split(key, 48))
    f32 = jnp.float32

    def nrm(shape, scale=1.0):
        return jax.random.normal(next(keys), shape, f32) * scale

    def gain(shape):
        return 1.0 + nrm(shape, 0.02)

    n_pages = PAST_LEN // PAGE_SIZE
    n_used = DEC_BATCH * n_pages
    n_pool = n_used + n_used // 4
    page_table = jax.random.permutation(next(keys), n_pool)[:n_used].reshape(DEC_BATCH, n_pages).astype(jnp.int32)
    dt0 = jnp.exp(jax.random.uniform(next(keys), (DEPTH, SSD_HEADS), f32, math.log(1e-3), math.log(1e-1)))
    ssd_dt_bias = dt0 + jnp.log(-jnp.expm1(-dt0))
    return {
        'x_prompt': nrm((BATCH, SEQ, D_MODEL)),
        'x_sample': nrm((DEC_BATCH, DEC_SEQ, D_MODEL)),
        'cache_fox_k': nrm((DEPTH, n_pool, PAGE_SIZE, FOX_HEADS, FOX_HD)),
        'cache_fox_v': nrm((DEPTH, n_pool, PAGE_SIZE, FOX_HEADS, FOX_HD)),
        'cache_fox_logf': jax.nn.log_sigmoid(nrm((DEPTH, n_pool, PAGE_SIZE, FOX_HEADS)) + 3.0),
        'state_ssm': nrm((DEPTH, DEC_BATCH, SSD_HEADS, SSD_HD, SSD_STATE), 0.1),
        'state_conv': nrm((DEPTH, DEC_BATCH, SSD_CONV - 1, SSD_CONV_DIM)),
        'state_gla': nrm((DEPTH, DEC_BATCH, GLA_HEADS, GLA_DK, GLA_DV), 0.3),
        'page_table': page_table,
        'meta_tokens': nrm((N_META, D_MODEL)),
        'ffn1_norm': gain((DEPTH, D_MODEL)),
        'ffn1_w_in': nrm((DEPTH, D_MODEL, 2 * D_FF), D_MODEL ** -0.5),
        'ffn1_w_out': nrm((DEPTH, D_FF, D_MODEL), D_FF ** -0.5),
        'mix_norm': gain((DEPTH, D_MODEL)),
        'w_mix_in': nrm((DEPTH, D_MODEL, N_IN), D_MODEL ** -0.5),
        'fox_q_norm': gain((DEPTH, FOX_HD)),
        'fox_k_norm': gain((DEPTH, FOX_HD)),
        'fox_f_bias': jax.random.uniform(next(keys), (DEPTH, FOX_HEADS), f32, 1.0, 4.0),
        'ssd_conv_w': nrm((DEPTH, SSD_CONV, SSD_CONV_DIM), SSD_CONV ** -0.5),
        'ssd_conv_b': nrm((DEPTH, SSD_CONV_DIM), 0.02),
        'ssd_dt_bias': ssd_dt_bias,
        'ssd_a_log': jnp.log(jax.random.uniform(next(keys), (DEPTH, SSD_HEADS), f32, 1.0, 16.0)),
        'ssd_d': gain((DEPTH, SSD_HEADS)),
        'ssd_norm': gain((DEPTH, SSD_W)),
        'gla_w_gate': nrm((DEPTH, GLA_RANK, GLA_KW), GLA_RANK ** -0.5),
        'gla_gate_bias': nrm((DEPTH, GLA_KW), 0.1),
        'gla_norm': gain((DEPTH, GLA_DV)),
        'w_mix_out': nrm((DEPTH, D_MIX, D_MODEL), D_MIX ** -0.5),
        'ffn2_norm': gain((DEPTH, D_MODEL)),
        'ffn2_w_in': nrm((DEPTH, D_MODEL, 2 * D_FF), D_MODEL ** -0.5),
        'ffn2_w_out': nrm((DEPTH, D_FF, D_MODEL), D_FF ** -0.5),
    }


def reference(x_prompt, x_sample, cache_fox_k, cache_fox_v, cache_fox_logf, state_ssm, state_conv,
              state_gla, page_table, meta_tokens, ffn1_norm, ffn1_w_in, ffn1_w_out, mix_norm, w_mix_in,
              fox_q_norm, fox_k_norm, fox_f_bias, ssd_conv_w, ssd_conv_b, ssd_dt_bias, ssd_a_log, ssd_d,
              ssd_norm, gla_w_gate, gla_gate_bias, gla_norm, w_mix_out, ffn2_norm, ffn2_w_in, ffn2_w_out):
    bp, seq = x_prompt.shape[0], x_prompt.shape[1]
    db, dseq = x_sample.shape[0], x_sample.shape[1]
    meta = jnp.broadcast_to(meta_tokens.astype(x_prompt.dtype)[None], (bp, N_META, D_MODEL))
    xp = jnp.concatenate([meta, x_prompt], axis=1)
    xs = x_sample
    seg_p = ((N_META, N_META), (seq, BLOCK))
    seg_s = ((dseq, dseq),)
    zero_conv = jnp.zeros((bp, SSD_CONV - 1, SSD_CONV_DIM), xp.dtype)
    zero_ssm = jnp.zeros((bp, SSD_HEADS, SSD_HD, SSD_STATE), jnp.float32)
    zero_gla = jnp.zeros((bp, GLA_HEADS, GLA_DK, GLA_DV), jnp.float32)
    new_p = [[] for _ in range(6)]
    new_s = [[] for _ in range(6)]
    for l in range(DEPTH):
        lp = dict(ffn1_norm=ffn1_norm[l], ffn1_w_in=ffn1_w_in[l], ffn1_w_out=ffn1_w_out[l],
                  mix_norm=mix_norm[l], w_mix_in=w_mix_in[l], fox_q_norm=fox_q_norm[l],
                  fox_k_norm=fox_k_norm[l], fox_f_bias=fox_f_bias[l], ssd_conv_w=ssd_conv_w[l],
                  ssd_conv_b=ssd_conv_b[l], ssd_dt_bias=ssd_dt_bias[l], ssd_a_log=ssd_a_log[l],
                  ssd_d=ssd_d[l], ssd_norm=ssd_norm[l], gla_w_gate=gla_w_gate[l],
                  gla_gate_bias=gla_gate_bias[l], gla_norm=gla_norm[l], w_mix_out=w_mix_out[l],
                  ffn2_norm=ffn2_norm[l], ffn2_w_in=ffn2_w_in[l], ffn2_w_out=ffn2_w_out[l])
        xp, st_p = trunk_layer(xp, lp, zero_conv, zero_ssm, zero_gla, seg_p, fox_prompt)
        k_past = cache_fox_k[l, page_table].reshape(db, -1, FOX_HEADS, FOX_HD)
        v_past = cache_fox_v[l, page_table].reshape(db, -1, FOX_HEADS, FOX_HD)
        lf_past = cache_fox_logf[l, page_table].reshape(db, -1, FOX_HEADS)
        fox_s = functools.partial(fox_sample, k_past=k_past, v_past=v_past, logf_past=lf_past)
        xs, st_s = trunk_layer(xs, lp, state_conv[l], state_ssm[l], state_gla[l], seg_s, fox_s)
        for i in range(6):
            new_p[i].append(st_p[i])
            new_s[i].append(st_s[i])
    k_p, v_p, lf_p, conv_p, ssm_p, gla_p = [jnp.stack(a) for a in new_p]
    k_s, v_s, lf_s, conv_s, ssm_s, gla_s = [jnp.stack(a) for a in new_s]
    return (xp[:, N_META:], xs, k_p, v_p, lf_p, ssm_p, conv_p, gla_p, k_s, v_s, lf_s, ssm_s, conv_s, gla_s)
```

```python
import functools
import math

import jax
import jax.numpy as jnp
import numpy as np
from jax import lax
from jax.experimental import pallas as pl
from jax.experimental.pallas import tpu as pltpu

F32 = jnp.float32
BF16 = jnp.bfloat16

D_MODEL = 1024
N_META = 16
D_FF = 2816
EPS = 1e-6
FOX_HEADS = 4
FOX_HD = 64
FOX_W = FOX_HEADS * FOX_HD
SSD_HEADS = 8
SSD_HD = 64
SSD_W = SSD_HEADS * SSD_HD
SSD_GROUPS = 2
SSD_STATE = 64
SSD_CONV = 4
SSD_BC = SSD_GROUPS * SSD_STATE
SSD_CONV_DIM = SSD_W + 2 * SSD_BC
GLA_HEADS = 4
GLA_DK = 32
GLA_DV = 64
GLA_KW = GLA_HEADS * GLA_DK
GLA_W = GLA_HEADS * GLA_DV
GLA_RANK = 16
GLA_TAU = 16.0
D_MIX = FOX_W + SSD_W + GLA_W
IN_SIZES = (FOX_W, FOX_W, FOX_W, FOX_HEADS, SSD_W, SSD_CONV_DIM, SSD_HEADS,
            GLA_KW, GLA_KW, GLA_W, GLA_RANK, GLA_W)
IN_SPLITS = tuple(int(v) for v in np.cumsum(IN_SIZES)[:-1])

LANES = 128
CHUNK = 128
GLA_SUB = 32
FOX_KW = 512
FFN_CK = 256
FFN_TM = 512
VMEM_LIMIT = 56 * 1024 * 1024

C_Q, C_K, C_V = 0, 256, 512
C_Z = 768
C_XBC = 1280
C_GQ, C_GK, C_GV, C_GG = 2048, 2176, 2304, 2560
C_SM = 2816
N_PROJ = 2944
SM_F = 0
SM_DT = 4
SM_LR = 12
R_K, R_V, R_Z, R_XBC = 0, 256, 512, 1024
R_GQ, R_GK, R_GV, R_GG, R_SM = 1792, 1920, 2048, 2304, 2560
N_PT = 2688


def _dot(a, b):
    return jnp.dot(a.astype(BF16), b.astype(BF16), preferred_element_type=F32)


def _dot_nt(a, b):
    return lax.dot_general(a.astype(BF16), b.astype(BF16), (((1,), (1,)), ((), ())),
                           preferred_element_type=F32)


def _dot_tn(a, b):
    return lax.dot_general(a.astype(BF16), b.astype(BF16), (((0,), (0,)), ((), ())),
                           preferred_element_type=F32)


def _split(a):
    hi = a.astype(BF16)
    lo = (a - hi.astype(F32)).astype(BF16)
    return hi, lo


def _dot_hl(a, m):
    hi, lo = _split(a)
    return (jnp.dot(hi, m, preferred_element_type=F32) + jnp.dot(lo, m, preferred_element_type=F32))


def _dot_hl_left(m, a):
    hi, lo = _split(a)
    return (jnp.dot(m, hi, preferred_element_type=F32) + jnp.dot(m, lo, preferred_element_type=F32))


def _silu(x):
    return x * jax.nn.sigmoid(x)


def _softplus(x):
    return jnp.maximum(x, 0.0) + jnp.log1p(jnp.exp(-jnp.abs(x)))


def _log_sigmoid(x):
    return jnp.minimum(x, 0.0) - jnp.log1p(jnp.exp(-jnp.abs(x)))


def _rms(x, g):
    return x * lax.rsqrt(jnp.mean(x * x, axis=-1, keepdims=True) + EPS) * g


def _const_spec(shape):
    n = len(shape)
    return pl.BlockSpec(shape, lambda *_: (0,) * n)


def _params(sem):
    return pltpu.CompilerParams(dimension_semantics=sem, vmem_limit_bytes=VMEM_LIMIT)


def _swiglu_half(x, g_ref, win_ref, wout_ref):
    h = _rms(x, g_ref[...]).astype(BF16)
    acc = None
    for c in range(D_FF // FFN_CK):
        g = jnp.dot(h, win_ref[:, c * FFN_CK:(c + 1) * FFN_CK], preferred_element_type=F32)
        u = jnp.dot(h, win_ref[:, D_FF + c * FFN_CK:D_FF + (c + 1) * FFN_CK],
                    preferred_element_type=F32)
        a = (_silu(g) * u).astype(BF16)
        part = jnp.dot(a, wout_ref[c * FFN_CK:(c + 1) * FFN_CK, :], preferred_element_type=F32)
        acc = part if acc is None else acc + part
    return x + 0.5 * acc


def _ffn_kernel(x_ref, g_ref, win_ref, wout_ref, o_ref):
    o_ref[...] = _swiglu_half(x_ref[...], g_ref, win_ref, wout_ref)


def _mix_ffn_kernel(x_ref, fo_ref, so_ref, go_ref, wmo_ref, g_ref, win_ref, wout_ref, o_ref):
    x = x_ref[...]
    x = x + jnp.dot(fo_ref[...], wmo_ref[0:FOX_W, :], preferred_element_type=F32)
    x = x + jnp.dot(so_ref[...], wmo_ref[FOX_W:FOX_W + SSD_W, :], preferred_element_type=F32)
    x = x + jnp.dot(go_ref[...], wmo_ref[FOX_W + SSD_W:D_MIX, :], preferred_element_type=F32)
    o_ref[...] = _swiglu_half(x, g_ref, win_ref, wout_ref)


def _mix_ffn_t_kernel(x_ref, fo_ref, so_ref, go_ref, wmo_ref, g_ref, win_ref, wout_ref, o_ref):
    x = x_ref[...]
    x = x + _dot(fo_ref[0], wmo_ref[0:FOX_W, :])
    x = x + _dot(so_ref[0].T, wmo_ref[FOX_W:FOX_W + SSD_W, :])
    x = x + _dot(go_ref[0].T, wmo_ref[FOX_W + SSD_W:D_MIX, :])
    o_ref[...] = _swiglu_half(x, g_ref, win_ref, wout_ref)


def _ffn_weight_specs():
    return [_const_spec((1, D_MODEL)), _const_spec((D_MODEL, 2 * D_FF)), _const_spec((D_FF, D_MODEL))]


def _ffn(x, g, win, wout, tm):
    rows = x.shape[0]
    row = lambda w: pl.BlockSpec((tm, w), lambda i: (i, 0))
    return pl.pallas_call(
        _ffn_kernel, out_shape=jax.ShapeDtypeStruct((rows, D_MODEL), F32), grid=(rows // tm,),
        in_specs=[row(D_MODEL)] + _ffn_weight_specs(), out_specs=row(D_MODEL),
        compiler_params=_params(("parallel",)), name="ffn")(x, g, win, wout)


def _mix_ffn(x, fo, so, go, wmo, g, win, wout, tm):
    rows = x.shape[0]
    row = lambda w: pl.BlockSpec((tm, w), lambda i: (i, 0))
    return pl.pallas_call(
        _mix_ffn_kernel, out_shape=jax.ShapeDtypeStruct((rows, D_MODEL), F32), grid=(rows // tm,),
        in_specs=[row(D_MODEL), row(FOX_W), row(SSD_W), row(GLA_W), _const_spec((D_MIX, D_MODEL))]
        + _ffn_weight_specs(), out_specs=row(D_MODEL),
        compiler_params=_params(("parallel",)), name="mix_ffn")(x, fo, so, go, wmo, g, win, wout)


def _mix_ffn_t(x, fo, so_t, go_t, wmo, g, win, wout):
    rows = x.shape[0]
    n_t = rows // LANES
    row = pl.BlockSpec((LANES, D_MODEL), lambda i: (i, 0))
    per_t = lambda w, l: pl.BlockSpec((1, w, l), lambda i: (i, 0, 0))
    return pl.pallas_call(
        _mix_ffn_t_kernel, out_shape=jax.ShapeDtypeStruct((rows, D_MODEL), F32), grid=(n_t,),
        in_specs=[row, per_t(LANES, FOX_W), per_t(SSD_W, LANES), per_t(GLA_W, LANES),
                  _const_spec((D_MIX, D_MODEL))] + _ffn_weight_specs(), out_specs=row,
        compiler_params=_params(("parallel",)), name="mix_ffn_t")(x, fo, so_t, go_t, wmo, g, win, wout)


def _head_norm(x, gain_row, g64_ref):
    msq = _dot_hl(x * x, g64_ref[...])
    return x * lax.rsqrt(msq + EPS) * gain_row


def _small_block(sm, fb_ref, dtb_ref):
    logf = _log_sigmoid(sm + fb_ref[...])
    dt = _softplus(sm + dtb_ref[...])
    return logf, dt


def _proj_p_kernel(x_ref, nrm_ref, w_ref, qg_ref, kg_ref, fb_ref, dtb_ref, cw_ref, cb_ref, g64_ref,
                   ltri_ref,
                   q_ref, kt_ref, vt_ref, z_ref, xbc_ref, gqk_ref, gv_ref, gg_ref, smr_ref, smt_ref,
                   ft_ref, convp_ref,
                   xb_scr, carry_scr, *, t_real):
    j = pl.program_id(1)
    nj = pl.num_programs(1)

    @pl.when(j == 0)
    def _():
        xb_scr[0:8, :] = jnp.zeros((8, SSD_CONV_DIM), F32)
        carry_scr[...] = jnp.zeros((8, LANES), F32)

    h = _rms(x_ref[0], nrm_ref[...]).astype(BF16)
    p = jnp.dot(h, w_ref[...], preferred_element_type=F32)

    q = _head_norm(p[:, C_Q:C_Q + FOX_W], qg_ref[...], g64_ref) * (FOX_HD ** -0.5)
    q_ref[0] = q.astype(BF16)
    k = _head_norm(p[:, C_K:C_K + FOX_W], kg_ref[...], g64_ref)
    kt_ref[0] = k.T
    vt_ref[0] = p[:, C_V:C_V + FOX_W].T
    z_ref[0] = p[:, C_Z:C_Z + SSD_W]
    gqk_ref[0] = p[:, C_GQ:C_GQ + 2 * GLA_KW]
    gv_ref[0] = p[:, C_GV:C_GV + GLA_W]
    gg_ref[0] = p[:, C_GG:C_GG + GLA_W]

    xb_scr[8:8 + CHUNK, :] = p[:, C_XBC:C_XBC + SSD_CONV_DIM]
    conv = cb_ref[...]
    for w in range(SSD_CONV):
        conv = conv + xb_scr[pl.ds(8 - (SSD_CONV - 1) + w, CHUNK), :] * cw_ref[w:w + 1, :]
    xbc_ref[0] = _silu(conv)
    r_end = t_real - (t_real - 1) // CHUNK * CHUNK

    @pl.when(j == nj - 1)
    def _():
        convp_ref[0] = xb_scr[r_end:r_end + 8, :]

    xb_scr[0:8, :] = xb_scr[CHUNK:CHUNK + 8, :]

    sm = p[:, C_SM:C_SM + LANES]
    lane = lax.broadcasted_iota(jnp.int32, (CHUNK, LANES), 1)
    row = lax.broadcasted_iota(jnp.int32, (CHUNK, LANES), 0)
    logf, dt = _small_block(sm, fb_ref, dtb_ref)
    logf = jnp.where(lane < SM_DT, logf, 0.0)
    dt = jnp.where(j * CHUNK + row < t_real, dt, 0.0)
    f_rows = _dot_hl_left(ltri_ref[...], logf) + carry_scr[0:1, :]
    carry_scr[...] = jnp.broadcast_to(f_rows[CHUNK - 1:CHUNK, :], (8, LANES))
    smr_ref[0] = jnp.where(lane < SM_DT, f_rows, jnp.where(lane < SM_LR, dt, sm))
    smt_ref[0] = jnp.where(lane < SM_DT, logf, dt).T[0:16, :]
    ft_ref[0] = f_rows.T[0:8, :]


def _proj_p(x3, nrm, w, qg, kg, fb, dtb, cw, cb, g64, ltri, t_real):
    nb, t_pad, _ = x3.shape
    nj = t_pad // CHUNK
    rows = lambda w_: pl.BlockSpec((1, CHUNK, w_), lambda b, j: (b, j, 0))
    cols = lambda h_: pl.BlockSpec((1, h_, CHUNK), lambda b, j: (b, 0, j))
    out_shape = [
        jax.ShapeDtypeStruct((nb, t_pad, FOX_W), BF16),
        jax.ShapeDtypeStruct((nb, FOX_W, t_pad), F32),
        jax.ShapeDtypeStruct((nb, FOX_W, t_pad), F32),
        jax.ShapeDtypeStruct((nb, t_pad, SSD_W), F32),
        jax.ShapeDtypeStruct((nb, t_pad, SSD_CONV_DIM), F32),
        jax.ShapeDtypeStruct((nb, t_pad, 2 * GLA_KW), F32),
        jax.ShapeDtypeStruct((nb, t_pad, GLA_W), F32),
        jax.ShapeDtypeStruct((nb, t_pad, GLA_W), F32),
        jax.ShapeDtypeStruct((nb, t_pad, LANES), F32),
        jax.ShapeDtypeStruct((nb, 16, t_pad), F32),
        jax.ShapeDtypeStruct((nb, 8, t_pad), F32),
        jax.ShapeDtypeStruct((nb, 8, SSD_CONV_DIM), F32),
    ]
    out_specs = [rows(FOX_W), cols(FOX_W), cols(FOX_W), rows(SSD_W), rows(SSD_CONV_DIM),
                 rows(2 * GLA_KW), rows(GLA_W), rows(GLA_W), rows(LANES), cols(16), cols(8),
                 pl.BlockSpec((1, 8, SSD_CONV_DIM), lambda b, j: (b, 0, 0))]
    in_specs = [rows(D_MODEL), _const_spec((1, D_MODEL)), _const_spec((D_MODEL, N_PROJ)),
                _const_spec((1, FOX_W)), _const_spec((1, FOX_W)), _const_spec((1, LANES)),
                _const_spec((1, LANES)), _const_spec((8, SSD_CONV_DIM)), _const_spec((1, SSD_CONV_DIM)),
                _const_spec((FOX_W, FOX_W)), _const_spec((CHUNK, CHUNK))]
    return pl.pallas_call(
        functools.partial(_proj_p_kernel, t_real=t_real), out_shape=out_shape, grid=(nb, nj),
        in_specs=in_specs, out_specs=out_specs,
        scratch_shapes=[pltpu.VMEM((8 + CHUNK, SSD_CONV_DIM), F32), pltpu.VMEM((8, LANES), F32)],
        compiler_params=_params(("parallel", "arbitrary")), name="proj_p")(
            x3, nrm, w, qg, kg, fb, dtb, cw, cb, g64, ltri)


def _proj_s_kernel(x_ref, nrm_ref, w_ref, qg_ref, kg_ref, fb_ref, dtb_ref, cw_ref, cb_ref, g64_ref,
                   cst_ref,
                   q_ref, kr_ref, vr_ref, pt_ref, cum_ref, convs_ref,
                   hist_scr, carry_scr, *, n_t):
    t = pl.program_id(0)

    @pl.when(t == 0)
    def _():
        hist_scr[0:SSD_CONV - 1] = cst_ref[...]
        carry_scr[...] = jnp.zeros((LANES, LANES), F32)

    h = _rms(x_ref[...], nrm_ref[...]).astype(BF16)
    p = jnp.dot(h, w_ref[...], preferred_element_type=F32)

    q_ref[0] = _head_norm(p[:, C_Q:C_Q + FOX_W], qg_ref[...], g64_ref) * (FOX_HD ** -0.5)
    k = _head_norm(p[:, C_K:C_K + FOX_W], kg_ref[...], g64_ref)
    v = p[:, C_V:C_V + FOX_W]
    kr_ref[0] = k
    vr_ref[0] = v
    for c in range(FOX_W // LANES):
        pt_ref[0, R_K + c * LANES:R_K + (c + 1) * LANES, :] = k[:, c * LANES:(c + 1) * LANES].T
        pt_ref[0, R_V + c * LANES:R_V + (c + 1) * LANES, :] = v[:, c * LANES:(c + 1) * LANES].T

    hist_scr[pl.ds(SSD_CONV - 1 + t, 1)] = p[:, C_XBC:C_XBC + SSD_CONV_DIM][None]
    conv = cb_ref[...]
    for w in range(SSD_CONV):
        conv = conv + hist_scr[t + w] * cw_ref[w:w + 1, :]
    xbc = _silu(conv)

    @pl.when(t == n_t - 1)
    def _():
        convs_ref[...] = hist_scr[n_t:n_t + SSD_CONV - 1]

    sm = p[:, C_SM:C_SM + LANES]
    lane = lax.broadcasted_iota(jnp.int32, (LANES, LANES), 1)
    logf, dt = _small_block(sm, fb_ref, dtb_ref)
    cum = carry_scr[...] + jnp.where(lane < SM_DT, logf, 0.0)
    carry_scr[...] = cum
    cum_ref[0] = cum
    smc = jnp.where(lane < SM_DT, logf, jnp.where(lane < SM_LR, dt, sm))

    def put_t(r0, val):
        for c in range(val.shape[1] // LANES):
            pt_ref[0, r0 + c * LANES:r0 + (c + 1) * LANES, :] = val[:, c * LANES:(c + 1) * LANES].T

    put_t(R_Z, p[:, C_Z:C_Z + SSD_W])
    put_t(R_XBC, xbc)
    put_t(R_GQ, p[:, C_GQ:C_GQ + GLA_KW] * (GLA_DK ** -0.5))
    put_t(R_GK, p[:, C_GK:C_GK + GLA_KW])
    put_t(R_GV, p[:, C_GV:C_GV + GLA_W])
    put_t(R_GG, p[:, C_GG:C_GG + GLA_W])
    put_t(R_SM, smc)


def _proj_s(x, nrm, w, qg, kg, fb, dtb, cw, cb, g64, conv_state):
    n_t = x.shape[0] // LANES
    per_t = lambda a, b: pl.BlockSpec((1, a, b), lambda t: (t, 0, 0))
    out_shape = [
        jax.ShapeDtypeStruct((n_t, LANES, FOX_W), F32),
        jax.ShapeDtypeStruct((n_t, LANES, FOX_W), F32),
        jax.ShapeDtypeStruct((n_t, LANES, FOX_W), F32),
        jax.ShapeDtypeStruct((n_t, N_PT, LANES), F32),
        jax.ShapeDtypeStruct((n_t, LANES, LANES), F32),
        jax.ShapeDtypeStruct((SSD_CONV - 1, LANES, SSD_CONV_DIM), F32),
    ]
    out_specs = [per_t(LANES, FOX_W), per_t(LANES, FOX_W), per_t(LANES, FOX_W), per_t(N_PT, LANES),
                 per_t(LANES, LANES), _const_spec((SSD_CONV - 1, LANES, SSD_CONV_DIM))]
    in_specs = [pl.BlockSpec((LANES, D_MODEL), lambda t: (t, 0)), _const_spec((1, D_MODEL)),
                _const_spec((D_MODEL, N_PROJ)), _const_spec((1, FOX_W)), _const_spec((1, FOX_W)),
                _const_spec((1, LANES)), _const_spec((1, LANES)), _const_spec((8, SSD_CONV_DIM)),
                _const_spec((1, SSD_CONV_DIM)), _const_spec((FOX_W, FOX_W)),
                _const_spec((SSD_CONV - 1, LANES, SSD_CONV_DIM))]
    return pl.pallas_call(
        functools.partial(_proj_s_kernel, n_t=n_t), out_shape=out_shape, grid=(n_t,),
        in_specs=in_specs, out_specs=out_specs,
        scratch_shapes=[pltpu.VMEM((n_t + SSD_CONV - 1, LANES, SSD_CONV_DIM), F32),
                        pltpu.VMEM((LANES, LANES), F32)],
        compiler_params=_params(("arbitrary",)), name="proj_s")(
            x, nrm, w, qg, kg, fb, dtb, cw, cb, g64, conv_state)


def _fox_p_kernel(q_ref, kt_ref, vt_ref, smr_ref, ft_ref, o_ref, *, kw):
    qi = pl.program_id(1)
    q = q_ref[0]
    fcol = smr_ref[0]
    n_full = (qi * CHUNK) // kw

    def block(h, start, width, carry, masked):
        m, l, acc = carry
        hs = slice(h * FOX_HD, (h + 1) * FOX_HD)
        kt = kt_ref[0, hs, pl.ds(start, width)]
        s = _dot(q[:, hs], kt) + (fcol[:, h:h + 1] - ft_ref[0, h:h + 1, pl.ds(start, width)])
        if masked:
            r = lax.broadcasted_iota(jnp.int32, (CHUNK, width), 0) + qi * CHUNK
            c = lax.broadcasted_iota(jnp.int32, (CHUNK, width), 1) + start
            s = jnp.where(c <= r, s, -jnp.inf)
        m_new = jnp.maximum(m, jnp.max(s, axis=-1, keepdims=True))
        alpha = jnp.exp(m - m_new)
        pr = jnp.exp(s - m_new)
        l = alpha * l + jnp.sum(pr, axis=-1, keepdims=True)
        acc = alpha * acc + _dot_nt(pr, vt_ref[0, hs, pl.ds(start, width)])
        return m_new, l, acc

    for h in range(FOX_HEADS):
        carry = (jnp.full((CHUNK, 1), -jnp.inf, F32), jnp.zeros((CHUNK, 1), F32),
                 jnp.zeros((CHUNK, FOX_HD), F32))
        carry = lax.fori_loop(
            0, n_full,
            lambda i, c: block(h, pl.multiple_of(i * kw, kw), kw, c, False), carry)
        carry = lax.fori_loop(
            n_full * (kw // CHUNK), qi + 1,
            lambda i, c: block(h, pl.multiple_of(i * CHUNK, CHUNK), CHUNK, c, True), carry)
        m, l, acc = carry
        o_ref[0, :, h * FOX_HD:(h + 1) * FOX_HD] = (acc / l).astype(BF16)


def _fox_p(q, kt, vt, smr, ft):
    nb, t_pad, _ = q.shape
    nq = t_pad // CHUNK
    full = lambda h_: pl.BlockSpec((1, h_, t_pad), lambda b, i: (b, 0, 0))
    rows = lambda w_: pl.BlockSpec((1, CHUNK, w_), lambda b, i: (b, i, 0))
    return pl.pallas_call(
        functools.partial(_fox_p_kernel, kw=FOX_KW),
        out_shape=jax.ShapeDtypeStruct((nb, t_pad, FOX_W), BF16), grid=(nb, nq),
        in_specs=[rows(FOX_W), full(FOX_W), full(FOX_W), rows(LANES), full(8)],
        out_specs=rows(FOX_W),
        compiler_params=_params(("parallel", "arbitrary")), name="fox_p")(q, kt, vt, smr, ft)


def _fox_s_kernel(tbl_ref, q_ref, kr_ref, vr_ref, cum_ref, lf_ref, mstrict_ref, pgsuf_ref,
                  k_hbm, v_hbm, o_ref, kbuf, vbuf, lfst, sem, *, layer, n_pages, n_t):
    b = pl.program_id(0)
    nb = pl.num_programs(0)
    slot = lax.rem(b, 2)

    def copies(seq, sl):
        out = []
        for pg in range(n_pages):
            page = tbl_ref[seq * n_pages + pg]
            out.append(pltpu.make_async_copy(k_hbm.at[layer, page], kbuf.at[sl, pg], sem.at[0, sl]))
            out.append(pltpu.make_async_copy(v_hbm.at[layer, page], vbuf.at[sl, pg], sem.at[1, sl]))
        return out

    @pl.when(b == 0)
    def _():
        lfst[...] = jnp.zeros(lfst.shape, F32)
        for c in copies(0, 0):
            c.start()

    @pl.when(b + 1 < nb)
    def _():
        for c in copies(b + 1, 1 - slot):
            c.start()

    for pg in range(n_pages):
        page = tbl_ref[b * n_pages + pg]
        lfst[pg * 8:pg * 8 + FOX_HEADS, :] = lf_ref[0, page // 2, pl.ds(lax.rem(page, 2) * FOX_HEADS,
                                                                      FOX_HEADS), :]
    lf = lfst[...]
    within = _dot_hl(lf, mstrict_ref[...])
    tot = jnp.broadcast_to(jnp.sum(lf, axis=-1, keepdims=True), lf.shape)
    dsuf = within + _dot_hl_left(pgsuf_ref[...], tot)

    lane = lax.broadcasted_iota(jnp.int32, (8, FOX_W), 1)
    rowh = lax.broadcasted_iota(jnp.int32, (8, FOX_W), 0)
    headmask = (lane // FOX_HD) == rowh
    qexp, cumcol = [], []
    cum_rows = jnp.concatenate([cum_ref[t, pl.ds(b, 1), :] for t in range(n_t)]
                               + [jnp.zeros((LANES - n_t, LANES), F32)], axis=0)
    cum_t = cum_rows.T[0:8, 0:8]
    for t in range(n_t):
        qrow = q_ref[t, pl.ds(b, 1), :]
        qexp.append(jnp.where(headmask, jnp.broadcast_to(qrow, (8, FOX_W)), 0.0))
        cumcol.append(cum_t[:, t:t + 1])
    qexp = jnp.concatenate(qexp, axis=0).astype(BF16)
    cumcol = jnp.concatenate(cumcol, axis=0)
    rows = 8 * n_t

    kn = jnp.concatenate([kr_ref[t, pl.ds(b, 1), :] for t in range(n_t)]
                         + [jnp.zeros((8 - n_t, FOX_W), F32)], axis=0)
    vn = jnp.concatenate([vr_ref[t, pl.ds(b, 1), :] for t in range(n_t)]
                         + [jnp.zeros((8 - n_t, FOX_W), F32)], axis=0)
    s_new = _dot_nt(qexp, kn) + cumcol - jnp.concatenate([cum_t] * n_t, axis=0)
    tq = lax.broadcasted_iota(jnp.int32, (rows, 8), 0) // 8
    tk = lax.broadcasted_iota(jnp.int32, (rows, 8), 1)
    s_new = jnp.where(tk <= tq, s_new, -jnp.inf)

    for c in copies(b, slot):
        c.wait()

    s_pg = []
    for pg in range(n_pages):
        bias = jnp.concatenate([dsuf[pg * 8:(pg + 1) * 8, :]] * n_t, axis=0) + cumcol
        s_pg.append(_dot(qexp, kbuf[slot, pg]) + bias)
    m = s_pg[0]
    for s in s_pg[1:]:
        m = jnp.maximum(m, s)
    m = jnp.maximum(jnp.max(m, axis=-1, keepdims=True), jnp.max(s_new, axis=-1, keepdims=True))
    p_new = jnp.exp(s_new - m)
    l = jnp.sum(p_new, axis=-1, keepdims=True)
    acc = _dot(p_new, vn)
    lsum = None
    for pg in range(n_pages):
        pr = jnp.exp(s_pg[pg] - m)
        lsum = pr if lsum is None else lsum + pr
        acc = acc + _dot_nt(pr, vbuf[slot, pg])
    l = l + jnp.sum(lsum, axis=-1, keepdims=True)
    o = acc / l
    for t in range(n_t):
        ot = jnp.where(headmask, o[t * 8:(t + 1) * 8, :], 0.0)
        o_ref[t, pl.ds(b, 1), :] = jnp.sum(ot, axis=0, keepdims=True)


def _fox_s(tbl, q, kr, vr, cum, lfpool, mstrict, pgsuf, k_cache, v_cache, layer):
    n_t, nb, _ = q.shape
    n_pages = tbl.shape[0] // nb
    whole = lambda a: pl.BlockSpec(a.shape, lambda b, tbl_: (0,) * a.ndim)
    grid_spec = pltpu.PrefetchScalarGridSpec(
        num_scalar_prefetch=1, grid=(nb,),
        in_specs=[whole(q), whole(kr), whole(vr), whole(cum),
                  pl.BlockSpec((1,) + lfpool.shape[1:], lambda b, tbl_: (layer, 0, 0, 0)),
                  whole(mstrict), whole(pgsuf),
                  pl.BlockSpec(memory_space=pl.ANY), pl.BlockSpec(memory_space=pl.ANY)],
        out_specs=pl.BlockSpec((n_t, nb, FOX_W), lambda b, tbl_: (0, 0, 0)),
        scratch_shapes=[pltpu.VMEM((2, n_pages, FOX_W, LANES), F32),
                        pltpu.VMEM((2, n_pages, FOX_W, LANES), F32),
                        pltpu.VMEM((n_pages * 8, LANES), F32),
                        pltpu.SemaphoreType.DMA((2, 2))])
    return pl.pallas_call(
        functools.partial(_fox_s_kernel, layer=layer, n_pages=n_pages, n_t=n_t),
        out_shape=jax.ShapeDtypeStruct((n_t, nb, FOX_W), F32), grid_spec=grid_spec,
        compiler_params=_params(("arbitrary",)), name="fox_s")(
            tbl, q, kr, vr, cum, lfpool, mstrict, pgsuf, k_cache, v_cache)


def _ssd_p_kernel(z_ref, xbc_ref, smr_ref, smt_ref, arow_ref, acol_ref, ltri_ref, utri_ref, dvec_ref,
                  nrm_ref, o_ref, h_ref, ybuf):
    j = pl.program_id(1)

    @pl.when(j == 0)
    def _():
        h_ref[...] = jnp.zeros(h_ref.shape, F32)

    smr = smr_ref[0]
    cs_rows = _dot_hl_left(ltri_ref[...], smr * arow_ref[...])
    cs_t = _dot_hl(smt_ref[0] * acol_ref[...], utri_ref[...])
    end = cs_rows[CHUNK - 1:CHUNK, :]
    xbc = xbc_ref[0]
    x = xbc[:, 0:SSD_W]
    tril = (lax.broadcasted_iota(jnp.int32, (CHUNK, CHUNK), 1)
            <= lax.broadcasted_iota(jnp.int32, (CHUNK, CHUNK), 0))
    rep = SSD_HEADS // SSD_GROUPS
    for g in range(SSD_GROUPS):
        bg = xbc[:, SSD_W + g * SSD_STATE:SSD_W + (g + 1) * SSD_STATE].astype(BF16)
        cg = xbc[:, SSD_W + SSD_BC + g * SSD_STATE:SSD_W + SSD_BC + (g + 1) * SSD_STATE].astype(BF16)
        cb = _dot_nt(cg, bg)
        for hh in range(g * rep, (g + 1) * rep):
            col = SM_DT + hh
            csc = cs_rows[:, col:col + 1]
            e = end[:, col:col + 1]
            lm = jnp.exp(jnp.where(tril, csc - cs_t[col:col + 1, :], -jnp.inf))
            xh = x[:, hh * SSD_HD:(hh + 1) * SSD_HD]
            xdt = xh * smr[:, col:col + 1]
            hst = h_ref[0, hh]
            y = _dot(cb * lm, xdt) + _dot_nt(cg, hst) * jnp.exp(csc)
            h_ref[0, hh] = hst * jnp.exp(e) + _dot_tn(xdt * jnp.exp(e - csc), bg)
            ybuf[:, hh * SSD_HD:(hh + 1) * SSD_HD] = y
    y = (ybuf[...] + x * dvec_ref[...]) * _silu(z_ref[0])
    gw = SSD_W // SSD_GROUPS
    for g in range(SSD_GROUPS):
        yg = y[:, g * gw:(g + 1) * gw]
        yn = yg * lax.rsqrt(jnp.mean(yg * yg, axis=-1, keepdims=True) + EPS)
        o_ref[0, :, g * gw:(g + 1) * gw] = (yn * nrm_ref[:, g * gw:(g + 1) * gw]).astype(BF16)


def _ssd_p(z, xbc, smr, smt, arow, acol, ltri, utri, dvec, nrm):
    nb, t_pad, _ = z.shape
    nj = t_pad // CHUNK
    rows = lambda w_: pl.BlockSpec((1, CHUNK, w_), lambda b, j: (b, j, 0))
    return pl.pallas_call(
        _ssd_p_kernel,
        out_shape=[jax.ShapeDtypeStruct((nb, t_pad, SSD_W), BF16),
                   jax.ShapeDtypeStruct((nb, SSD_HEADS, SSD_HD, SSD_STATE), F32)],
        grid=(nb, nj),
        in_specs=[rows(SSD_W), rows(SSD_CONV_DIM), rows(LANES),
                  pl.BlockSpec((1, 16, CHUNK), lambda b, j: (b, 0, j)),
                  _const_spec((1, LANES)), _const_spec((16, 1)), _const_spec((CHUNK, CHUNK)),
                  _const_spec((CHUNK, CHUNK)), _const_spec((1, SSD_W)), _const_spec((1, SSD_W))],
        out_specs=[rows(SSD_W),
                   pl.BlockSpec((1, SSD_HEADS, SSD_HD, SSD_STATE), lambda b, j: (b, 0, 0, 0))],
        scratch_shapes=[pltpu.VMEM((CHUNK, SSD_W), F32)],
        compiler_params=_params(("parallel", "arbitrary")), name="ssd_p")(
            z, xbc, smr, smt, arow, acol, ltri, utri, dvec, nrm)


def _ssd_s_kernel(x_ref, b_ref, c_ref, sm_ref, xg_ref, zg_ref, a_ref, d_ref, nrm_ref, st_ref,
                  o_ref, so_ref, ybuf, *, n_t):
    hh = pl.program_id(0)
    rep = SSD_HEADS // SSD_GROUPS
    gw = SSD_W // SSD_GROUPS
    dt = [sm_ref[t, pl.ds(SM_DT + hh, 1), :] for t in range(n_t)]
    dec = [jnp.exp(dt[t] * a_ref[pl.ds(hh, 1), :]) for t in range(n_t)]

    def body(p, carry):
        hp = st_ref[0, 0, p]
        for t in range(n_t):
            hp = hp * dec[t] + b_ref[t] * (x_ref[t, pl.ds(p, 1), :] * dt[t])
            ybuf[t, pl.ds(hh * SSD_HD + p, 1), :] = jnp.sum(c_ref[t] * hp, axis=0, keepdims=True)
        so_ref[0, 0, p] = hp
        return carry

    lax.fori_loop(0, SSD_HD, body, 0)

    @pl.when(lax.rem(hh, rep) == rep - 1)
    def _():
        r0 = pl.multiple_of((hh // rep) * gw, gw)
        for t in range(n_t):
            y = ybuf[t, pl.ds(r0, gw), :] + xg_ref[t] * d_ref[pl.ds(r0, gw), :]
            y = y * _silu(zg_ref[t])
            yn = y * lax.rsqrt(jnp.mean(y * y, axis=0, keepdims=True) + EPS)
            o_ref[t, pl.ds(r0, gw), :] = yn * nrm_ref[pl.ds(r0, gw), :]


def _ssd_s(pt, a_b, d_b, nrm_b, state, layer):
    n_t = pt.shape[0]
    rep = SSD_HEADS // SSD_GROUPS
    gw = SSD_W // SSD_GROUPS
    blk = lambda h_, f: pl.BlockSpec((n_t, h_, LANES), lambda hh: (0, f(hh), 0))
    st_spec = pl.BlockSpec((1, 1, SSD_HD, SSD_STATE, LANES), lambda hh: (layer, hh, 0, 0, 0))
    return pl.pallas_call(
        functools.partial(_ssd_s_kernel, n_t=n_t),
        out_shape=[jax.ShapeDtypeStruct((n_t, SSD_W, LANES), F32),
                   jax.ShapeDtypeStruct((1, SSD_HEADS, SSD_HD, SSD_STATE, LANES), F32)],
        grid=(SSD_HEADS,),
        in_specs=[blk(SSD_HD, lambda hh: R_XBC // SSD_HD + hh),
                  blk(SSD_STATE, lambda hh: (R_XBC + SSD_W) // SSD_STATE + hh // rep),
                  blk(SSD_STATE, lambda hh: (R_XBC + SSD_W + SSD_BC) // SSD_STATE + hh // rep),
                  blk(LANES, lambda hh: R_SM // LANES),
                  blk(gw, lambda hh: R_XBC // gw + hh // rep),
                  blk(gw, lambda hh: R_Z // gw + hh // rep),
                  _const_spec((SSD_HEADS, LANES)), _const_spec((SSD_W, LANES)),
                  _const_spec((SSD_W, LANES)), st_spec],
        out_specs=[_const_spec((n_t, SSD_W, LANES)),
                   pl.BlockSpec((1, 1, SSD_HD, SSD_STATE, LANES), lambda hh: (0, hh, 0, 0, 0))],
        scratch_shapes=[pltpu.VMEM((n_t, SSD_W, LANES), F32)],
        compiler_params=_params(("arbitrary",)), name="ssd_s")(
            pt, pt, pt, pt, pt, pt, a_b, d_b, nrm_b, state)


def _gla_p_kernel(gqk_ref, gv_ref, gg_ref, smr_ref, wg_ref, gb_ref, bl_ref, be_ref, g64_ref, gn_ref,
                  o_ref, s_ref, st_scr, *, t_real):
    j = pl.program_id(1)
    nj = pl.num_programs(1)
    sub = GLA_SUB

    @pl.when(j == 0)
    def _():
        st_scr[...] = jnp.zeros(st_scr.shape, F32)

    row = lax.broadcasted_iota(jnp.int32, (CHUNK, GLA_KW), 0)
    valid = j * CHUNK + row < t_real
    glog = _log_sigmoid(_dot(smr_ref[0], wg_ref[...]) + gb_ref[...]) * (1.0 / GLA_TAU)
    glog = jnp.where(valid, glog, 0.0)
    gqk = gqk_ref[0]
    gq = gqk[:, 0:GLA_KW] * (GLA_DK ** -0.5)
    gk = jnp.where(valid, gqk[:, GLA_KW:2 * GLA_KW], 0.0)
    v = gv_ref[0]
    bcl = _dot_hl_left(bl_ref[...], glog)
    tot = _dot_hl_left(be_ref[...], glog)
    qe = (gq * jnp.exp(bcl)).astype(BF16)
    kp = gk * jnp.exp(-bcl)
    kend = (gk * jnp.exp(tot - bcl)).astype(BF16)
    dec = jnp.exp(tot)

    r_k = lax.broadcasted_iota(jnp.int32, (GLA_HEADS * sub, GLA_KW), 0) // sub
    c_k = lax.broadcasted_iota(jnp.int32, (GLA_HEADS * sub, GLA_KW), 1) // GLA_DK
    r_v = lax.broadcasted_iota(jnp.int32, (GLA_HEADS * sub, GLA_W), 0) // sub
    c_v = lax.broadcasted_iota(jnp.int32, (GLA_HEADS * sub, GLA_W), 1) // GLA_DV
    r_s = lax.broadcasted_iota(jnp.int32, (GLA_W, GLA_KW), 0) // GLA_DV
    c_s = lax.broadcasted_iota(jnp.int32, (GLA_W, GLA_KW), 1) // GLA_DK
    causal = (lax.broadcasted_iota(jnp.int32, (sub, GLA_HEADS * sub), 1) % sub
              <= lax.broadcasted_iota(jnp.int32, (sub, GLA_HEADS * sub), 0))
    outs = []
    for i in range(CHUNK // sub):
        sl = slice(i * sub, (i + 1) * sub)
        kbd = jnp.where(r_k == c_k, jnp.concatenate([kp[sl]] * GLA_HEADS, axis=0), 0.0)
        att = jnp.where(causal, _dot_nt(qe[sl], kbd), 0.0)
        vbd = jnp.where(r_v == c_v, jnp.concatenate([v[sl]] * GLA_HEADS, axis=0), 0.0)
        st = st_scr[...]
        outs.append(_dot(att, vbd) + _dot_nt(qe[sl], st))
        upd = jnp.where(r_s == c_s, _dot_tn(v[sl], kend[sl]), 0.0)
        st_scr[...] = st * dec[i * sub:i * sub + 1, :] + upd
    o = jnp.concatenate(outs, axis=0)
    msq = _dot_hl(o * o, g64_ref[...])
    o_ref[0] = (o * lax.rsqrt(msq + EPS) * gn_ref[...] * _silu(gg_ref[0])).astype(BF16)

    @pl.when(j == nj - 1)
    def _():
        s_kv = st_scr[...].T
        for h in range(GLA_HEADS):
            s_ref[0, h] = s_kv[h * GLA_DK:(h + 1) * GLA_DK, h * GLA_DV:(h + 1) * GLA_DV]


def _gla_p(gqk, gv, gg, smr, wgp, gb, bl, be, g64, gn, t_real):
    nb, t_pad, _ = gv.shape
    nj = t_pad // CHUNK
    rows = lambda w_: pl.BlockSpec((1, CHUNK, w_), lambda b, j: (b, j, 0))
    return pl.pallas_call(
        functools.partial(_gla_p_kernel, t_real=t_real),
        out_shape=[jax.ShapeDtypeStruct((nb, t_pad, GLA_W), BF16),
                   jax.ShapeDtypeStruct((nb, GLA_HEADS, GLA_DK, GLA_DV), F32)],
        grid=(nb, nj),
        in_specs=[rows(2 * GLA_KW), rows(GLA_W), rows(GLA_W), rows(LANES),
                  _const_spec((LANES, GLA_KW)), _const_spec((1, GLA_KW)),
                  _const_spec((CHUNK, CHUNK)), _const_spec((CHUNK, CHUNK)),
                  _const_spec((GLA_W, GLA_W)), _const_spec((1, GLA_W))],
        out_specs=[rows(GLA_W),
                   pl.BlockSpec((1, GLA_HEADS, GLA_DK, GLA_DV), lambda b, j: (b, 0, 0, 0))],
        scratch_shapes=[pltpu.VMEM((GLA_W, GLA_KW), F32)],
        compiler_params=_params(("parallel", "arbitrary")), name="gla_p")(
            gqk, gv, gg, smr, wgp, gb, bl, be, g64, gn)


def _gla_s_kernel(q_ref, k_ref, v_ref, gg_ref, sm_ref, wgt_ref, gb_ref, gn_ref, st_ref,
                  o_ref, so_ref, eg_scr, acc_scr, *, n_t):
    for t in range(n_t):
        glog = _log_sigmoid(_dot(wgt_ref[...], sm_ref[t]) + gb_ref[...]) * (1.0 / GLA_TAU)
        eg_scr[t] = jnp.exp(glog)
        acc_scr[t] = jnp.zeros((GLA_DV, LANES), F32)

    def body(kk, carry):
        s = st_ref[0, 0, kk]
        for t in range(n_t):
            s = s * eg_scr[t, pl.ds(kk, 1), :] + k_ref[t, pl.ds(kk, 1), :] * v_ref[t]
            acc_scr[t] = acc_scr[t] + q_ref[t, pl.ds(kk, 1), :] * s
        so_ref[0, 0, kk] = s
        return carry

    lax.fori_loop(0, GLA_DK, body, 0)
    for t in range(n_t):
        o = acc_scr[t]
        on = o * lax.rsqrt(jnp.mean(o * o, axis=0, keepdims=True) + EPS)
        o_ref[t] = on * gn_ref[...] * _silu(gg_ref[t])


def _gla_s(pt, wgt, gb_b, gn_b, state, layer):
    n_t = pt.shape[0]
    blk = lambda h_, f: pl.BlockSpec((n_t, h_, LANES), lambda hh: (0, f(hh), 0))
    return pl.pallas_call(
        functools.partial(_gla_s_kernel, n_t=n_t),
        out_shape=[jax.ShapeDtypeStruct((n_t, GLA_W, LANES), F32),
                   jax.ShapeDtypeStruct((1, GLA_HEADS, GLA_DK, GLA_DV, LANES), F32)],
        grid=(GLA_HEADS,),
        in_specs=[blk(GLA_DK, lambda hh: R_GQ // GLA_DK + hh),
                  blk(GLA_DK, lambda hh: R_GK // GLA_DK + hh),
                  blk(GLA_DV, lambda hh: R_GV // GLA_DV + hh),
                  blk(GLA_DV, lambda hh: R_GG // GLA_DV + hh),
                  blk(LANES, lambda hh: R_SM // LANES),
                  pl.BlockSpec((GLA_DK, LANES), lambda hh: (hh, 0)),
                  pl.BlockSpec((GLA_DK, LANES), lambda hh: (hh, 0)),
                  _const_spec((GLA_DV, LANES)),
                  pl.BlockSpec((1, 1, GLA_DK, GLA_DV, LANES), lambda hh: (layer, hh, 0, 0, 0))],
        out_specs=[blk(GLA_DV, lambda hh: hh),
                   pl.BlockSpec((1, 1, GLA_DK, GLA_DV, LANES), lambda hh: (0, hh, 0, 0, 0))],
        scratch_shapes=[pltpu.VMEM((n_t, GLA_DK, LANES), F32), pltpu.VMEM((n_t, GLA_DV, LANES), F32)],
        compiler_params=_params(("arbitrary",)), name="gla_s")(
            pt, pt, pt, pt, pt, wgt, gb_b, gn_b, state)


def _tri_consts(n_pages):
    i = np.arange(CHUNK)
    ltri = (i[None, :] <= i[:, None]).astype(np.float32)
    utri = ltri.T
    mstrict = (i[:, None] > i[None, :]).astype(np.float32)
    same = (i[:, None] // GLA_SUB) == (i[None, :] // GLA_SUB)
    bl = (same & (i[None, :] <= i[:, None])).astype(np.float32)
    be = same.astype(np.float32)
    r = np.arange(n_pages * 8)
    pgsuf = ((r[:, None] % 8 == r[None, :] % 8) & (r[None, :] // 8 > r[:, None] // 8)).astype(np.float32)
    h = np.arange(FOX_W)
    g64 = ((h[:, None] // 64) == (h[None, :] // 64)).astype(np.float32) / 64.0
    c = lambda a: jnp.asarray(a, BF16)
    return dict(ltri=c(ltri), utri=c(utri), mstrict=c(mstrict), bl=c(bl), be=c(be), pgsuf=c(pgsuf),
                g64=c(g64))


def _pad_lanes(v, offset):
    return jnp.zeros((1, LANES), F32).at[0, offset:offset + v.shape[0]].set(v)


def kernel(x_prompt, x_sample, cache_fox_k, cache_fox_v, cache_fox_logf, state_ssm, state_conv,
           state_gla, page_table, meta_tokens, ffn1_norm, ffn1_w_in, ffn1_w_out, mix_norm, w_mix_in,
           fox_q_norm, fox_k_norm, fox_f_bias, ssd_conv_w, ssd_conv_b, ssd_dt_bias, ssd_a_log, ssd_d,
           ssd_norm, gla_w_gate, gla_gate_bias, gla_norm, w_mix_out, ffn2_norm, ffn2_w_in, ffn2_w_out):
    nbp, seq, _ = x_prompt.shape
    nbs, n_t, _ = x_sample.shape
    depth = ffn1_norm.shape[0]
    assert nbs == LANES and seq % CHUNK == 0
    t_real = N_META + seq
    t_pad = -(-t_real // CHUNK) * CHUNK
    n_pool, page_size = cache_fox_k.shape[1], cache_fox_k.shape[2]
    assert page_size == LANES and n_pool % 2 == 0
    n_pages = page_table.shape[1]
    consts = _tri_consts(n_pages)

    meta = jnp.broadcast_to(meta_tokens[None], (nbp, N_META, D_MODEL))
    xp = jnp.concatenate([meta, x_prompt, jnp.zeros((nbp, t_pad - t_real, D_MODEL), F32)], axis=1)
    xp = xp.reshape(nbp * t_pad, D_MODEL)
    xs = jnp.transpose(x_sample, (1, 0, 2)).reshape(n_t * nbs, D_MODEL)

    k_cache = jnp.transpose(cache_fox_k, (0, 1, 3, 4, 2)).reshape(depth, n_pool, FOX_W, page_size)
    v_cache = jnp.transpose(cache_fox_v, (0, 1, 3, 4, 2)).reshape(depth, n_pool, FOX_W, page_size)
    lfpool = jnp.transpose(cache_fox_logf, (0, 1, 3, 2)).reshape(depth, n_pool // 2, 8, page_size)
    ssm_in = jnp.transpose(state_ssm, (0, 2, 3, 4, 1))
    gla_in = jnp.transpose(state_gla, (0, 2, 3, 4, 1))
    conv_in = jnp.transpose(state_conv, (0, 2, 1, 3))
    tbl = page_table.reshape(-1).astype(jnp.int32)

    outs_p = [[] for _ in range(6)]
    outs_s = [[] for _ in range(6)]
    for l in range(depth):
        w1i, w1o = ffn1_w_in[l].astype(BF16), ffn1_w_out[l].astype(BF16)
        w2i, w2o = ffn2_w_in[l].astype(BF16), ffn2_w_out[l].astype(BF16)
        wmo = w_mix_out[l].astype(BF16)
        fq, fk, fv, ff, sz, sxbc, sdt, gq, gk, gv, glr, gg = jnp.split(w_mix_in[l], IN_SPLITS, axis=-1)
        small = jnp.concatenate(
            [ff, sdt, glr, jnp.zeros((D_MODEL, LANES - FOX_HEADS - SSD_HEADS - GLA_RANK), F32)], axis=-1)
        wproj = jnp.concatenate([fq, fk, fv, sz, sxbc, gq, gk, gv, gg, small], axis=-1).astype(BF16)
        n1 = ffn1_norm[l][None]
        n2 = ffn2_norm[l][None]
        nm = mix_norm[l][None]
        qg = jnp.tile(fox_q_norm[l], FOX_HEADS)[None]
        kg = jnp.tile(fox_k_norm[l], FOX_HEADS)[None]
        fb = _pad_lanes(fox_f_bias[l], SM_F)
        dtb = _pad_lanes(ssd_dt_bias[l], SM_DT)
        cw = jnp.zeros((8, SSD_CONV_DIM), F32).at[0:SSD_CONV].set(ssd_conv_w[l])
        cb = ssd_conv_b[l][None]
        a = -jnp.exp(ssd_a_log[l])
        arow = _pad_lanes(a, SM_DT)
        acol = jnp.zeros((16, 1), F32).at[SM_DT:SM_DT + SSD_HEADS, 0].set(a)
        dvec = jnp.repeat(ssd_d[l], SSD_HD)[None]
        snrm = ssd_norm[l][None]
        wgp = jnp.zeros((LANES, GLA_KW), F32).at[SM_LR:SM_LR + GLA_RANK].set(gla_w_gate[l]).astype(BF16)
        gb = gla_gate_bias[l][None]
        gn = jnp.tile(gla_norm[l], GLA_HEADS)[None]

        xp = _ffn(xp, n1, w1i, w1o, FFN_TM)
        (q, kt, vt, z, xbc, gqk, gvv, ggg, smr, smt, ft, convp) = _proj_p(
            xp.reshape(nbp, t_pad, D_MODEL), nm, wproj, qg, kg, fb, dtb, cw, cb, consts["g64"],
            consts["ltri"], t_real)
        fox_o = _fox_p(q, kt, vt, smr, ft)
        ssd_o, ssm_p = _ssd_p(z, xbc, smr, smt, arow, acol, consts["ltri"], consts["utri"], dvec, snrm)
        gla_o, gla_p = _gla_p(gqk, gvv, ggg, smr, wgp, gb, consts["bl"], consts["be"], consts["g64"],
                              gn, t_real)
        xp = _mix_ffn(xp, fox_o.reshape(-1, FOX_W), ssd_o.reshape(-1, SSD_W), gla_o.reshape(-1, GLA_W),
                      wmo, n2, w2i, w2o, FFN_TM)
        outs_p[0].append(jnp.transpose(kt.reshape(nbp, FOX_HEADS, FOX_HD, t_pad)[..., :t_real], (0, 3, 1, 2)))
        outs_p[1].append(jnp.transpose(vt.reshape(nbp, FOX_HEADS, FOX_HD, t_pad)[..., :t_real], (0, 3, 1, 2)))
        outs_p[2].append(jnp.transpose(smt[:, 0:FOX_HEADS, :t_real], (0, 2, 1)))
        outs_p[3].append(ssm_p)
        outs_p[4].append(convp[:, 8 - (SSD_CONV - 1):])
        outs_p[5].append(gla_p)

        xs = _ffn(xs, n1, w1i, w1o, LANES)
        qs, krs, vrs, pt, cum, convs = _proj_s(xs, nm, wproj, qg, kg, fb, dtb, cw, cb, consts["g64"],
                                               conv_in[l])
        fox_os = _fox_s(tbl, qs, krs, vrs, cum, lfpool, consts["mstrict"], consts["pgsuf"],
                        k_cache, v_cache, l)
        a_b = jnp.broadcast_to(a[:, None], (SSD_HEADS, LANES))
        d_b = jnp.broadcast_to(jnp.repeat(ssd_d[l], SSD_HD)[:, None], (SSD_W, LANES))
        snrm_b = jnp.broadcast_to(ssd_norm[l][:, None], (SSD_W, LANES))
        ssd_ot, ssm_s = _ssd_s(pt, a_b, d_b, snrm_b, ssm_in, l)
        wgt = jnp.zeros((GLA_KW, LANES), F32).at[:, SM_LR:SM_LR + GLA_RANK].set(gla_w_gate[l].T).astype(BF16)
        gb_b = jnp.broadcast_to(gla_gate_bias[l][:, None], (GLA_KW, LANES))
        gn_b = jnp.broadcast_to(gla_norm[l][:, None], (GLA_DV, LANES))
        gla_ot, gla_s = _gla_s(pt, wgt, gb_b, gn_b, gla_in, l)
        xs = _mix_ffn_t(xs, fox_os, ssd_ot, gla_ot, wmo, n2, w2i, w2o)
        kts = pt[:, R_K:R_K + FOX_W].reshape(n_t, FOX_HEADS, FOX_HD, nbs)
        vts = pt[:, R_V:R_V + FOX_W].reshape(n_t, FOX_HEADS, FOX_HD, nbs)
        outs_s[0].append(jnp.transpose(kts, (3, 0, 1, 2)))
        outs_s[1].append(jnp.transpose(vts, (3, 0, 1, 2)))
        outs_s[2].append(jnp.transpose(pt[:, R_SM + SM_F:R_SM + SM_F + FOX_HEADS], (2, 0, 1)))
        outs_s[3].append(jnp.transpose(ssm_s[0], (3, 0, 1, 2)))
        outs_s[4].append(jnp.transpose(convs, (1, 0, 2)))
        outs_s[5].append(jnp.transpose(gla_s[0], (3, 0, 1, 2)))

    y_prompt = xp.reshape(nbp, t_pad, D_MODEL)[:, N_META:t_real]
    y_sample = jnp.transpose(xs.reshape(n_t, nbs, D_MODEL), (1, 0, 2))
    k_p, v_p, lf_p, ssm_p, conv_p, gla_p = [jnp.stack(a) for a in outs_p]
    k_s, v_s, lf_s, ssm_s, conv_s, gla_s = [jnp.stack(a) for a in outs_s]
    return (y_prompt, y_sample, k_p, v_p, lf_p, ssm_p, conv_p, gla_p, k_s, v_s, lf_s, ssm_s, conv_s, gla_s)
```

```python
import functools
import math

import jax
import jax.numpy as jnp
import numpy as np
from jax import lax
from jax.experimental import pallas as pl
from jax.experimental.pallas import tpu as pltpu

F32 = jnp.float32
BF16 = jnp.bfloat16

D_MODEL = 1024
N_META = 16
D_FF = 2816
EPS = 1e-6
FOX_HEADS = 4
FOX_HD = 64
FOX_W = FOX_HEADS * FOX_HD
SSD_HEADS = 8
SSD_HD = 64
SSD_W = SSD_HEADS * SSD_HD
SSD_GROUPS = 2
SSD_STATE = 64
SSD_CONV = 4
SSD_BC = SSD_GROUPS * SSD_STATE
SSD_CONV_DIM = SSD_W + 2 * SSD_BC
GLA_HEADS = 4
GLA_DK = 32
GLA_DV = 64
GLA_KW = GLA_HEADS * GLA_DK
GLA_W = GLA_HEADS * GLA_DV
GLA_RANK = 16
GLA_TAU = 16.0
D_MIX = FOX_W + SSD_W + GLA_W
IN_SIZES = (FOX_W, FOX_W, FOX_W, FOX_HEADS, SSD_W, SSD_CONV_DIM, SSD_HEADS,
            GLA_KW, GLA_KW, GLA_W, GLA_RANK, GLA_W)
IN_SPLITS = tuple(int(v) for v in np.cumsum(IN_SIZES)[:-1])

LOG2E = 1.4426950408889634
FOX_AUG = FOX_HEADS * 128

LANES = 128
CHUNK = 128
GLA_SUB = 32
FOX_KW = 512
FFN_CK = 256
FFN_TM = 512
PROJ_GROUP = 4
VMEM_LIMIT = 56 * 1024 * 1024

C_Q, C_K, C_V = 0, 256, 512
C_Z = 768
C_XBC = 1280
C_GQ, C_GK, C_GV, C_GG = 2048, 2176, 2304, 2560
C_SM = 2816
N_PROJ = 2944
SM_F = 0
SM_DT = 4
SM_LR = 12
R_K, R_V, R_Z, R_XBC = 0, 256, 512, 1024
R_GQ, R_GK, R_GV, R_GG, R_SM = 1792, 1920, 2048, 2304, 2560
N_PT = 2688


def _dot(a, b):
    return jnp.dot(a.astype(BF16), b.astype(BF16), preferred_element_type=F32)


def _dot_nt(a, b):
    return lax.dot_general(a.astype(BF16), b.astype(BF16), (((1,), (1,)), ((), ())),
                           preferred_element_type=F32)


def _dot_tn(a, b):
    return lax.dot_general(a.astype(BF16), b.astype(BF16), (((0,), (0,)), ((), ())),
                           preferred_element_type=F32)


def _split(a):
    hi = a.astype(BF16)
    lo = (a - hi.astype(F32)).astype(BF16)
    return hi, lo


def _dot_hl(a, m):
    hi, lo = _split(a)
    return (jnp.dot(hi, m, preferred_element_type=F32) + jnp.dot(lo, m, preferred_element_type=F32))


def _dot_hl_left(m, a):
    hi, lo = _split(a)
    return (jnp.dot(m, hi, preferred_element_type=F32) + jnp.dot(m, lo, preferred_element_type=F32))


def _silu(x):
    return x * jax.nn.sigmoid(x)


def _softplus(x):
    return jnp.maximum(x, 0.0) + jnp.log1p(jnp.exp(-jnp.abs(x)))


def _log_sigmoid(x):
    return jnp.minimum(x, 0.0) - jnp.log1p(jnp.exp(-jnp.abs(x)))


def _rms(x, g):
    return x * lax.rsqrt(jnp.mean(x * x, axis=-1, keepdims=True) + EPS) * g


def _const_spec(shape):
    n = len(shape)
    return pl.BlockSpec(shape, lambda *_: (0,) * n)


def _lspec(shape, l):
    n = len(shape)
    return pl.BlockSpec((None,) + tuple(shape), lambda *_: (l,) + (0,) * n)


def _params(sem):
    return pltpu.CompilerParams(dimension_semantics=sem, vmem_limit_bytes=VMEM_LIMIT)


def _swiglu_half(x, g_ref, win_ref, wout_ref):
    h = _rms(x, g_ref[...]).astype(BF16)
    acc = None
    for c in range(D_FF // FFN_CK):
        g = jnp.dot(h, win_ref[:, c * FFN_CK:(c + 1) * FFN_CK], preferred_element_type=F32)
        u = jnp.dot(h, win_ref[:, D_FF + c * FFN_CK:D_FF + (c + 1) * FFN_CK],
                    preferred_element_type=F32)
        a = (_silu(g) * u).astype(BF16)
        part = jnp.dot(a, wout_ref[c * FFN_CK:(c + 1) * FFN_CK, :], preferred_element_type=F32)
        acc = part if acc is None else acc + part
    return x + 0.5 * acc


def _ffn_kernel(x_ref, g_ref, win_ref, wout_ref, o_ref):
    o_ref[...] = _swiglu_half(x_ref[...], g_ref, win_ref, wout_ref)


def _mix_ffn_kernel(x_ref, fo_ref, so_ref, go_ref, wmo_ref, g_ref, win_ref, wout_ref, o_ref):
    x = x_ref[...]
    x = x + jnp.dot(fo_ref[...], wmo_ref[0:FOX_W, :], preferred_element_type=F32)
    x = x + jnp.dot(so_ref[...], wmo_ref[FOX_W:FOX_W + SSD_W, :], preferred_element_type=F32)
    x = x + jnp.dot(go_ref[...], wmo_ref[FOX_W + SSD_W:D_MIX, :], preferred_element_type=F32)
    o_ref[...] = _swiglu_half(x, g_ref, win_ref, wout_ref)


def _mix_ffn_t_kernel(x_ref, fo_ref, so_ref, go_ref, wmo_ref, g_ref, win_ref, wout_ref, o_ref):
    x = x_ref[...]
    x = x + _dot(fo_ref[0], wmo_ref[0:FOX_W, :])
    x = x + _dot(so_ref[0].T, wmo_ref[FOX_W:FOX_W + SSD_W, :])
    x = x + _dot(go_ref[0].T, wmo_ref[FOX_W + SSD_W:D_MIX, :])
    o_ref[...] = _swiglu_half(x, g_ref, win_ref, wout_ref)


def _ffn_weight_specs(l):
    return [_lspec((1, D_MODEL), l), _lspec((D_MODEL, 2 * D_FF), l), _lspec((D_FF, D_MODEL), l)]


def _ffn(x, g, win, wout, tm, l):
    rows = x.shape[0]
    row = lambda w: pl.BlockSpec((tm, w), lambda i: (i, 0))
    return pl.pallas_call(
        _ffn_kernel, out_shape=jax.ShapeDtypeStruct((rows, D_MODEL), F32), grid=(rows // tm,),
        in_specs=[row(D_MODEL)] + _ffn_weight_specs(l), out_specs=row(D_MODEL),
        compiler_params=_params(("parallel",)), name="ffn")(x, g, win, wout)


def _mix_ffn(x, fo, so, go, wmo, g, win, wout, tm, l):
    rows = x.shape[0]
    row = lambda w: pl.BlockSpec((tm, w), lambda i: (i, 0))
    return pl.pallas_call(
        _mix_ffn_kernel, out_shape=jax.ShapeDtypeStruct((rows, D_MODEL), F32), grid=(rows // tm,),
        in_specs=[row(D_MODEL), row(FOX_W), row(SSD_W), row(GLA_W), _lspec((D_MIX, D_MODEL), l)]
        + _ffn_weight_specs(l), out_specs=row(D_MODEL),
        compiler_params=_params(("parallel",)), name="mix_ffn")(x, fo, so, go, wmo, g, win, wout)


def _mix_ffn_t(x, fo, so_t, go_t, wmo, g, win, wout, l):
    rows = x.shape[0]
    n_t = rows // LANES
    row = pl.BlockSpec((LANES, D_MODEL), lambda i: (i, 0))
    per_t = lambda w, n: pl.BlockSpec((1, w, n), lambda i: (i, 0, 0))
    return pl.pallas_call(
        _mix_ffn_t_kernel, out_shape=jax.ShapeDtypeStruct((rows, D_MODEL), F32), grid=(n_t,),
        in_specs=[row, per_t(LANES, FOX_W), per_t(SSD_W, LANES), per_t(GLA_W, LANES),
                  _lspec((D_MIX, D_MODEL), l)] + _ffn_weight_specs(l), out_specs=row,
        compiler_params=_params(("parallel",)), name="mix_ffn_t")(x, fo, so_t, go_t, wmo, g, win, wout)


def _head_norm(x, gain_row, g64_ref):
    msq = _dot_hl(x * x, g64_ref[...])
    return x * lax.rsqrt(msq + EPS) * gain_row


def _small_block(sm, fb_ref, dtb_ref):
    logf = _log_sigmoid(sm + fb_ref[...])
    dt = _softplus(sm + dtb_ref[...])
    return logf, dt


def _proj_p_kernel(x_ref, nrm_ref, w_ref, qg_ref, kg_ref, fb_ref, dtb_ref, cw_ref, cb_ref, g64_ref,
                   ltri_ref,
                   qat_ref, ka_ref, kt_ref, vt_ref, vt16_ref, z_ref, xbc_ref, gqk_ref, gv_ref, gg_ref,
                   smr_ref, smt_ref, convp_ref,
                   xb_scr, carry_scr, *, t_real, grp):
    j = pl.program_id(1)

    @pl.when(j == 0)
    def _():
        xb_scr[:, 0:8, :] = jnp.zeros((grp, 8, SSD_CONV_DIM), F32)
        carry_scr[...] = jnp.zeros((grp, 8, LANES), F32)

    h = _rms(x_ref[...].reshape(grp * CHUNK, D_MODEL), nrm_ref[...]).astype(BF16)
    proj = lambda c0, n: jnp.dot(h, w_ref[:, c0:c0 + n], preferred_element_type=F32)
    seq = lambda a, g: a[g * CHUNK:(g + 1) * CHUNK]

    sm_all = proj(C_SM, LANES)
    lane = lax.broadcasted_iota(jnp.int32, (CHUNK, LANES), 1)
    row = lax.broadcasted_iota(jnp.int32, (CHUNK, LANES), 0)
    logf_all, dt_all = _small_block(sm_all, fb_ref, dtb_ref)
    f_rows = []
    for g in range(grp):
        logf = jnp.where(lane < SM_DT, seq(logf_all, g), 0.0)
        dt = jnp.where(j * CHUNK + row < t_real, seq(dt_all, g), 0.0)
        f = _dot_hl_left(ltri_ref[...], logf) + carry_scr[g, 0:1, :]
        carry_scr[g] = jnp.broadcast_to(f[CHUNK - 1:CHUNK, :], (8, LANES))
        smr_ref[g] = jnp.where(lane < SM_LR, dt, seq(sm_all, g))
        smt_ref[g] = jnp.where(lane < SM_DT, logf, dt).T[0:16, :]
        f_rows.append(f)

    q_all = _head_norm(proj(C_Q, FOX_W), qg_ref[...], g64_ref) * (FOX_HD ** -0.5 * LOG2E)
    k_all = _head_norm(proj(C_K, FOX_W), kg_ref[...], g64_ref)
    v_all = proj(C_V, FOX_W)
    l64 = lax.broadcasted_iota(jnp.int32, (CHUNK, FOX_HD), 1)
    for g in range(grp):
        q, k = seq(q_all, g), seq(k_all, g)
        qa, ka = [], []
        for hh in range(FOX_HEADS):
            y = f_rows[g][:, hh:hh + 1] * LOG2E
            hi = y.astype(BF16).astype(F32)
            mid = (y - hi).astype(BF16).astype(F32)
            lo = y - hi - mid
            parts = jnp.where(l64 == 0, hi, jnp.where(l64 == 1, mid, lo))
            ext_q = jnp.where(l64 < 3, parts, jnp.where(l64 < 6, 1.0, 0.0))
            parts = jnp.where(l64 == 3, hi, jnp.where(l64 == 4, mid, lo))
            ext_k = jnp.where(l64 < 3, 1.0, jnp.where(l64 < 6, -parts, 0.0))
            qa += [q[:, hh * FOX_HD:(hh + 1) * FOX_HD], ext_q]
            ka += [k[:, hh * FOX_HD:(hh + 1) * FOX_HD], ext_k]
        qat_ref[g] = jnp.concatenate(qa, axis=1).T.astype(BF16)
        ka_ref[g] = jnp.concatenate(ka, axis=1).astype(BF16)
        kt_ref[g] = k.T
        vt = seq(v_all, g).T
        vt_ref[g] = vt
        vt16_ref[g] = vt.astype(BF16)

    xbc_all = proj(C_XBC, SSD_CONV_DIM)
    r_end = t_real - (t_real - 1) // CHUNK * CHUNK
    for g in range(grp):
        xb_scr[g, 8:8 + CHUNK, :] = seq(xbc_all, g)
        conv = cb_ref[...]
        for w in range(SSD_CONV):
            conv = conv + xb_scr[g, pl.ds(8 - (SSD_CONV - 1) + w, CHUNK), :] * cw_ref[w:w + 1, :]
        xbc_ref[g] = _silu(conv)
        convp_ref[g] = xb_scr[g, r_end:r_end + 8, :]
        xb_scr[g, 0:8, :] = xb_scr[g, CHUNK:CHUNK + 8, :]

    z_ref[...] = proj(C_Z, SSD_W).reshape(grp, CHUNK, SSD_W)
    gqk_ref[...] = proj(C_GQ, 2 * GLA_KW).reshape(grp, CHUNK, 2 * GLA_KW)
    gv_ref[...] = proj(C_GV, GLA_W).reshape(grp, CHUNK, GLA_W)
    gg_ref[...] = proj(C_GG, GLA_W).reshape(grp, CHUNK, GLA_W)


def _proj_p(x3, nrm, w, qg, kg, fb, dtb, cw, cb, g64, ltri, t_real, l):
    nb, t_pad, _ = x3.shape
    nj = t_pad // CHUNK
    grp = math.gcd(nb, PROJ_GROUP)
    rows = lambda w_: pl.BlockSpec((grp, CHUNK, w_), lambda b, j: (b, j, 0))
    cols = lambda h_: pl.BlockSpec((grp, h_, CHUNK), lambda b, j: (b, 0, j))
    out_shape = [
        jax.ShapeDtypeStruct((nb, FOX_AUG, t_pad), BF16),
        jax.ShapeDtypeStruct((nb, t_pad, FOX_AUG), BF16),
        jax.ShapeDtypeStruct((nb, FOX_W, t_pad), F32),
        jax.ShapeDtypeStruct((nb, FOX_W, t_pad), F32),
        jax.ShapeDtypeStruct((nb, FOX_W, t_pad), BF16),
        jax.ShapeDtypeStruct((nb, t_pad, SSD_W), F32),
        jax.ShapeDtypeStruct((nb, t_pad, SSD_CONV_DIM), F32),
        jax.ShapeDtypeStruct((nb, t_pad, 2 * GLA_KW), F32),
        jax.ShapeDtypeStruct((nb, t_pad, GLA_W), F32),
        jax.ShapeDtypeStruct((nb, t_pad, GLA_W), F32),
        jax.ShapeDtypeStruct((nb, t_pad, LANES), F32),
        jax.ShapeDtypeStruct((nb, 16, t_pad), F32),
        jax.ShapeDtypeStruct((nb, 8, SSD_CONV_DIM), F32),
    ]
    out_specs = [cols(FOX_AUG), rows(FOX_AUG), cols(FOX_W), cols(FOX_W), cols(FOX_W), rows(SSD_W),
                 rows(SSD_CONV_DIM), rows(2 * GLA_KW), rows(GLA_W), rows(GLA_W), rows(LANES), cols(16),
                 pl.BlockSpec((grp, 8, SSD_CONV_DIM), lambda b, j: (b, 0, 0))]
    in_specs = [rows(D_MODEL), _lspec((1, D_MODEL), l), _lspec((D_MODEL, N_PROJ), l),
                _lspec((1, FOX_W), l), _lspec((1, FOX_W), l), _lspec((1, LANES), l),
                _lspec((1, LANES), l), _lspec((8, SSD_CONV_DIM), l), _lspec((1, SSD_CONV_DIM), l),
                _const_spec((FOX_W, FOX_W)), _const_spec((CHUNK, CHUNK))]
    return pl.pallas_call(
        functools.partial(_proj_p_kernel, t_real=t_real, grp=grp), out_shape=out_shape,
        grid=(nb // grp, nj), in_specs=in_specs, out_specs=out_specs,
        scratch_shapes=[pltpu.VMEM((grp, 8 + CHUNK, SSD_CONV_DIM), F32),
                        pltpu.VMEM((grp, 8, LANES), F32)],
        compiler_params=_params(("parallel", "arbitrary")), name="proj_p")(
            x3, nrm, w, qg, kg, fb, dtb, cw, cb, g64, ltri)


def _proj_s_kernel(x_ref, nrm_ref, w_ref, qg_ref, kg_ref, fb_ref, dtb_ref, cw_ref, cb_ref, g64_ref,
                   cst_ref,
                   q_ref, kr_ref, vr_ref, pt_ref, cum_ref, convs_ref,
                   hist_scr, carry_scr, *, n_t):
    t = pl.program_id(0)

    @pl.when(t == 0)
    def _():
        hist_scr[0:SSD_CONV - 1] = cst_ref[...]
        carry_scr[...] = jnp.zeros((LANES, LANES), F32)

    h = _rms(x_ref[...], nrm_ref[...]).astype(BF16)
    p = jnp.dot(h, w_ref[...], preferred_element_type=F32)

    q_ref[0] = _head_norm(p[:, C_Q:C_Q + FOX_W], qg_ref[...], g64_ref) * (FOX_HD ** -0.5)
    k = _head_norm(p[:, C_K:C_K + FOX_W], kg_ref[...], g64_ref)
    v = p[:, C_V:C_V + FOX_W]
    kr_ref[0] = k
    vr_ref[0] = v
    for c in range(FOX_W // LANES):
        pt_ref[0, R_K + c * LANES:R_K + (c + 1) * LANES, :] = k[:, c * LANES:(c + 1) * LANES].T
        pt_ref[0, R_V + c * LANES:R_V + (c + 1) * LANES, :] = v[:, c * LANES:(c + 1) * LANES].T

    hist_scr[pl.ds(SSD_CONV - 1 + t, 1)] = p[:, C_XBC:C_XBC + SSD_CONV_DIM][None]
    conv = cb_ref[...]
    for w in range(SSD_CONV):
        conv = conv + hist_scr[t + w] * cw_ref[w:w + 1, :]
    xbc = _silu(conv)

    @pl.when(t == n_t - 1)
    def _():
        convs_ref[...] = hist_scr[n_t:n_t + SSD_CONV - 1]

    sm = p[:, C_SM:C_SM + LANES]
    lane = lax.broadcasted_iota(jnp.int32, (LANES, LANES), 1)
    logf, dt = _small_block(sm, fb_ref, dtb_ref)
    cum = carry_scr[...] + jnp.where(lane < SM_DT, logf, 0.0)
    carry_scr[...] = cum
    cum_ref[0] = cum
    smc = jnp.where(lane < SM_DT, logf, jnp.where(lane < SM_LR, dt, sm))

    def put_t(r0, val):
        for c in range(val.shape[1] // LANES):
            pt_ref[0, r0 + c * LANES:r0 + (c + 1) * LANES, :] = val[:, c * LANES:(c + 1) * LANES].T

    put_t(R_Z, p[:, C_Z:C_Z + SSD_W])
    put_t(R_XBC, xbc)
    put_t(R_GQ, p[:, C_GQ:C_GQ + GLA_KW] * (GLA_DK ** -0.5))
    put_t(R_GK, p[:, C_GK:C_GK + GLA_KW])
    put_t(R_GV, p[:, C_GV:C_GV + GLA_W])
    put_t(R_GG, p[:, C_GG:C_GG + GLA_W])
    put_t(R_SM, smc)


def _proj_s(x, nrm, w, qg, kg, fb, dtb, cw, cb, g64, conv_state, l):
    n_t = x.shape[0] // LANES
    per_t = lambda a, b: pl.BlockSpec((1, a, b), lambda t: (t, 0, 0))
    out_shape = [
        jax.ShapeDtypeStruct((n_t, LANES, FOX_W), F32),
        jax.ShapeDtypeStruct((n_t, LANES, FOX_W), F32),
        jax.ShapeDtypeStruct((n_t, LANES, FOX_W), F32),
        jax.ShapeDtypeStruct((n_t, N_PT, LANES), F32),
        jax.ShapeDtypeStruct((n_t, LANES, LANES), F32),
        jax.ShapeDtypeStruct((SSD_CONV - 1, LANES, SSD_CONV_DIM), F32),
    ]
    out_specs = [per_t(LANES, FOX_W), per_t(LANES, FOX_W), per_t(LANES, FOX_W), per_t(N_PT, LANES),
                 per_t(LANES, LANES), _const_spec((SSD_CONV - 1, LANES, SSD_CONV_DIM))]
    in_specs = [pl.BlockSpec((LANES, D_MODEL), lambda t: (t, 0)), _lspec((1, D_MODEL), l),
                _lspec((D_MODEL, N_PROJ), l), _lspec((1, FOX_W), l), _lspec((1, FOX_W), l),
                _lspec((1, LANES), l), _lspec((1, LANES), l), _lspec((8, SSD_CONV_DIM), l),
                _lspec((1, SSD_CONV_DIM), l), _const_spec((FOX_W, FOX_W)),
                _lspec((SSD_CONV - 1, LANES, SSD_CONV_DIM), l)]
    return pl.pallas_call(
        functools.partial(_proj_s_kernel, n_t=n_t), out_shape=out_shape, grid=(n_t,),
        in_specs=in_specs, out_specs=out_specs,
        scratch_shapes=[pltpu.VMEM((n_t + SSD_CONV - 1, LANES, SSD_CONV_DIM), F32),
                        pltpu.VMEM((LANES, LANES), F32)],
        compiler_params=_params(("arbitrary",)), name="proj_s")(
            x, nrm, w, qg, kg, fb, dtb, cw, cb, g64, conv_state)


def _fox_p_kernel(qat_ref, ka_ref, vt_ref, o_ref, *, kw):
    qi = pl.program_id(1)
    nsub = kw // CHUNK
    n_full = qi // nsub

    def chunk(start, width, carry, masked):
        ss = []
        for h in range(FOX_HEADS):
            ka = ka_ref[0, pl.ds(start, width), h * LANES:(h + 1) * LANES]
            ss.append(jnp.dot(ka, qat_ref[0, h * LANES:(h + 1) * LANES, :], preferred_element_type=F32))
        prs, stats = [], []
        for h in range(FOX_HEADS):
            m, l, _ = carry[h]
            s = ss[h]
            if masked:
                kpos = lax.broadcasted_iota(jnp.int32, (width, CHUNK), 0) + start
                qpos = lax.broadcasted_iota(jnp.int32, (width, CHUNK), 1) + qi * CHUNK
                s = jnp.where(kpos <= qpos, s, -jnp.inf)
            m_new = jnp.maximum(m, jnp.max(s, axis=0, keepdims=True))
            alpha = jnp.exp2(m - m_new)
            pr = jnp.exp2(s - m_new)
            stats.append((m_new, alpha * l + jnp.sum(pr, axis=0, keepdims=True), alpha))
            prs.append(pr.astype(BF16))
        out = []
        for h in range(FOX_HEADS):
            vt = vt_ref[0, h * FOX_HD:(h + 1) * FOX_HD, pl.ds(start, width)]
            m_new, l, alpha = stats[h]
            out.append((m_new, l, alpha * carry[h][2] + jnp.dot(vt, prs[h], preferred_element_type=F32)))
        return tuple(out)

    init = tuple((jnp.full((1, CHUNK), -jnp.inf, F32), jnp.zeros((1, CHUNK), F32),
                  jnp.zeros((FOX_HD, CHUNK), F32)) for _ in range(FOX_HEADS))
    carry = lax.fori_loop(0, n_full, lambda i, c: chunk(pl.multiple_of(i * kw, kw), kw, c, False), init)
    start = pl.multiple_of(n_full * kw, kw)
    for r in range(nsub):
        @pl.when(lax.rem(qi, nsub) == r)
        def _(r=r):
            fin = chunk(start, (r + 1) * CHUNK, carry, True)
            o_t = jnp.concatenate([acc / l for (_, l, acc) in fin], axis=0)
            o_ref[0] = o_t.T.astype(BF16)


def _fox_p(qat, ka, vt16):
    nb, _, t_pad = qat.shape
    nq = t_pad // CHUNK
    return pl.pallas_call(
        functools.partial(_fox_p_kernel, kw=FOX_KW),
        out_shape=jax.ShapeDtypeStruct((nb, t_pad, FOX_W), BF16), grid=(nb, nq),
        in_specs=[pl.BlockSpec((1, FOX_AUG, CHUNK), lambda b, i: (b, 0, i)),
                  pl.BlockSpec((1, t_pad, FOX_AUG), lambda b, i: (b, 0, 0)),
                  pl.BlockSpec((1, FOX_W, t_pad), lambda b, i: (b, 0, 0))],
        out_specs=pl.BlockSpec((1, CHUNK, FOX_W), lambda b, i: (b, i, 0)),
        compiler_params=_params(("parallel", "arbitrary")), name="fox_p")(qat, ka, vt16)


def _fox_s_kernel(tbl_ref, q_ref, kr_ref, vr_ref, cum_ref, lf_ref, mstrict_ref, pgsuf_ref,
                  k_hbm, v_hbm, o_ref, kbuf, vbuf, lfst, sem, *, layer, n_pages, n_t):
    b = pl.program_id(0)
    nb = pl.num_programs(0)
    slot = lax.rem(b, 2)

    def copies(seq, sl):
        out = []
        for pg in range(n_pages):
            page = tbl_ref[seq * n_pages + pg]
            out.append(pltpu.make_async_copy(k_hbm.at[layer, page], kbuf.at[sl, pg], sem.at[0, sl]))
            out.append(pltpu.make_async_copy(v_hbm.at[layer, page], vbuf.at[sl, pg], sem.at[1, sl]))
        return out

    @pl.when(b == 0)
    def _():
        lfst[...] = jnp.zeros(lfst.shape, F32)
        for c in copies(0, 0):
            c.start()

    @pl.when(b + 1 < nb)
    def _():
        for c in copies(b + 1, 1 - slot):
            c.start()

    for pg in range(n_pages):
        page = tbl_ref[b * n_pages + pg]
        lfst[pg * 8:pg * 8 + FOX_HEADS, :] = lf_ref[0, page // 2, pl.ds(lax.rem(page, 2) * FOX_HEADS,
                                                                      FOX_HEADS), :]
    lf = lfst[...]
    within = _dot_hl(lf, mstrict_ref[...])
    tot = jnp.broadcast_to(jnp.sum(lf, axis=-1, keepdims=True), lf.shape)
    dsuf = within + _dot_hl_left(pgsuf_ref[...], tot)

    lane = lax.broadcasted_iota(jnp.int32, (8, FOX_W), 1)
    rowh = lax.broadcasted_iota(jnp.int32, (8, FOX_W), 0)
    headmask = (lane // FOX_HD) == rowh
    qexp, cumcol = [], []
    cum_rows = jnp.concatenate([cum_ref[t, pl.ds(b, 1), :] for t in range(n_t)]
                               + [jnp.zeros((LANES - n_t, LANES), F32)], axis=0)
    cum_t = cum_rows.T[0:8, 0:8]
    for t in range(n_t):
        qrow = q_ref[t, pl.ds(b, 1), :]
        qexp.append(jnp.where(headmask, jnp.broadcast_to(qrow, (8, FOX_W)), 0.0))
        cumcol.append(cum_t[:, t:t + 1])
    qexp = jnp.concatenate(qexp, axis=0).astype(BF16)
    cumcol = jnp.concatenate(cumcol, axis=0)
    rows = 8 * n_t

    kn = jnp.concatenate([kr_ref[t, pl.ds(b, 1), :] for t in range(n_t)]
                         + [jnp.zeros((8 - n_t, FOX_W), F32)], axis=0)
    vn = jnp.concatenate([vr_ref[t, pl.ds(b, 1), :] for t in range(n_t)]
                         + [jnp.zeros((8 - n_t, FOX_W), F32)], axis=0)
    s_new = _dot_nt(qexp, kn) + cumcol - jnp.concatenate([cum_t] * n_t, axis=0)
    tq = lax.broadcasted_iota(jnp.int32, (rows, 8), 0) // 8
    tk = lax.broadcasted_iota(jnp.int32, (rows, 8), 1)
    s_new = jnp.where(tk <= tq, s_new, -jnp.inf)

    for c in copies(b, slot):
        c.wait()

    s_pg = []
    for pg in range(n_pages):
        bias = jnp.concatenate([dsuf[pg * 8:(pg + 1) * 8, :]] * n_t, axis=0) + cumcol
        s_pg.append(_dot(qexp, kbuf[slot, pg]) + bias)
    m = s_pg[0]
    for s in s_pg[1:]:
        m = jnp.maximum(m, s)
    m = jnp.maximum(jnp.max(m, axis=-1, keepdims=True), jnp.max(s_new, axis=-1, keepdims=True))
    p_new = jnp.exp(s_new - m)
    l = jnp.sum(p_new, axis=-1, keepdims=True)
    acc = _dot(p_new, vn)
    lsum = None
    for pg in range(n_pages):
        pr = jnp.exp(s_pg[pg] - m)
        lsum = pr if lsum is None else lsum + pr
        acc = acc + _dot_nt(pr, vbuf[slot, pg])
    l = l + jnp.sum(lsum, axis=-1, keepdims=True)
    o = acc / l
    for t in range(n_t):
        ot = jnp.where(headmask, o[t * 8:(t + 1) * 8, :], 0.0)
        o_ref[t, pl.ds(b, 1), :] = jnp.sum(ot, axis=0, keepdims=True)


def _fox_s(tbl, q, kr, vr, cum, lfpool, mstrict, pgsuf, k_cache, v_cache, layer):
    n_t, nb, _ = q.shape
    n_pages = tbl.shape[0] // nb
    whole = lambda a: pl.BlockSpec(a.shape, lambda b, tbl_: (0,) * a.ndim)
    grid_spec = pltpu.PrefetchScalarGridSpec(
        num_scalar_prefetch=1, grid=(nb,),
        in_specs=[whole(q), whole(kr), whole(vr), whole(cum),
                  pl.BlockSpec((1,) + lfpool.shape[1:], lambda b, tbl_: (layer, 0, 0, 0)),
                  whole(mstrict), whole(pgsuf),
                  pl.BlockSpec(memory_space=pl.ANY), pl.BlockSpec(memory_space=pl.ANY)],
        out_specs=pl.BlockSpec((n_t, nb, FOX_W), lambda b, tbl_: (0, 0, 0)),
        scratch_shapes=[pltpu.VMEM((2, n_pages, FOX_W, LANES), F32),
                        pltpu.VMEM((2, n_pages, FOX_W, LANES), F32),
                        pltpu.VMEM((n_pages * 8, LANES), F32),
                        pltpu.SemaphoreType.DMA((2, 2))])
    return pl.pallas_call(
        functools.partial(_fox_s_kernel, layer=layer, n_pages=n_pages, n_t=n_t),
        out_shape=jax.ShapeDtypeStruct((n_t, nb, FOX_W), F32), grid_spec=grid_spec,
        compiler_params=_params(("arbitrary",)), name="fox_s")(
            tbl, q, kr, vr, cum, lfpool, mstrict, pgsuf, k_cache, v_cache)


def _ssd_p_kernel(z_ref, xbc_ref, smr_ref, smt_ref, arow_ref, acol_ref, ltri_ref, utri_ref, dvec_ref,
                  nrm_ref, o_ref, h_ref, ybuf):
    j = pl.program_id(1)

    @pl.when(j == 0)
    def _():
        h_ref[...] = jnp.zeros(h_ref.shape, F32)

    smr = smr_ref[0]
    cs_rows = _dot_hl_left(ltri_ref[...], smr * arow_ref[...])
    cs_t = _dot_hl(smt_ref[0] * acol_ref[...], utri_ref[...])
    end = cs_rows[CHUNK - 1:CHUNK, :]
    xbc = xbc_ref[0]
    x = xbc[:, 0:SSD_W]
    tril = (lax.broadcasted_iota(jnp.int32, (CHUNK, CHUNK), 1)
            <= lax.broadcasted_iota(jnp.int32, (CHUNK, CHUNK), 0))
    rep = SSD_HEADS // SSD_GROUPS
    for g in range(SSD_GROUPS):
        bg = xbc[:, SSD_W + g * SSD_STATE:SSD_W + (g + 1) * SSD_STATE].astype(BF16)
        cg = xbc[:, SSD_W + SSD_BC + g * SSD_STATE:SSD_W + SSD_BC + (g + 1) * SSD_STATE].astype(BF16)
        cb = _dot_nt(cg, bg)
        for hh in range(g * rep, (g + 1) * rep):
            col = SM_DT + hh
            csc = cs_rows[:, col:col + 1]
            e = end[:, col:col + 1]
            lm = jnp.exp(jnp.where(tril, csc - cs_t[col:col + 1, :], -jnp.inf))
            xh = x[:, hh * SSD_HD:(hh + 1) * SSD_HD]
            xdt = xh * smr[:, col:col + 1]
            hst = h_ref[0, hh]
            y = _dot(cb * lm, xdt) + _dot_nt(cg, hst) * jnp.exp(csc)
            h_ref[0, hh] = hst * jnp.exp(e) + _dot_tn(xdt * jnp.exp(e - csc), bg)
            ybuf[:, hh * SSD_HD:(hh + 1) * SSD_HD] = y
    y = (ybuf[...] + x * dvec_ref[...]) * _silu(z_ref[0])
    gw = SSD_W // SSD_GROUPS
    for g in range(SSD_GROUPS):
        yg = y[:, g * gw:(g + 1) * gw]
        yn = yg * lax.rsqrt(jnp.mean(yg * yg, axis=-1, keepdims=True) + EPS)
        o_ref[0, :, g * gw:(g + 1) * gw] = (yn * nrm_ref[:, g * gw:(g + 1) * gw]).astype(BF16)


def _ssd_p(z, xbc, smr, smt, arow, acol, ltri, utri, dvec, nrm, l):
    nb, t_pad, _ = z.shape
    nj = t_pad // CHUNK
    rows = lambda w_: pl.BlockSpec((1, CHUNK, w_), lambda b, j: (b, j, 0))
    return pl.pallas_call(
        _ssd_p_kernel,
        out_shape=[jax.ShapeDtypeStruct((nb, t_pad, SSD_W), BF16),
                   jax.ShapeDtypeStruct((nb, SSD_HEADS, SSD_HD, SSD_STATE), F32)],
        grid=(nb, nj),
        in_specs=[rows(SSD_W), rows(SSD_CONV_DIM), rows(LANES),
                  pl.BlockSpec((1, 16, CHUNK), lambda b, j: (b, 0, j)),
                  _lspec((1, LANES), l), _lspec((16, 1), l), _const_spec((CHUNK, CHUNK)),
                  _const_spec((CHUNK, CHUNK)), _lspec((1, SSD_W), l), _lspec((1, SSD_W), l)],
        out_specs=[rows(SSD_W),
                   pl.BlockSpec((1, SSD_HEADS, SSD_HD, SSD_STATE), lambda b, j: (b, 0, 0, 0))],
        scratch_shapes=[pltpu.VMEM((CHUNK, SSD_W), F32)],
        compiler_params=_params(("parallel", "arbitrary")), name="ssd_p")(
            z, xbc, smr, smt, arow, acol, ltri, utri, dvec, nrm)


def _ssd_s_kernel(x_ref, b_ref, c_ref, sm_ref, xg_ref, zg_ref, a_ref, d_ref, nrm_ref, st_ref,
                  o_ref, so_ref, ybuf, *, n_t):
    hh = pl.program_id(0)
    rep = SSD_HEADS // SSD_GROUPS
    gw = SSD_W // SSD_GROUPS
    dt = [sm_ref[t, pl.ds(SM_DT + hh, 1), :] for t in range(n_t)]
    dec = [jnp.exp(dt[t] * a_ref[pl.ds(hh, 1), :]) for t in range(n_t)]

    def body(p, carry):
        hp = st_ref[0, 0, p]
        for t in range(n_t):
            hp = hp * dec[t] + b_ref[t] * (x_ref[t, pl.ds(p, 1), :] * dt[t])
            ybuf[t, pl.ds(hh * SSD_HD + p, 1), :] = jnp.sum(c_ref[t] * hp, axis=0, keepdims=True)
        so_ref[0, 0, p] = hp
        return carry

    lax.fori_loop(0, SSD_HD, body, 0)

    @pl.when(lax.rem(hh, rep) == rep - 1)
    def _():
        r0 = pl.multiple_of((hh // rep) * gw, gw)
        for t in range(n_t):
            y = ybuf[t, pl.ds(r0, gw), :] + xg_ref[t] * d_ref[pl.ds(r0, gw), :]
            y = y * _silu(zg_ref[t])
            yn = y * lax.rsqrt(jnp.mean(y * y, axis=0, keepdims=True) + EPS)
            o_ref[t, pl.ds(r0, gw), :] = yn * nrm_ref[pl.ds(r0, gw), :]


def _ssd_s(pt, a_b, d_b, nrm_b, state, layer):
    n_t = pt.shape[0]
    rep = SSD_HEADS // SSD_GROUPS
    gw = SSD_W // SSD_GROUPS
    blk = lambda h_, f: pl.BlockSpec((n_t, h_, LANES), lambda hh: (0, f(hh), 0))
    st_spec = pl.BlockSpec((1, 1, SSD_HD, SSD_STATE, LANES), lambda hh: (layer, hh, 0, 0, 0))
    return pl.pallas_call(
        functools.partial(_ssd_s_kernel, n_t=n_t),
        out_shape=[jax.ShapeDtypeStruct((n_t, SSD_W, LANES), F32),
                   jax.ShapeDtypeStruct((1, SSD_HEADS, SSD_HD, SSD_STATE, LANES), F32)],
        grid=(SSD_HEADS,),
        in_specs=[blk(SSD_HD, lambda hh: R_XBC // SSD_HD + hh),
                  blk(SSD_STATE, lambda hh: (R_XBC + SSD_W) // SSD_STATE + hh // rep),
                  blk(SSD_STATE, lambda hh: (R_XBC + SSD_W + SSD_BC) // SSD_STATE + hh // rep),
                  blk(LANES, lambda hh: R_SM // LANES),
                  blk(gw, lambda hh: R_XBC // gw + hh // rep),
                  blk(gw, lambda hh: R_Z // gw + hh // rep),
                  _lspec((SSD_HEADS, LANES), layer), _lspec((SSD_W, LANES), layer),
                  _lspec((SSD_W, LANES), layer), st_spec],
        out_specs=[_const_spec((n_t, SSD_W, LANES)),
                   pl.BlockSpec((1, 1, SSD_HD, SSD_STATE, LANES), lambda hh: (0, hh, 0, 0, 0))],
        scratch_shapes=[pltpu.VMEM((n_t, SSD_W, LANES), F32)],
        compiler_params=_params(("arbitrary",)), name="ssd_s")(
            pt, pt, pt, pt, pt, pt, a_b, d_b, nrm_b, state)


def _gla_p_kernel(gqk_ref, gv_ref, gg_ref, smr_ref, wg_ref, gb_ref, bl_ref, be_ref, g64_ref, gn_ref,
                  o_ref, s_ref, st_scr, *, t_real):
    j = pl.program_id(1)
    nj = pl.num_programs(1)
    sub = GLA_SUB

    @pl.when(j == 0)
    def _():
        st_scr[...] = jnp.zeros(st_scr.shape, F32)

    row = lax.broadcasted_iota(jnp.int32, (CHUNK, GLA_KW), 0)
    valid = j * CHUNK + row < t_real
    glog = _log_sigmoid(_dot(smr_ref[0], wg_ref[...]) + gb_ref[...]) * (1.0 / GLA_TAU)
    glog = jnp.where(valid, glog, 0.0)
    gqk = gqk_ref[0]
    gq = gqk[:, 0:GLA_KW] * (GLA_DK ** -0.5)
    gk = jnp.where(valid, gqk[:, GLA_KW:2 * GLA_KW], 0.0)
    v = gv_ref[0]
    bcl = _dot_hl_left(bl_ref[...], glog)
    tot = _dot_hl_left(be_ref[...], glog)
    qe = (gq * jnp.exp(bcl)).astype(BF16)
    kp = gk * jnp.exp(-bcl)
    kend = (gk * jnp.exp(tot - bcl)).astype(BF16)
    dec = jnp.exp(tot)

    r_k = lax.broadcasted_iota(jnp.int32, (GLA_HEADS * sub, GLA_KW), 0) // sub
    c_k = lax.broadcasted_iota(jnp.int32, (GLA_HEADS * sub, GLA_KW), 1) // GLA_DK
    r_v = lax.broadcasted_iota(jnp.int32, (GLA_HEADS * sub, GLA_W), 0) // sub
    c_v = lax.broadcasted_iota(jnp.int32, (GLA_HEADS * sub, GLA_W), 1) // GLA_DV
    r_s = lax.broadcasted_iota(jnp.int32, (GLA_W, GLA_KW), 0) // GLA_DV
    c_s = lax.broadcasted_iota(jnp.int32, (GLA_W, GLA_KW), 1) // GLA_DK
    causal = (lax.broadcasted_iota(jnp.int32, (sub, GLA_HEADS * sub), 1) % sub
              <= lax.broadcasted_iota(jnp.int32, (sub, GLA_HEADS * sub), 0))
    outs = []
    for i in range(CHUNK // sub):
        sl = slice(i * sub, (i + 1) * sub)
        kbd = jnp.where(r_k == c_k, jnp.concatenate([kp[sl]] * GLA_HEADS, axis=0), 0.0)
        att = jnp.where(causal, _dot_nt(qe[sl], kbd), 0.0)
        vbd = jnp.where(r_v == c_v, jnp.concatenate([v[sl]] * GLA_HEADS, axis=0), 0.0)
        st = st_scr[...]
        outs.append(_dot(att, vbd) + _dot_nt(qe[sl], st))
        upd = jnp.where(r_s == c_s, _dot_tn(v[sl], kend[sl]), 0.0)
        st_scr[...] = st * dec[i * sub:i * sub + 1, :] + upd
    o = jnp.concatenate(outs, axis=0)
    msq = _dot_hl(o * o, g64_ref[...])
    o_ref[0] = (o * lax.rsqrt(msq + EPS) * gn_ref[...] * _silu(gg_ref[0])).astype(BF16)

    @pl.when(j == nj - 1)
    def _():
        s_kv = st_scr[...].T
        for h in range(GLA_HEADS):
            s_ref[0, h] = s_kv[h * GLA_DK:(h + 1) * GLA_DK, h * GLA_DV:(h + 1) * GLA_DV]


def _gla_p(gqk, gv, gg, smr, wgp, gb, bl, be, g64, gn, t_real, l):
    nb, t_pad, _ = gv.shape
    nj = t_pad // CHUNK
    rows = lambda w_: pl.BlockSpec((1, CHUNK, w_), lambda b, j: (b, j, 0))
    return pl.pallas_call(
        functools.partial(_gla_p_kernel, t_real=t_real),
        out_shape=[jax.ShapeDtypeStruct((nb, t_pad, GLA_W), BF16),
                   jax.ShapeDtypeStruct((nb, GLA_HEADS, GLA_DK, GLA_DV), F32)],
        grid=(nb, nj),
        in_specs=[rows(2 * GLA_KW), rows(GLA_W), rows(GLA_W), rows(LANES),
                  _lspec((LANES, GLA_KW), l), _lspec((1, GLA_KW), l),
                  _const_spec((CHUNK, CHUNK)), _const_spec((CHUNK, CHUNK)),
                  _const_spec((GLA_W, GLA_W)), _lspec((1, GLA_W), l)],
        out_specs=[rows(GLA_W),
                   pl.BlockSpec((1, GLA_HEADS, GLA_DK, GLA_DV), lambda b, j: (b, 0, 0, 0))],
        scratch_shapes=[pltpu.VMEM((GLA_W, GLA_KW), F32)],
        compiler_params=_params(("parallel", "arbitrary")), name="gla_p")(
            gqk, gv, gg, smr, wgp, gb, bl, be, g64, gn)


def _gla_s_kernel(q_ref, k_ref, v_ref, gg_ref, sm_ref, wgt_ref, gb_ref, gn_ref, st_ref,
                  o_ref, so_ref, eg_scr, acc_scr, *, n_t):
    for t in range(n_t):
        glog = _log_sigmoid(_dot(wgt_ref[...], sm_ref[t]) + gb_ref[...]) * (1.0 / GLA_TAU)
        eg_scr[t] = jnp.exp(glog)
        acc_scr[t] = jnp.zeros((GLA_DV, LANES), F32)

    def body(kk, carry):
        s = st_ref[0, 0, kk]
        for t in range(n_t):
            s = s * eg_scr[t, pl.ds(kk, 1), :] + k_ref[t, pl.ds(kk, 1), :] * v_ref[t]
            acc_scr[t] = acc_scr[t] + q_ref[t, pl.ds(kk, 1), :] * s
        so_ref[0, 0, kk] = s
        return carry

    lax.fori_loop(0, GLA_DK, body, 0)
    for t in range(n_t):
        o = acc_scr[t]
        on = o * lax.rsqrt(jnp.mean(o * o, axis=0, keepdims=True) + EPS)
        o_ref[t] = on * gn_ref[...] * _silu(gg_ref[t])


def _gla_s(pt, wgt, gb_b, gn_b, state, layer):
    n_t = pt.shape[0]
    blk = lambda h_, f: pl.BlockSpec((n_t, h_, LANES), lambda hh: (0, f(hh), 0))
    return pl.pallas_call(
        functools.partial(_gla_s_kernel, n_t=n_t),
        out_shape=[jax.ShapeDtypeStruct((n_t, GLA_W, LANES), F32),
                   jax.ShapeDtypeStruct((1, GLA_HEADS, GLA_DK, GLA_DV, LANES), F32)],
        grid=(GLA_HEADS,),
        in_specs=[blk(GLA_DK, lambda hh: R_GQ // GLA_DK + hh),
                  blk(GLA_DK, lambda hh: R_GK // GLA_DK + hh),
                  blk(GLA_DV, lambda hh: R_GV // GLA_DV + hh),
                  blk(GLA_DV, lambda hh: R_GG // GLA_DV + hh),
                  blk(LANES, lambda hh: R_SM // LANES),
                  pl.BlockSpec((None, GLA_DK, LANES), lambda hh: (layer, hh, 0)),
                  pl.BlockSpec((None, GLA_DK, LANES), lambda hh: (layer, hh, 0)),
                  _lspec((GLA_DV, LANES), layer),
                  pl.BlockSpec((1, 1, GLA_DK, GLA_DV, LANES), lambda hh: (layer, hh, 0, 0, 0))],
        out_specs=[blk(GLA_DV, lambda hh: hh),
                   pl.BlockSpec((1, 1, GLA_DK, GLA_DV, LANES), lambda hh: (0, hh, 0, 0, 0))],
        scratch_shapes=[pltpu.VMEM((n_t, GLA_DK, LANES), F32), pltpu.VMEM((n_t, GLA_DV, LANES), F32)],
        compiler_params=_params(("arbitrary",)), name="gla_s")(
            pt, pt, pt, pt, pt, wgt, gb_b, gn_b, state)


def _tri_consts(n_pages):
    i = np.arange(CHUNK)
    ltri = (i[None, :] <= i[:, None]).astype(np.float32)
    utri = ltri.T
    mstrict = (i[:, None] > i[None, :]).astype(np.float32)
    same = (i[:, None] // GLA_SUB) == (i[None, :] // GLA_SUB)
    bl = (same & (i[None, :] <= i[:, None])).astype(np.float32)
    be = same.astype(np.float32)
    r = np.arange(n_pages * 8)
    pgsuf = ((r[:, None] % 8 == r[None, :] % 8) & (r[None, :] // 8 > r[:, None] // 8)).astype(np.float32)
    h = np.arange(FOX_W)
    g64 = ((h[:, None] // 64) == (h[None, :] // 64)).astype(np.float32) / 64.0
    c = lambda a: jnp.asarray(a, BF16)
    return dict(ltri=c(ltri), utri=c(utri), mstrict=c(mstrict), bl=c(bl), be=c(be), pgsuf=c(pgsuf),
                g64=c(g64))


def kernel(x_prompt, x_sample, cache_fox_k, cache_fox_v, cache_fox_logf, state_ssm, state_conv,
           state_gla, page_table, meta_tokens, ffn1_norm, ffn1_w_in, ffn1_w_out, mix_norm, w_mix_in,
           fox_q_norm, fox_k_norm, fox_f_bias, ssd_conv_w, ssd_conv_b, ssd_dt_bias, ssd_a_log, ssd_d,
           ssd_norm, gla_w_gate, gla_gate_bias, gla_norm, w_mix_out, ffn2_norm, ffn2_w_in, ffn2_w_out):
    nbp, seq, _ = x_prompt.shape
    nbs, n_t, _ = x_sample.shape
    depth = ffn1_norm.shape[0]
    assert nbs == LANES and seq % CHUNK == 0
    t_real = N_META + seq
    t_pad = -(-t_real // CHUNK) * CHUNK
    n_pool, page_size = cache_fox_k.shape[1], cache_fox_k.shape[2]
    assert page_size == LANES and n_pool % 2 == 0
    n_pages = page_table.shape[1]
    consts = _tri_consts(n_pages)

    meta = jnp.broadcast_to(meta_tokens[None], (nbp, N_META, D_MODEL))
    xp = jnp.concatenate([meta, x_prompt, jnp.zeros((nbp, t_pad - t_real, D_MODEL), F32)], axis=1)
    xp = xp.reshape(nbp * t_pad, D_MODEL)
    xs = jnp.transpose(x_sample, (1, 0, 2)).reshape(n_t * nbs, D_MODEL)

    k_cache = jnp.transpose(cache_fox_k, (0, 1, 3, 4, 2)).reshape(depth, n_pool, FOX_W, page_size)
    v_cache = jnp.transpose(cache_fox_v, (0, 1, 3, 4, 2)).reshape(depth, n_pool, FOX_W, page_size)
    lfpool = jnp.transpose(cache_fox_logf, (0, 1, 3, 2)).reshape(depth, n_pool // 2, 8, page_size)
    ssm_in = jnp.transpose(state_ssm, (0, 2, 3, 4, 1))
    gla_in = jnp.transpose(state_gla, (0, 2, 3, 4, 1))
    conv_in = jnp.transpose(state_conv, (0, 2, 1, 3))
    tbl = page_table.reshape(-1).astype(jnp.int32)

    w1i, w1o = ffn1_w_in.astype(BF16), ffn1_w_out.astype(BF16)
    w2i, w2o = ffn2_w_in.astype(BF16), ffn2_w_out.astype(BF16)
    wmo = w_mix_out.astype(BF16)
    col = [0] + list(IN_SPLITS) + [w_mix_in.shape[-1]]
    part = lambda i: w_mix_in[:, :, col[i]:col[i + 1]]
    fq, fk, fv, ff, sz, sxbc, sdt, gq, gk, gv, glr, gg = [part(i) for i in range(len(IN_SIZES))]
    n_small = FOX_HEADS + SSD_HEADS + GLA_RANK
    wproj = jnp.concatenate([fq, fk, fv, sz, sxbc, gq, gk, gv, gg, ff, sdt, glr,
                             jnp.zeros((depth, D_MODEL, LANES - n_small), F32)], axis=-1).astype(BF16)
    n1, n2, nm = ffn1_norm[:, None], ffn2_norm[:, None], mix_norm[:, None]
    qg = jnp.tile(fox_q_norm, (1, FOX_HEADS))[:, None]
    kg = jnp.tile(fox_k_norm, (1, FOX_HEADS))[:, None]
    a = -jnp.exp(ssd_a_log)
    lanes_pad = lambda v, off: jnp.pad(v, ((0, 0), (off, LANES - off - v.shape[1])))[:, None]
    fb, dtb, arow = lanes_pad(fox_f_bias, SM_F), lanes_pad(ssd_dt_bias, SM_DT), lanes_pad(a, SM_DT)
    acol = jnp.pad(a, ((0, 0), (SM_DT, 16 - SM_DT - SSD_HEADS)))[:, :, None]
    cw = jnp.pad(ssd_conv_w, ((0, 0), (0, 8 - SSD_CONV), (0, 0)))
    cb = ssd_conv_b[:, None]
    d_rep = jnp.repeat(ssd_d, SSD_HD, axis=1)
    dvec, snrm = d_rep[:, None], ssd_norm[:, None]
    wgp = jnp.pad(gla_w_gate, ((0, 0), (SM_LR, LANES - SM_LR - GLA_RANK), (0, 0))).astype(BF16)
    gb = gla_gate_bias[:, None]
    gn = jnp.tile(gla_norm, (1, GLA_HEADS))[:, None]
    on_lanes = lambda v: jnp.broadcast_to(v[:, :, None], v.shape + (LANES,))
    a_b, d_b, snrm_b = on_lanes(a), on_lanes(d_rep), on_lanes(ssd_norm)
    wgt = jnp.pad(jnp.transpose(gla_w_gate, (0, 2, 1)),
                  ((0, 0), (0, 0), (SM_LR, LANES - SM_LR - GLA_RANK))).astype(BF16)
    gb_b, gn_b = on_lanes(gla_gate_bias), on_lanes(gla_norm)

    outs_p = [[] for _ in range(6)]
    outs_s = [[] for _ in range(6)]
    for l in range(depth):
        xp = _ffn(xp, n1, w1i, w1o, FFN_TM, l)
        (qat, kaug, kt, vt, vt16, z, xbc, gqk, gvv, ggg, smr, smt, convp) = _proj_p(
            xp.reshape(nbp, t_pad, D_MODEL), nm, wproj, qg, kg, fb, dtb, cw, cb, consts["g64"],
            consts["ltri"], t_real, l)
        fox_o = _fox_p(qat, kaug, vt16)
        ssd_o, ssm_p = _ssd_p(z, xbc, smr, smt, arow, acol, consts["ltri"], consts["utri"], dvec, snrm, l)
        gla_o, gla_p = _gla_p(gqk, gvv, ggg, smr, wgp, gb, consts["bl"], consts["be"], consts["g64"],
                              gn, t_real, l)
        xp = _mix_ffn(xp, fox_o.reshape(-1, FOX_W), ssd_o.reshape(-1, SSD_W), gla_o.reshape(-1, GLA_W),
                      wmo, n2, w2i, w2o, FFN_TM, l)
        outs_p[0].append(kt.reshape(nbp, FOX_HEADS, FOX_HD, t_pad))
        outs_p[1].append(vt.reshape(nbp, FOX_HEADS, FOX_HD, t_pad))
        outs_p[2].append(jnp.transpose(smt[:, 0:FOX_HEADS, :t_real], (0, 2, 1)))
        outs_p[3].append(ssm_p)
        outs_p[4].append(convp[:, 8 - (SSD_CONV - 1):])
        outs_p[5].append(gla_p)

        xs = _ffn(xs, n1, w1i, w1o, LANES, l)
        qs, krs, vrs, pt, cum, convs = _proj_s(xs, nm, wproj, qg, kg, fb, dtb, cw, cb, consts["g64"],
                                               conv_in, l)
        fox_os = _fox_s(tbl, qs, krs, vrs, cum, lfpool, consts["mstrict"], consts["pgsuf"],
                        k_cache, v_cache, l)
        ssd_ot, ssm_s = _ssd_s(pt, a_b, d_b, snrm_b, ssm_in, l)
        gla_ot, gla_s = _gla_s(pt, wgt, gb_b, gn_b, gla_in, l)
        xs = _mix_ffn_t(xs, fox_os, ssd_ot, gla_ot, wmo, n2, w2i, w2o, l)
        kts = pt[:, R_K:R_K + FOX_W].reshape(n_t, FOX_HEADS, FOX_HD, nbs)
        vts = pt[:, R_V:R_V + FOX_W].reshape(n_t, FOX_HEADS, FOX_HD, nbs)
        outs_s[0].append(jnp.transpose(kts, (3, 0, 1, 2)))
        outs_s[1].append(jnp.transpose(vts, (3, 0, 1, 2)))
        outs_s[2].append(jnp.transpose(pt[:, R_SM + SM_F:R_SM + SM_F + FOX_HEADS], (2, 0, 1)))
        outs_s[3].append(jnp.transpose(ssm_s[0], (3, 0, 1, 2)))
        outs_s[4].append(jnp.transpose(convs, (1, 0, 2)))
        outs_s[5].append(jnp.transpose(gla_s[0], (3, 0, 1, 2)))

    y_prompt = xp.reshape(nbp, t_pad, D_MODEL)[:, N_META:t_real]
    y_sample = jnp.transpose(xs.reshape(n_t, nbs, D_MODEL), (1, 0, 2))
    k_p = jnp.transpose(jnp.stack(outs_p[0])[..., :t_real], (0, 1, 4, 2, 3))
    v_p = jnp.transpose(jnp.stack(outs_p[1])[..., :t_real], (0, 1, 4, 2, 3))
    lf_p, ssm_p, conv_p, gla_p = [jnp.stack(a) for a in outs_p[2:]]
    k_s, v_s, lf_s, ssm_s, conv_s, gla_s = [jnp.stack(a) for a in outs_s]
    return (y_prompt, y_sample, k_p, v_p, lf_p, ssm_p, conv_p, gla_p, k_s, v_s, lf_s, ssm_s, conv_s, gla_s)
```

```python
import functools
import math

import jax
import jax.numpy as jnp
import numpy as np
from jax import lax
from jax.experimental import pallas as pl
from jax.experimental.pallas import tpu as pltpu

F32 = jnp.float32
BF16 = jnp.bfloat16

D_MODEL = 1024
N_META = 16
D_FF = 2816
EPS = 1e-6
FOX_HEADS = 4
FOX_HD = 64
FOX_W = FOX_HEADS * FOX_HD
SSD_HEADS = 8
SSD_HD = 64
SSD_W = SSD_HEADS * SSD_HD
SSD_GROUPS = 2
SSD_STATE = 64
SSD_CONV = 4
SSD_BC = SSD_GROUPS * SSD_STATE
SSD_CONV_DIM = SSD_W + 2 * SSD_BC
GLA_HEADS = 4
GLA_DK = 32
GLA_DV = 64
GLA_KW = GLA_HEADS * GLA_DK
GLA_W = GLA_HEADS * GLA_DV
GLA_RANK = 16
GLA_TAU = 16.0
D_MIX = FOX_W + SSD_W + GLA_W
IN_SIZES = (FOX_W, FOX_W, FOX_W, FOX_HEADS, SSD_W, SSD_CONV_DIM, SSD_HEADS,
            GLA_KW, GLA_KW, GLA_W, GLA_RANK, GLA_W)
IN_SPLITS = tuple(int(v) for v in np.cumsum(IN_SIZES)[:-1])

LOG2E = 1.4426950408889634
FOX_AUG = FOX_HEADS * 128

LANES = 128
CHUNK = 128
GLA_SUB = 32
FOX_KW = 512
FFN_CK = 256
FFN_TM = 512
PROJ_GROUP = 4
MIX_GROUP = 4
VMEM_LIMIT = 56 * 1024 * 1024

C_Q, C_K, C_V = 0, 256, 512
C_Z = 768
C_XBC = 1280
C_GQ, C_GK, C_GV, C_GG = 2048, 2176, 2304, 2560
C_SM = 2816
N_PROJ = 2944
SM_F = 0
SM_DT = 4
SM_LR = 12
R_K, R_V, R_Z, R_XBC = 0, 256, 512, 1024
R_GQ, R_GK, R_GV, R_GG, R_SM = 1792, 1920, 2048, 2304, 2560
N_PT = 2688


def _dot(a, b):
    return jnp.dot(a.astype(BF16), b.astype(BF16), preferred_element_type=F32)


def _dot_nt(a, b):
    return lax.dot_general(a.astype(BF16), b.astype(BF16), (((1,), (1,)), ((), ())),
                           preferred_element_type=F32)


def _dot_tn(a, b):
    return lax.dot_general(a.astype(BF16), b.astype(BF16), (((0,), (0,)), ((), ())),
                           preferred_element_type=F32)


def _split(a):
    hi = a.astype(BF16)
    lo = (a - hi.astype(F32)).astype(BF16)
    return hi, lo


def _dot_hl(a, m):
    hi, lo = _split(a)
    return (jnp.dot(hi, m, preferred_element_type=F32) + jnp.dot(lo, m, preferred_element_type=F32))


def _dot_hl_left(m, a):
    hi, lo = _split(a)
    return (jnp.dot(m, hi, preferred_element_type=F32) + jnp.dot(m, lo, preferred_element_type=F32))


def _silu(x):
    return x * jax.nn.sigmoid(x)


def _softplus(x):
    return jnp.maximum(x, 0.0) + jnp.log1p(jnp.exp(-jnp.abs(x)))


def _log_sigmoid(x):
    return jnp.minimum(x, 0.0) - jnp.log1p(jnp.exp(-jnp.abs(x)))


def _rms(x, g):
    return x * lax.rsqrt(jnp.mean(x * x, axis=-1, keepdims=True) + EPS) * g


def _const_spec(shape):
    n = len(shape)
    return pl.BlockSpec(shape, lambda *_: (0,) * n)


def _lspec(shape, l):
    n = len(shape)
    return pl.BlockSpec((None,) + tuple(shape), lambda *_: (l,) + (0,) * n)


def _params(sem):
    return pltpu.CompilerParams(dimension_semantics=sem, vmem_limit_bytes=VMEM_LIMIT)


def _swiglu_half(x, g_ref, win_ref, wout_ref):
    h = _rms(x, g_ref[...]).astype(BF16)
    acc = None
    for c in range(D_FF // FFN_CK):
        g = jnp.dot(h, win_ref[:, c * FFN_CK:(c + 1) * FFN_CK], preferred_element_type=F32)
        u = jnp.dot(h, win_ref[:, D_FF + c * FFN_CK:D_FF + (c + 1) * FFN_CK],
                    preferred_element_type=F32)
        a = (_silu(g) * u).astype(BF16)
        part = jnp.dot(a, wout_ref[c * FFN_CK:(c + 1) * FFN_CK, :], preferred_element_type=F32)
        acc = part if acc is None else acc + part
    return x + 0.5 * acc


def _ffn_kernel(x_ref, g_ref, win_ref, wout_ref, o_ref):
    o_ref[...] = _swiglu_half(x_ref[...], g_ref, win_ref, wout_ref)


def _mix_ffn_kernel(x_ref, fo_ref, so_ref, go_ref, wmo_ref, g_ref, win_ref, wout_ref, o_ref):
    x = x_ref[...]
    x = x + jnp.dot(fo_ref[...], wmo_ref[0:FOX_W, :], preferred_element_type=F32)
    x = x + jnp.dot(so_ref[...], wmo_ref[FOX_W:FOX_W + SSD_W, :], preferred_element_type=F32)
    x = x + jnp.dot(go_ref[...], wmo_ref[FOX_W + SSD_W:D_MIX, :], preferred_element_type=F32)
    o_ref[...] = _swiglu_half(x, g_ref, win_ref, wout_ref)


def _mix_ffn_t_kernel(x_ref, fo_ref, so_ref, go_ref, wmo_ref, g_ref, win_ref, wout_ref, o_ref):
    x = x_ref[...]
    x = x + _dot(fo_ref[0], wmo_ref[0:FOX_W, :])
    x = x + _dot(so_ref[0].T, wmo_ref[FOX_W:FOX_W + SSD_W, :])
    x = x + _dot(go_ref[0].T, wmo_ref[FOX_W + SSD_W:D_MIX, :])
    o_ref[...] = _swiglu_half(x, g_ref, win_ref, wout_ref)


def _ffn_weight_specs(l):
    return [_lspec((1, D_MODEL), l), _lspec((D_MODEL, 2 * D_FF), l), _lspec((D_FF, D_MODEL), l)]


def _ffn(x, g, win, wout, tm, l):
    rows = x.shape[0]
    row = lambda w: pl.BlockSpec((tm, w), lambda i: (i, 0))
    return pl.pallas_call(
        _ffn_kernel, out_shape=jax.ShapeDtypeStruct((rows, D_MODEL), F32), grid=(rows // tm,),
        in_specs=[row(D_MODEL)] + _ffn_weight_specs(l), out_specs=row(D_MODEL),
        compiler_params=_params(("parallel",)), name="ffn")(x, g, win, wout)


def _mix_ffn(x, fo, so, go, wmo, g, win, wout, tm, l):
    rows = x.shape[0]
    row = lambda w: pl.BlockSpec((tm, w), lambda i: (i, 0))
    return pl.pallas_call(
        _mix_ffn_kernel, out_shape=jax.ShapeDtypeStruct((rows, D_MODEL), F32), grid=(rows // tm,),
        in_specs=[row(D_MODEL), row(FOX_W), row(SSD_W), row(GLA_W), _lspec((D_MIX, D_MODEL), l)]
        + _ffn_weight_specs(l), out_specs=row(D_MODEL),
        compiler_params=_params(("parallel",)), name="mix_ffn")(x, fo, so, go, wmo, g, win, wout)


def _mix_ffn_t(x, fo, so_t, go_t, wmo, g, win, wout, l):
    rows = x.shape[0]
    n_t = rows // LANES
    row = pl.BlockSpec((LANES, D_MODEL), lambda i: (i, 0))
    per_t = lambda w, n: pl.BlockSpec((1, w, n), lambda i: (i, 0, 0))
    return pl.pallas_call(
        _mix_ffn_t_kernel, out_shape=jax.ShapeDtypeStruct((rows, D_MODEL), F32), grid=(n_t,),
        in_specs=[row, per_t(LANES, FOX_W), per_t(SSD_W, LANES), per_t(GLA_W, LANES),
                  _lspec((D_MIX, D_MODEL), l)] + _ffn_weight_specs(l), out_specs=row,
        compiler_params=_params(("parallel",)), name="mix_ffn_t")(x, fo, so_t, go_t, wmo, g, win, wout)


def _head_norm(x, gain_row, g64_ref):
    msq = _dot_hl(x * x, g64_ref[...])
    return x * lax.rsqrt(msq + EPS) * gain_row


def _small_block(sm, fb_ref, dtb_ref):
    logf = _log_sigmoid(sm + fb_ref[...])
    dt = _softplus(sm + dtb_ref[...])
    return logf, dt


def _proj_p_kernel(x_ref, nrm_ref, w_ref, qg_ref, kg_ref, fb_ref, dtb_ref, cw_ref, cb_ref, g64_ref,
                   ltri_ref,
                   qat_ref, ka_ref, kt_ref, vt_ref, vt16_ref, z_ref, xbc_ref, gqk_ref, gv_ref, gg_ref,
                   smr_ref, smt_ref, convp_ref,
                   xb_scr, carry_scr, *, t_real, grp):
    j = pl.program_id(1)

    @pl.when(j == 0)
    def _():
        xb_scr[:, 0:8, :] = jnp.zeros((grp, 8, SSD_CONV_DIM), F32)
        carry_scr[...] = jnp.zeros((grp, 8, LANES), F32)

    h = _rms(x_ref[...].reshape(grp * CHUNK, D_MODEL), nrm_ref[...]).astype(BF16)
    proj = lambda c0, n: jnp.dot(h, w_ref[:, c0:c0 + n], preferred_element_type=F32)
    seq = lambda a, g: a[g * CHUNK:(g + 1) * CHUNK]

    sm_all = proj(C_SM, LANES)
    lane = lax.broadcasted_iota(jnp.int32, (CHUNK, LANES), 1)
    row = lax.broadcasted_iota(jnp.int32, (CHUNK, LANES), 0)
    logf_all, dt_all = _small_block(sm_all, fb_ref, dtb_ref)
    f_rows = []
    for g in range(grp):
        logf = jnp.where(lane < SM_DT, seq(logf_all, g), 0.0)
        dt = jnp.where(j * CHUNK + row < t_real, seq(dt_all, g), 0.0)
        f = _dot_hl_left(ltri_ref[...], logf) + carry_scr[g, 0:1, :]
        carry_scr[g] = jnp.broadcast_to(f[CHUNK - 1:CHUNK, :], (8, LANES))
        smr_ref[g] = jnp.where(lane < SM_LR, dt, seq(sm_all, g))
        smt_ref[g] = jnp.where(lane < SM_DT, logf, dt).T[0:16, :]
        f_rows.append(f)

    q_all = _head_norm(proj(C_Q, FOX_W), qg_ref[...], g64_ref) * (FOX_HD ** -0.5 * LOG2E)
    k_all = _head_norm(proj(C_K, FOX_W), kg_ref[...], g64_ref)
    v_all = proj(C_V, FOX_W)
    l64 = lax.broadcasted_iota(jnp.int32, (CHUNK, FOX_HD), 1)
    for g in range(grp):
        q, k = seq(q_all, g), seq(k_all, g)
        qa, ka = [], []
        for hh in range(FOX_HEADS):
            y = f_rows[g][:, hh:hh + 1] * LOG2E
            hi = y.astype(BF16).astype(F32)
            mid = (y - hi).astype(BF16).astype(F32)
            lo = y - hi - mid
            parts = jnp.where(l64 == 0, hi, jnp.where(l64 == 1, mid, lo))
            ext_q = jnp.where(l64 < 3, parts, jnp.where(l64 < 6, 1.0, 0.0))
            parts = jnp.where(l64 == 3, hi, jnp.where(l64 == 4, mid, lo))
            ext_k = jnp.where(l64 < 3, 1.0, jnp.where(l64 < 6, -parts, 0.0))
            qa += [q[:, hh * FOX_HD:(hh + 1) * FOX_HD], ext_q]
            ka += [k[:, hh * FOX_HD:(hh + 1) * FOX_HD], ext_k]
        qat_ref[g] = jnp.concatenate(qa, axis=1).T.astype(BF16)
        ka_ref[g] = jnp.concatenate(ka, axis=1).astype(BF16)
        kt_ref[g] = k.T
        vt = seq(v_all, g).T
        vt_ref[g] = vt
        vt16_ref[g] = vt.astype(BF16)

    xbc_all = proj(C_XBC, SSD_CONV_DIM)
    r_end = t_real - (t_real - 1) // CHUNK * CHUNK
    for g in range(grp):
        xb_scr[g, 8:8 + CHUNK, :] = seq(xbc_all, g)
        conv = cb_ref[...]
        for w in range(SSD_CONV):
            conv = conv + xb_scr[g, pl.ds(8 - (SSD_CONV - 1) + w, CHUNK), :] * cw_ref[w:w + 1, :]
        xbc_ref[g] = _silu(conv)
        convp_ref[g] = xb_scr[g, r_end:r_end + 8, :]
        xb_scr[g, 0:8, :] = xb_scr[g, CHUNK:CHUNK + 8, :]

    z_ref[...] = proj(C_Z, SSD_W).reshape(grp, CHUNK, SSD_W)
    gqk_ref[...] = proj(C_GQ, 2 * GLA_KW).reshape(grp, CHUNK, 2 * GLA_KW)
    gv_ref[...] = proj(C_GV, GLA_W).reshape(grp, CHUNK, GLA_W)
    gg_ref[...] = proj(C_GG, GLA_W).reshape(grp, CHUNK, GLA_W)


def _proj_p(x3, nrm, w, qg, kg, fb, dtb, cw, cb, g64, ltri, t_real, l):
    nb, t_pad, _ = x3.shape
    nj = t_pad // CHUNK
    grp = math.gcd(nb, PROJ_GROUP)
    rows = lambda w_: pl.BlockSpec((grp, CHUNK, w_), lambda b, j: (b, j, 0))
    cols = lambda h_: pl.BlockSpec((grp, h_, CHUNK), lambda b, j: (b, 0, j))
    out_shape = [
        jax.ShapeDtypeStruct((nb, FOX_AUG, t_pad), BF16),
        jax.ShapeDtypeStruct((nb, t_pad, FOX_AUG), BF16),
        jax.ShapeDtypeStruct((nb, FOX_W, t_pad), F32),
        jax.ShapeDtypeStruct((nb, FOX_W, t_pad), F32),
        jax.ShapeDtypeStruct((nb, FOX_W, t_pad), BF16),
        jax.ShapeDtypeStruct((nb, t_pad, SSD_W), F32),
        jax.ShapeDtypeStruct((nb, t_pad, SSD_CONV_DIM), F32),
        jax.ShapeDtypeStruct((nb, t_pad, 2 * GLA_KW), F32),
        jax.ShapeDtypeStruct((nb, t_pad, GLA_W), F32),
        jax.ShapeDtypeStruct((nb, t_pad, GLA_W), F32),
        jax.ShapeDtypeStruct((nb, t_pad, LANES), F32),
        jax.ShapeDtypeStruct((nb, 16, t_pad), F32),
        jax.ShapeDtypeStruct((nb, 8, SSD_CONV_DIM), F32),
    ]
    out_specs = [cols(FOX_AUG), rows(FOX_AUG), cols(FOX_W), cols(FOX_W), cols(FOX_W), rows(SSD_W),
                 rows(SSD_CONV_DIM), rows(2 * GLA_KW), rows(GLA_W), rows(GLA_W), rows(LANES), cols(16),
                 pl.BlockSpec((grp, 8, SSD_CONV_DIM), lambda b, j: (b, 0, 0))]
    in_specs = [rows(D_MODEL), _lspec((1, D_MODEL), l), _lspec((D_MODEL, N_PROJ), l),
                _lspec((1, FOX_W), l), _lspec((1, FOX_W), l), _lspec((1, LANES), l),
                _lspec((1, LANES), l), _lspec((8, SSD_CONV_DIM), l), _lspec((1, SSD_CONV_DIM), l),
                _const_spec((FOX_W, FOX_W)), _const_spec((CHUNK, CHUNK))]
    return pl.pallas_call(
        functools.partial(_proj_p_kernel, t_real=t_real, grp=grp), out_shape=out_shape,
        grid=(nb // grp, nj), in_specs=in_specs, out_specs=out_specs,
        scratch_shapes=[pltpu.VMEM((grp, 8 + CHUNK, SSD_CONV_DIM), F32),
                        pltpu.VMEM((grp, 8, LANES), F32)],
        compiler_params=_params(("parallel", "arbitrary")), name="proj_p")(
            x3, nrm, w, qg, kg, fb, dtb, cw, cb, g64, ltri)


def _proj_s_kernel(x_ref, nrm_ref, w_ref, qg_ref, kg_ref, fb_ref, dtb_ref, cw_ref, cb_ref, g64_ref,
                   cst_ref,
                   q_ref, kr_ref, vr_ref, pt_ref, cum_ref, convs_ref,
                   hist_scr, carry_scr, *, n_t):
    t = pl.program_id(0)

    @pl.when(t == 0)
    def _():
        hist_scr[0:SSD_CONV - 1] = cst_ref[...]
        carry_scr[...] = jnp.zeros((LANES, LANES), F32)

    h = _rms(x_ref[...], nrm_ref[...]).astype(BF16)
    p = jnp.dot(h, w_ref[...], preferred_element_type=F32)

    q_ref[0] = _head_norm(p[:, C_Q:C_Q + FOX_W], qg_ref[...], g64_ref) * (FOX_HD ** -0.5)
    k = _head_norm(p[:, C_K:C_K + FOX_W], kg_ref[...], g64_ref)
    v = p[:, C_V:C_V + FOX_W]
    kr_ref[0] = k
    vr_ref[0] = v
    for c in range(FOX_W // LANES):
        pt_ref[0, R_K + c * LANES:R_K + (c + 1) * LANES, :] = k[:, c * LANES:(c + 1) * LANES].T
        pt_ref[0, R_V + c * LANES:R_V + (c + 1) * LANES, :] = v[:, c * LANES:(c + 1) * LANES].T

    hist_scr[pl.ds(SSD_CONV - 1 + t, 1)] = p[:, C_XBC:C_XBC + SSD_CONV_DIM][None]
    conv = cb_ref[...]
    for w in range(SSD_CONV):
        conv = conv + hist_scr[t + w] * cw_ref[w:w + 1, :]
    xbc = _silu(conv)

    @pl.when(t == n_t - 1)
    def _():
        convs_ref[...] = hist_scr[n_t:n_t + SSD_CONV - 1]

    sm = p[:, C_SM:C_SM + LANES]
    lane = lax.broadcasted_iota(jnp.int32, (LANES, LANES), 1)
    logf, dt = _small_block(sm, fb_ref, dtb_ref)
    cum = carry_scr[...] + jnp.where(lane < SM_DT, logf, 0.0)
    carry_scr[...] = cum
    cum_ref[0] = cum
    smc = jnp.where(lane < SM_DT, logf, jnp.where(lane < SM_LR, dt, sm))

    def put_t(r0, val):
        for c in range(val.shape[1] // LANES):
            pt_ref[0, r0 + c * LANES:r0 + (c + 1) * LANES, :] = val[:, c * LANES:(c + 1) * LANES].T

    put_t(R_Z, p[:, C_Z:C_Z + SSD_W])
    put_t(R_XBC, xbc)
    put_t(R_GQ, p[:, C_GQ:C_GQ + GLA_KW] * (GLA_DK ** -0.5))
    put_t(R_GK, p[:, C_GK:C_GK + GLA_KW])
    put_t(R_GV, p[:, C_GV:C_GV + GLA_W])
    put_t(R_GG, p[:, C_GG:C_GG + GLA_W])
    put_t(R_SM, smc)


def _proj_s(x, nrm, w, qg, kg, fb, dtb, cw, cb, g64, conv_state, l):
    n_t = x.shape[0] // LANES
    per_t = lambda a, b: pl.BlockSpec((1, a, b), lambda t: (t, 0, 0))
    out_shape = [
        jax.ShapeDtypeStruct((n_t, LANES, FOX_W), F32),
        jax.ShapeDtypeStruct((n_t, LANES, FOX_W), F32),
        jax.ShapeDtypeStruct((n_t, LANES, FOX_W), F32),
        jax.ShapeDtypeStruct((n_t, N_PT, LANES), F32),
        jax.ShapeDtypeStruct((n_t, LANES, LANES), F32),
        jax.ShapeDtypeStruct((SSD_CONV - 1, LANES, SSD_CONV_DIM), F32),
    ]
    out_specs = [per_t(LANES, FOX_W), per_t(LANES, FOX_W), per_t(LANES, FOX_W), per_t(N_PT, LANES),
                 per_t(LANES, LANES), _const_spec((SSD_CONV - 1, LANES, SSD_CONV_DIM))]
    in_specs = [pl.BlockSpec((LANES, D_MODEL), lambda t: (t, 0)), _lspec((1, D_MODEL), l),
                _lspec((D_MODEL, N_PROJ), l), _lspec((1, FOX_W), l), _lspec((1, FOX_W), l),
                _lspec((1, LANES), l), _lspec((1, LANES), l), _lspec((8, SSD_CONV_DIM), l),
                _lspec((1, SSD_CONV_DIM), l), _const_spec((FOX_W, FOX_W)),
                _lspec((SSD_CONV - 1, LANES, SSD_CONV_DIM), l)]
    return pl.pallas_call(
        functools.partial(_proj_s_kernel, n_t=n_t), out_shape=out_shape, grid=(n_t,),
        in_specs=in_specs, out_specs=out_specs,
        scratch_shapes=[pltpu.VMEM((n_t + SSD_CONV - 1, LANES, SSD_CONV_DIM), F32),
                        pltpu.VMEM((LANES, LANES), F32)],
        compiler_params=_params(("arbitrary",)), name="proj_s")(
            x, nrm, w, qg, kg, fb, dtb, cw, cb, g64, conv_state)


def _fox_p_kernel(qat_ref, ka_ref, vt_ref, o_ref, *, kw):
    qi = pl.program_id(1)
    nsub = kw // CHUNK
    n_full = qi // nsub

    def chunk(start, width, carry, masked):
        ss = []
        for h in range(FOX_HEADS):
            ka = ka_ref[0, pl.ds(start, width), h * LANES:(h + 1) * LANES]
            ss.append(jnp.dot(ka, qat_ref[0, h * LANES:(h + 1) * LANES, :], preferred_element_type=F32))
        prs, stats = [], []
        for h in range(FOX_HEADS):
            m, l, _ = carry[h]
            s = ss[h]
            if masked:
                kpos = lax.broadcasted_iota(jnp.int32, (width, CHUNK), 0) + start
                qpos = lax.broadcasted_iota(jnp.int32, (width, CHUNK), 1) + qi * CHUNK
                s = jnp.where(kpos <= qpos, s, -jnp.inf)
            m_new = jnp.maximum(m, jnp.max(s, axis=0, keepdims=True))
            alpha = jnp.exp2(m - m_new)
            pr = jnp.exp2(s - m_new)
            stats.append((m_new, alpha * l + jnp.sum(pr, axis=0, keepdims=True), alpha))
            prs.append(pr.astype(BF16))
        out = []
        for h in range(FOX_HEADS):
            vt = vt_ref[0, h * FOX_HD:(h + 1) * FOX_HD, pl.ds(start, width)]
            m_new, l, alpha = stats[h]
            out.append((m_new, l, alpha * carry[h][2] + jnp.dot(vt, prs[h], preferred_element_type=F32)))
        return tuple(out)

    init = tuple((jnp.full((1, CHUNK), -jnp.inf, F32), jnp.zeros((1, CHUNK), F32),
                  jnp.zeros((FOX_HD, CHUNK), F32)) for _ in range(FOX_HEADS))
    carry = lax.fori_loop(0, n_full, lambda i, c: chunk(pl.multiple_of(i * kw, kw), kw, c, False), init)
    start = pl.multiple_of(n_full * kw, kw)
    for r in range(nsub):
        @pl.when(lax.rem(qi, nsub) == r)
        def _(r=r):
            fin = chunk(start, (r + 1) * CHUNK, carry, True)
            o_t = jnp.concatenate([acc / l for (_, l, acc) in fin], axis=0)
            o_ref[0] = o_t.T.astype(BF16)


def _fox_p(qat, ka, vt16):
    nb, _, t_pad = qat.shape
    nq = t_pad // CHUNK
    return pl.pallas_call(
        functools.partial(_fox_p_kernel, kw=FOX_KW),
        out_shape=jax.ShapeDtypeStruct((nb, t_pad, FOX_W), BF16), grid=(nb, nq),
        in_specs=[pl.BlockSpec((1, FOX_AUG, CHUNK), lambda b, i: (b, 0, i)),
                  pl.BlockSpec((1, t_pad, FOX_AUG), lambda b, i: (b, 0, 0)),
                  pl.BlockSpec((1, FOX_W, t_pad), lambda b, i: (b, 0, 0))],
        out_specs=pl.BlockSpec((1, CHUNK, FOX_W), lambda b, i: (b, i, 0)),
        compiler_params=_params(("parallel", "arbitrary")), name="fox_p")(qat, ka, vt16)


def _fox_s_kernel(tbl_ref, q_ref, kr_ref, vr_ref, cum_ref, lf_ref, mstrict_ref, pgsuf_ref,
                  k_hbm, v_hbm, o_ref, kbuf, vbuf, lfst, sem, *, layer, n_pages, n_t):
    b = pl.program_id(0)
    nb = pl.num_programs(0)
    slot = lax.rem(b, 2)

    def copies(seq, sl):
        out = []
        for pg in range(n_pages):
            page = tbl_ref[seq * n_pages + pg]
            out.append(pltpu.make_async_copy(k_hbm.at[layer, page], kbuf.at[sl, pg], sem.at[0, sl]))
            out.append(pltpu.make_async_copy(v_hbm.at[layer, page], vbuf.at[sl, pg], sem.at[1, sl]))
        return out

    @pl.when(b == 0)
    def _():
        lfst[...] = jnp.zeros(lfst.shape, F32)
        for c in copies(0, 0):
            c.start()

    @pl.when(b + 1 < nb)
    def _():
        for c in copies(b + 1, 1 - slot):
            c.start()

    for pg in range(n_pages):
        page = tbl_ref[b * n_pages + pg]
        lfst[pg * 8:pg * 8 + FOX_HEADS, :] = lf_ref[0, page // 2, pl.ds(lax.rem(page, 2) * FOX_HEADS,
                                                                      FOX_HEADS), :]
    lf = lfst[...]
    within = _dot_hl(lf, mstrict_ref[...])
    tot = jnp.broadcast_to(jnp.sum(lf, axis=-1, keepdims=True), lf.shape)
    dsuf = within + _dot_hl_left(pgsuf_ref[...], tot)

    lane = lax.broadcasted_iota(jnp.int32, (8, FOX_W), 1)
    rowh = lax.broadcasted_iota(jnp.int32, (8, FOX_W), 0)
    headmask = (lane // FOX_HD) == rowh
    qexp, cumcol = [], []
    cum_rows = jnp.concatenate([cum_ref[t, pl.ds(b, 1), :] for t in range(n_t)]
                               + [jnp.zeros((LANES - n_t, LANES), F32)], axis=0)
    cum_t = cum_rows.T[0:8, 0:8]
    for t in range(n_t):
        qrow = q_ref[t, pl.ds(b, 1), :]
        qexp.append(jnp.where(headmask, jnp.broadcast_to(qrow, (8, FOX_W)), 0.0))
        cumcol.append(cum_t[:, t:t + 1])
    qexp = jnp.concatenate(qexp, axis=0).astype(BF16)
    cumcol = jnp.concatenate(cumcol, axis=0)
    rows = 8 * n_t

    kn = jnp.concatenate([kr_ref[t, pl.ds(b, 1), :] for t in range(n_t)]
                         + [jnp.zeros((8 - n_t, FOX_W), F32)], axis=0)
    vn = jnp.concatenate([vr_ref[t, pl.ds(b, 1), :] for t in range(n_t)]
                         + [jnp.zeros((8 - n_t, FOX_W), F32)], axis=0)
    s_new = _dot_nt(qexp, kn) + cumcol - jnp.concatenate([cum_t] * n_t, axis=0)
    tq = lax.broadcasted_iota(jnp.int32, (rows, 8), 0) // 8
    tk = lax.broadcasted_iota(jnp.int32, (rows, 8), 1)
    s_new = jnp.where(tk <= tq, s_new, -jnp.inf)

    for c in copies(b, slot):
        c.wait()

    s_pg = []
    for pg in range(n_pages):
        bias = jnp.concatenate([dsuf[pg * 8:(pg + 1) * 8, :]] * n_t, axis=0) + cumcol
        s_pg.append(_dot(qexp, kbuf[slot, pg]) + bias)
    m = s_pg[0]
    for s in s_pg[1:]:
        m = jnp.maximum(m, s)
    m = jnp.maximum(jnp.max(m, axis=-1, keepdims=True), jnp.max(s_new, axis=-1, keepdims=True))
    p_new = jnp.exp(s_new - m)
    l = jnp.sum(p_new, axis=-1, keepdims=True)
    acc = _dot(p_new, vn)
    lsum = None
    for pg in range(n_pages):
        pr = jnp.exp(s_pg[pg] - m)
        lsum = pr if lsum is None else lsum + pr
        acc = acc + _dot_nt(pr, vbuf[slot, pg])
    l = l + jnp.sum(lsum, axis=-1, keepdims=True)
    o = acc / l
    for t in range(n_t):
        ot = jnp.where(headmask, o[t * 8:(t + 1) * 8, :], 0.0)
        o_ref[t, pl.ds(b, 1), :] = jnp.sum(ot, axis=0, keepdims=True)


def _fox_s(tbl, q, kr, vr, cum, lfpool, mstrict, pgsuf, k_cache, v_cache, layer):
    n_t, nb, _ = q.shape
    n_pages = tbl.shape[0] // nb
    whole = lambda a: pl.BlockSpec(a.shape, lambda b, tbl_: (0,) * a.ndim)
    grid_spec = pltpu.PrefetchScalarGridSpec(
        num_scalar_prefetch=1, grid=(nb,),
        in_specs=[whole(q), whole(kr), whole(vr), whole(cum),
                  pl.BlockSpec((1,) + lfpool.shape[1:], lambda b, tbl_: (layer, 0, 0, 0)),
                  whole(mstrict), whole(pgsuf),
                  pl.BlockSpec(memory_space=pl.ANY), pl.BlockSpec(memory_space=pl.ANY)],
        out_specs=pl.BlockSpec((n_t, nb, FOX_W), lambda b, tbl_: (0, 0, 0)),
        scratch_shapes=[pltpu.VMEM((2, n_pages, FOX_W, LANES), F32),
                        pltpu.VMEM((2, n_pages, FOX_W, LANES), F32),
                        pltpu.VMEM((n_pages * 8, LANES), F32),
                        pltpu.SemaphoreType.DMA((2, 2))])
    return pl.pallas_call(
        functools.partial(_fox_s_kernel, layer=layer, n_pages=n_pages, n_t=n_t),
        out_shape=jax.ShapeDtypeStruct((n_t, nb, FOX_W), F32), grid_spec=grid_spec,
        compiler_params=_params(("arbitrary",)), name="fox_s")(
            tbl, q, kr, vr, cum, lfpool, mstrict, pgsuf, k_cache, v_cache)


def _ssd_p_kernel(z_ref, xbc_ref, smr_ref, smt_ref, arow_ref, acol_ref, ltri_ref, utri_ref, exph_ref,
                  dvec_ref, nrm_ref, o_ref, h_ref, ht_scr, *, grp):
    j = pl.program_id(1)
    nj = pl.num_programs(1)

    @pl.when(j == 0)
    def _():
        ht_scr[...] = jnp.zeros(ht_scr.shape, F32)

    finals = [_ssd_p_chunk(s, z_ref, xbc_ref, smr_ref, smt_ref, arow_ref, acol_ref, ltri_ref, utri_ref,
                           exph_ref, dvec_ref, nrm_ref, o_ref, ht_scr) for s in range(grp)]

    @pl.when(j == nj - 1)
    def _():
        rep = SSD_HEADS // SSD_GROUPS
        for s in range(grp):
            h_t = finals[s].T
            for hh in range(SSD_HEADS):
                g = hh // rep
                h_ref[s, hh] = h_t[hh * SSD_HD:(hh + 1) * SSD_HD, g * SSD_STATE:(g + 1) * SSD_STATE]


def _ssd_p_chunk(s, z_ref, xbc_ref, smr_ref, smt_ref, arow_ref, acol_ref, ltri_ref, utri_ref, exph_ref,
                 dvec_ref, nrm_ref, o_ref, ht_scr):
    smr = smr_ref[s]
    exph = exph_ref[...]
    cs_rows = _dot_hl_left(ltri_ref[...], smr * arow_ref[...])
    cs_t = _dot_hl(smt_ref[s] * acol_ref[...], utri_ref[...])
    dt_x = _dot_hl(smr, exph)
    c_hi, c_lo = _split(cs_rows)
    c_lo2 = (cs_rows - c_hi.astype(F32) - c_lo.astype(F32)).astype(BF16)
    cs_x = (jnp.dot(c_hi, exph, preferred_element_type=F32)
            + jnp.dot(c_lo, exph, preferred_element_type=F32)
            + jnp.dot(c_lo2, exph, preferred_element_type=F32))
    end_x = cs_x[CHUNK - 1:CHUNK, :]
    xbc = xbc_ref[s]
    x = xbc[:, 0:SSD_W]
    b_all = xbc[:, SSD_W:SSD_W + SSD_BC].astype(BF16)
    c_all = xbc[:, SSD_W + SSD_BC:SSD_W + 2 * SSD_BC].astype(BF16)
    xdt = x * dt_x
    xdt16 = xdt.astype(BF16)
    ht = ht_scr[s]
    y_state = jnp.dot(c_all, ht.astype(BF16), preferred_element_type=F32)
    upd = _dot_tn(b_all, xdt * jnp.exp(end_x - cs_x))
    cbs = [_dot_nt(c_all[:, g * SSD_STATE:(g + 1) * SSD_STATE], b_all[:, g * SSD_STATE:(g + 1) * SSD_STATE])
           for g in range(SSD_GROUPS)]

    gmask = (lax.broadcasted_iota(jnp.int32, (SSD_BC, SSD_W), 0) // SSD_STATE
             == lax.broadcasted_iota(jnp.int32, (SSD_BC, SSD_W), 1) // (SSD_W // SSD_GROUPS))
    ht_new = ht * jnp.exp(end_x) + jnp.where(gmask, upd, 0.0)
    ht_scr[s] = ht_new

    tril = (lax.broadcasted_iota(jnp.int32, (CHUNK, CHUNK), 1)
            <= lax.broadcasted_iota(jnp.int32, (CHUNK, CHUNK), 0))
    low = lax.broadcasted_iota(jnp.int32, (CHUNK, LANES), 1) < SSD_HD
    rep = SSD_HEADS // SSD_GROUPS
    pairs = []
    for pr in range(SSD_HEADS // 2):
        xp = xdt16[:, pr * LANES:(pr + 1) * LANES]
        ys = []
        for hh in (2 * pr, 2 * pr + 1):
            col = SM_DT + hh
            lm = jnp.exp(jnp.where(tril, cs_rows[:, col:col + 1] - cs_t[col:col + 1, :], -jnp.inf))
            ys.append(jnp.dot((cbs[hh // rep] * lm).astype(BF16), xp, preferred_element_type=F32))
        pairs.append(jnp.where(low, ys[0], ys[1]))
    y = jnp.concatenate(pairs, axis=1) + y_state * jnp.exp(cs_x)
    y = (y + x * dvec_ref[...]) * _silu(z_ref[s])
    gw = SSD_W // SSD_GROUPS
    for g in range(SSD_GROUPS):
        yg = y[:, g * gw:(g + 1) * gw]
        yn = yg * lax.rsqrt(jnp.mean(yg * yg, axis=-1, keepdims=True) + EPS)
        o_ref[s, :, g * gw:(g + 1) * gw] = (yn * nrm_ref[:, g * gw:(g + 1) * gw]).astype(BF16)
    return ht_new


def _ssd_p(z, xbc, smr, smt, arow, acol, ltri, utri, exph, dvec, nrm, l):
    nb, t_pad, _ = z.shape
    nj = t_pad // CHUNK
    grp = math.gcd(nb, MIX_GROUP)
    rows = lambda w_: pl.BlockSpec((grp, CHUNK, w_), lambda b, j: (b, j, 0))
    return pl.pallas_call(
        functools.partial(_ssd_p_kernel, grp=grp),
        out_shape=[jax.ShapeDtypeStruct((nb, t_pad, SSD_W), BF16),
                   jax.ShapeDtypeStruct((nb, SSD_HEADS, SSD_HD, SSD_STATE), F32)],
        grid=(nb // grp, nj),
        in_specs=[rows(SSD_W), rows(SSD_CONV_DIM), rows(LANES),
                  pl.BlockSpec((grp, 16, CHUNK), lambda b, j: (b, 0, j)),
                  _lspec((1, LANES), l), _lspec((16, 1), l), _const_spec((CHUNK, CHUNK)),
                  _const_spec((CHUNK, CHUNK)), _const_spec((LANES, SSD_W)),
                  _lspec((1, SSD_W), l), _lspec((1, SSD_W), l)],
        out_specs=[rows(SSD_W),
                   pl.BlockSpec((grp, SSD_HEADS, SSD_HD, SSD_STATE), lambda b, j: (b, 0, 0, 0))],
        scratch_shapes=[pltpu.VMEM((grp, SSD_BC, SSD_W), F32)],
        compiler_params=_params(("parallel", "arbitrary")), name="ssd_p")(
            z, xbc, smr, smt, arow, acol, ltri, utri, exph, dvec, nrm)


def _ssd_s_kernel(x_ref, b_ref, c_ref, sm_ref, xg_ref, zg_ref, a_ref, d_ref, nrm_ref, st_ref,
                  o_ref, so_ref, ybuf, *, n_t):
    hh = pl.program_id(0)
    rep = SSD_HEADS // SSD_GROUPS
    gw = SSD_W // SSD_GROUPS
    dt = [sm_ref[t, pl.ds(SM_DT + hh, 1), :] for t in range(n_t)]
    dec = [jnp.exp(dt[t] * a_ref[pl.ds(hh, 1), :]) for t in range(n_t)]

    def body(p, carry):
        hp = st_ref[0, 0, p]
        for t in range(n_t):
            hp = hp * dec[t] + b_ref[t] * (x_ref[t, pl.ds(p, 1), :] * dt[t])
            ybuf[t, pl.ds(hh * SSD_HD + p, 1), :] = jnp.sum(c_ref[t] * hp, axis=0, keepdims=True)
        so_ref[0, 0, p] = hp
        return carry

    lax.fori_loop(0, SSD_HD, body, 0)

    @pl.when(lax.rem(hh, rep) == rep - 1)
    def _():
        r0 = pl.multiple_of((hh // rep) * gw, gw)
        for t in range(n_t):
            y = ybuf[t, pl.ds(r0, gw), :] + xg_ref[t] * d_ref[pl.ds(r0, gw), :]
            y = y * _silu(zg_ref[t])
            yn = y * lax.rsqrt(jnp.mean(y * y, axis=0, keepdims=True) + EPS)
            o_ref[t, pl.ds(r0, gw), :] = yn * nrm_ref[pl.ds(r0, gw), :]


def _ssd_s(pt, a_b, d_b, nrm_b, state, layer):
    n_t = pt.shape[0]
    rep = SSD_HEADS // SSD_GROUPS
    gw = SSD_W // SSD_GROUPS
    blk = lambda h_, f: pl.BlockSpec((n_t, h_, LANES), lambda hh: (0, f(hh), 0))
    st_spec = pl.BlockSpec((1, 1, SSD_HD, SSD_STATE, LANES), lambda hh: (layer, hh, 0, 0, 0))
    return pl.pallas_call(
        functools.partial(_ssd_s_kernel, n_t=n_t),
        out_shape=[jax.ShapeDtypeStruct((n_t, SSD_W, LANES), F32),
                   jax.ShapeDtypeStruct((1, SSD_HEADS, SSD_HD, SSD_STATE, LANES), F32)],
        grid=(SSD_HEADS,),
        in_specs=[blk(SSD_HD, lambda hh: R_XBC // SSD_HD + hh),
                  blk(SSD_STATE, lambda hh: (R_XBC + SSD_W) // SSD_STATE + hh // rep),
                  blk(SSD_STATE, lambda hh: (R_XBC + SSD_W + SSD_BC) // SSD_STATE + hh // rep),
                  blk(LANES, lambda hh: R_SM // LANES),
                  blk(gw, lambda hh: R_XBC // gw + hh // rep),
                  blk(gw, lambda hh: R_Z // gw + hh // rep),
                  _lspec((SSD_HEADS, LANES), layer), _lspec((SSD_W, LANES), layer),
                  _lspec((SSD_W, LANES), layer), st_spec],
        out_specs=[_const_spec((n_t, SSD_W, LANES)),
                   pl.BlockSpec((1, 1, SSD_HD, SSD_STATE, LANES), lambda hh: (0, hh, 0, 0, 0))],
        scratch_shapes=[pltpu.VMEM((n_t, SSD_W, LANES), F32)],
        compiler_params=_params(("arbitrary",)), name="ssd_s")(
            pt, pt, pt, pt, pt, pt, a_b, d_b, nrm_b, state)


def _gla_p_kernel(gqk_ref, gv_ref, gg_ref, smr_ref, wg_ref, gb_ref, bl_ref, be_ref, g64_ref, gn_ref,
                  o_ref, s_ref, st_scr, *, t_real, grp):
    j = pl.program_id(1)
    nj = pl.num_programs(1)

    @pl.when(j == 0)
    def _():
        st_scr[...] = jnp.zeros(st_scr.shape, F32)

    finals = [_gla_p_chunk(s, j, gqk_ref, gv_ref, gg_ref, smr_ref, wg_ref, gb_ref, bl_ref, be_ref,
                           g64_ref, gn_ref, o_ref, st_scr, t_real) for s in range(grp)]

    @pl.when(j == nj - 1)
    def _():
        for s in range(grp):
            s_kv = finals[s].T
            for h in range(GLA_HEADS):
                s_ref[s, h] = s_kv[h * GLA_DK:(h + 1) * GLA_DK, h * GLA_DV:(h + 1) * GLA_DV]


def _gla_p_chunk(s, j, gqk_ref, gv_ref, gg_ref, smr_ref, wg_ref, gb_ref, bl_ref, be_ref, g64_ref,
                 gn_ref, o_ref, st_scr, t_real):
    sub = GLA_SUB
    n_sub = CHUNK // sub
    row = lax.broadcasted_iota(jnp.int32, (CHUNK, GLA_KW), 0)
    valid = j * CHUNK + row < t_real
    glog = _log_sigmoid(_dot(smr_ref[s], wg_ref[...]) + gb_ref[...]) * (1.0 / GLA_TAU)
    glog = jnp.where(valid, glog, 0.0)
    gqk = gqk_ref[s]
    gq = gqk[:, 0:GLA_KW] * (GLA_DK ** -0.5)
    gk = jnp.where(valid, gqk[:, GLA_KW:2 * GLA_KW], 0.0)
    v = gv_ref[s]
    bcl = _dot_hl_left(bl_ref[...], glog)
    tot = _dot_hl_left(be_ref[...], glog)
    qe = (gq * jnp.exp(bcl)).astype(BF16)
    kp = gk * jnp.exp(-bcl)
    kend = (gk * jnp.exp(tot - bcl)).astype(BF16)
    dec = jnp.exp(tot)

    r_k = lax.broadcasted_iota(jnp.int32, (GLA_HEADS * sub, GLA_KW), 0) // sub
    c_k = lax.broadcasted_iota(jnp.int32, (GLA_HEADS * sub, GLA_KW), 1) // GLA_DK
    r_v = lax.broadcasted_iota(jnp.int32, (GLA_HEADS * sub, GLA_W), 0) // sub
    c_v = lax.broadcasted_iota(jnp.int32, (GLA_HEADS * sub, GLA_W), 1) // GLA_DV
    r_s = lax.broadcasted_iota(jnp.int32, (GLA_W, GLA_KW), 0) // GLA_DV
    c_s = lax.broadcasted_iota(jnp.int32, (GLA_W, GLA_KW), 1) // GLA_DK
    causal = (lax.broadcasted_iota(jnp.int32, (sub, GLA_HEADS * sub), 1) % sub
              <= lax.broadcasted_iota(jnp.int32, (sub, GLA_HEADS * sub), 0))
    sls = [slice(i * sub, (i + 1) * sub) for i in range(n_sub)]
    atts, upds = [], []
    for sl in sls:
        kbd = jnp.where(r_k == c_k, jnp.concatenate([kp[sl]] * GLA_HEADS, axis=0), 0.0)
        atts.append(_dot_nt(qe[sl], kbd))
        upds.append(_dot_tn(v[sl], kend[sl]))
    sts = [st_scr[s]]
    for i in range(n_sub):
        sts.append(sts[i] * dec[i * sub:i * sub + 1, :] + jnp.where(r_s == c_s, upds[i], 0.0))
    st_scr[s] = sts[n_sub]
    outs = []
    for i, sl in enumerate(sls):
        vbd = jnp.where(r_v == c_v, jnp.concatenate([v[sl]] * GLA_HEADS, axis=0), 0.0)
        outs.append(_dot(jnp.where(causal, atts[i], 0.0), vbd) + _dot_nt(qe[sl], sts[i]))
    o = jnp.concatenate(outs, axis=0)
    msq = _dot_hl(o * o, g64_ref[...])
    o_ref[s] = (o * lax.rsqrt(msq + EPS) * gn_ref[...] * _silu(gg_ref[s])).astype(BF16)
    return sts[n_sub]


def _gla_p(gqk, gv, gg, smr, wgp, gb, bl, be, g64, gn, t_real, l):
    nb, t_pad, _ = gv.shape
    nj = t_pad // CHUNK
    grp = math.gcd(nb, MIX_GROUP)
    rows = lambda w_: pl.BlockSpec((grp, CHUNK, w_), lambda b, j: (b, j, 0))
    return pl.pallas_call(
        functools.partial(_gla_p_kernel, t_real=t_real, grp=grp),
        out_shape=[jax.ShapeDtypeStruct((nb, t_pad, GLA_W), BF16),
                   jax.ShapeDtypeStruct((nb, GLA_HEADS, GLA_DK, GLA_DV), F32)],
        grid=(nb // grp, nj),
        in_specs=[rows(2 * GLA_KW), rows(GLA_W), rows(GLA_W), rows(LANES),
                  _lspec((LANES, GLA_KW), l), _lspec((1, GLA_KW), l),
                  _const_spec((CHUNK, CHUNK)), _const_spec((CHUNK, CHUNK)),
                  _const_spec((GLA_W, GLA_W)), _lspec((1, GLA_W), l)],
        out_specs=[rows(GLA_W),
                   pl.BlockSpec((grp, GLA_HEADS, GLA_DK, GLA_DV), lambda b, j: (b, 0, 0, 0))],
        scratch_shapes=[pltpu.VMEM((grp, GLA_W, GLA_KW), F32)],
        compiler_params=_params(("parallel", "arbitrary")), name="gla_p")(
            gqk, gv, gg, smr, wgp, gb, bl, be, g64, gn)


def _gla_s_kernel(q_ref, k_ref, v_ref, gg_ref, sm_ref, wgt_ref, gb_ref, gn_ref, st_ref,
                  o_ref, so_ref, eg_scr, acc_scr, *, n_t):
    for t in range(n_t):
        glog = _log_sigmoid(_dot(wgt_ref[...], sm_ref[t]) + gb_ref[...]) * (1.0 / GLA_TAU)
        eg_scr[t] = jnp.exp(glog)
        acc_scr[t] = jnp.zeros((GLA_DV, LANES), F32)

    def body(kk, carry):
        s = st_ref[0, 0, kk]
        for t in range(n_t):
            s = s * eg_scr[t, pl.ds(kk, 1), :] + k_ref[t, pl.ds(kk, 1), :] * v_ref[t]
            acc_scr[t] = acc_scr[t] + q_ref[t, pl.ds(kk, 1), :] * s
        so_ref[0, 0, kk] = s
        return carry

    lax.fori_loop(0, GLA_DK, body, 0)
    for t in range(n_t):
        o = acc_scr[t]
        on = o * lax.rsqrt(jnp.mean(o * o, axis=0, keepdims=True) + EPS)
        o_ref[t] = on * gn_ref[...] * _silu(gg_ref[t])


def _gla_s(pt, wgt, gb_b, gn_b, state, layer):
    n_t = pt.shape[0]
    blk = lambda h_, f: pl.BlockSpec((n_t, h_, LANES), lambda hh: (0, f(hh), 0))
    return pl.pallas_call(
        functools.partial(_gla_s_kernel, n_t=n_t),
        out_shape=[jax.ShapeDtypeStruct((n_t, GLA_W, LANES), F32),
                   jax.ShapeDtypeStruct((1, GLA_HEADS, GLA_DK, GLA_DV, LANES), F32)],
        grid=(GLA_HEADS,),
        in_specs=[blk(GLA_DK, lambda hh: R_GQ // GLA_DK + hh),
                  blk(GLA_DK, lambda hh: R_GK // GLA_DK + hh),
                  blk(GLA_DV, lambda hh: R_GV // GLA_DV + hh),
                  blk(GLA_DV, lambda hh: R_GG // GLA_DV + hh),
                  blk(LANES, lambda hh: R_SM // LANES),
                  pl.BlockSpec((None, GLA_DK, LANES), lambda hh: (layer, hh, 0)),
                  pl.BlockSpec((None, GLA_DK, LANES), lambda hh: (layer, hh, 0)),
                  _lspec((GLA_DV, LANES), layer),
                  pl.BlockSpec((1, 1, GLA_DK, GLA_DV, LANES), lambda hh: (layer, hh, 0, 0, 0))],
        out_specs=[blk(GLA_DV, lambda hh: hh),
                   pl.BlockSpec((1, 1, GLA_DK, GLA_DV, LANES), lambda hh: (0, hh, 0, 0, 0))],
        scratch_shapes=[pltpu.VMEM((n_t, GLA_DK, LANES), F32), pltpu.VMEM((n_t, GLA_DV, LANES), F32)],
        compiler_params=_params(("arbitrary",)), name="gla_s")(
            pt, pt, pt, pt, pt, wgt, gb_b, gn_b, state)


def _tri_consts(n_pages):
    i = np.arange(CHUNK)
    ltri = (i[None, :] <= i[:, None]).astype(np.float32)
    utri = ltri.T
    mstrict = (i[:, None] > i[None, :]).astype(np.float32)
    same = (i[:, None] // GLA_SUB) == (i[None, :] // GLA_SUB)
    bl = (same & (i[None, :] <= i[:, None])).astype(np.float32)
    be = same.astype(np.float32)
    r = np.arange(n_pages * 8)
    pgsuf = ((r[:, None] % 8 == r[None, :] % 8) & (r[None, :] // 8 > r[:, None] // 8)).astype(np.float32)
    h = np.arange(FOX_W)
    g64 = ((h[:, None] // 64) == (h[None, :] // 64)).astype(np.float32) / 64.0
    cw = np.arange(SSD_W)
    exph = (i[:, None] == SM_DT + cw[None, :] // SSD_HD).astype(np.float32)
    c = lambda a: jnp.asarray(a, BF16)
    return dict(ltri=c(ltri), utri=c(utri), mstrict=c(mstrict), bl=c(bl), be=c(be), pgsuf=c(pgsuf),
                g64=c(g64), exph=c(exph))


def kernel(x_prompt, x_sample, cache_fox_k, cache_fox_v, cache_fox_logf, state_ssm, state_conv,
           state_gla, page_table, meta_tokens, ffn1_norm, ffn1_w_in, ffn1_w_out, mix_norm, w_mix_in,
           fox_q_norm, fox_k_norm, fox_f_bias, ssd_conv_w, ssd_conv_b, ssd_dt_bias, ssd_a_log, ssd_d,
           ssd_norm, gla_w_gate, gla_gate_bias, gla_norm, w_mix_out, ffn2_norm, ffn2_w_in, ffn2_w_out):
    nbp, seq, _ = x_prompt.shape
    nbs, n_t, _ = x_sample.shape
    depth = ffn1_norm.shape[0]
    assert nbs == LANES and seq % CHUNK == 0
    t_real = N_META + seq
    t_pad = -(-t_real // CHUNK) * CHUNK
    n_pool, page_size = cache_fox_k.shape[1], cache_fox_k.shape[2]
    assert page_size == LANES and n_pool % 2 == 0
    n_pages = page_table.shape[1]
    consts = _tri_consts(n_pages)

    meta = jnp.broadcast_to(meta_tokens[None], (nbp, N_META, D_MODEL))
    xp = jnp.concatenate([meta, x_prompt, jnp.zeros((nbp, t_pad - t_real, D_MODEL), F32)], axis=1)
    xp = xp.reshape(nbp * t_pad, D_MODEL)
    xs = jnp.transpose(x_sample, (1, 0, 2)).reshape(n_t * nbs, D_MODEL)

    k_cache = jnp.transpose(cache_fox_k, (0, 1, 3, 4, 2)).reshape(depth, n_pool, FOX_W, page_size)
    v_cache = jnp.transpose(cache_fox_v, (0, 1, 3, 4, 2)).reshape(depth, n_pool, FOX_W, page_size)
    lfpool = jnp.transpose(cache_fox_logf, (0, 1, 3, 2)).reshape(depth, n_pool // 2, 8, page_size)
    ssm_in = jnp.transpose(state_ssm, (0, 2, 3, 4, 1))
    gla_in = jnp.transpose(state_gla, (0, 2, 3, 4, 1))
    conv_in = jnp.transpose(state_conv, (0, 2, 1, 3))
    tbl = page_table.reshape(-1).astype(jnp.int32)

    w1i, w1o = ffn1_w_in.astype(BF16), ffn1_w_out.astype(BF16)
    w2i, w2o = ffn2_w_in.astype(BF16), ffn2_w_out.astype(BF16)
    wmo = w_mix_out.astype(BF16)
    col = [0] + list(IN_SPLITS) + [w_mix_in.shape[-1]]
    part = lambda i: w_mix_in[:, :, col[i]:col[i + 1]]
    fq, fk, fv, ff, sz, sxbc, sdt, gq, gk, gv, glr, gg = [part(i) for i in range(len(IN_SIZES))]
    n_small = FOX_HEADS + SSD_HEADS + GLA_RANK
    wproj = jnp.concatenate([fq, fk, fv, sz, sxbc, gq, gk, gv, gg, ff, sdt, glr,
                             jnp.zeros((depth, D_MODEL, LANES - n_small), F32)], axis=-1).astype(BF16)
    n1, n2, nm = ffn1_norm[:, None], ffn2_norm[:, None], mix_norm[:, None]
    qg = jnp.tile(fox_q_norm, (1, FOX_HEADS))[:, None]
    kg = jnp.tile(fox_k_norm, (1, FOX_HEADS))[:, None]
    a = -jnp.exp(ssd_a_log)
    lanes_pad = lambda v, off: jnp.pad(v, ((0, 0), (off, LANES - off - v.shape[1])))[:, None]
    fb, dtb, arow = lanes_pad(fox_f_bias, SM_F), lanes_pad(ssd_dt_bias, SM_DT), lanes_pad(a, SM_DT)
    acol = jnp.pad(a, ((0, 0), (SM_DT, 16 - SM_DT - SSD_HEADS)))[:, :, None]
    cw = jnp.pad(ssd_conv_w, ((0, 0), (0, 8 - SSD_CONV), (0, 0)))
    cb = ssd_conv_b[:, None]
    d_rep = jnp.repeat(ssd_d, SSD_HD, axis=1)
    dvec, snrm = d_rep[:, None], ssd_norm[:, None]
    wgp = jnp.pad(gla_w_gate, ((0, 0), (SM_LR, LANES - SM_LR - GLA_RANK), (0, 0))).astype(BF16)
    gb = gla_gate_bias[:, None]
    gn = jnp.tile(gla_norm, (1, GLA_HEADS))[:, None]
    on_lanes = lambda v: jnp.broadcast_to(v[:, :, None], v.shape + (LANES,))
    a_b, d_b, snrm_b = on_lanes(a), on_lanes(d_rep), on_lanes(ssd_norm)
    wgt = jnp.pad(jnp.transpose(gla_w_gate, (0, 2, 1)),
                  ((0, 0), (0, 0), (SM_LR, LANES - SM_LR - GLA_RANK))).astype(BF16)
    gb_b, gn_b = on_lanes(gla_gate_bias), on_lanes(gla_norm)

    outs_p = [[] for _ in range(6)]
    outs_s = [[] for _ in range(6)]
    for l in range(depth):
        xp = _ffn(xp, n1, w1i, w1o, FFN_TM, l)
        (qat, kaug, kt, vt, vt16, z, xbc, gqk, gvv, ggg, smr, smt, convp) = _proj_p(
            xp.reshape(nbp, t_pad, D_MODEL), nm, wproj, qg, kg, fb, dtb, cw, cb, consts["g64"],
            consts["ltri"], t_real, l)
        fox_o = _fox_p(qat, kaug, vt16)
        ssd_o, ssm_p = _ssd_p(z, xbc, smr, smt, arow, acol, consts["ltri"], consts["utri"],
                              consts["exph"], dvec, snrm, l)
        gla_o, gla_p = _gla_p(gqk, gvv, ggg, smr, wgp, gb, consts["bl"], consts["be"], consts["g64"],
                              gn, t_real, l)
        xp = _mix_ffn(xp, fox_o.reshape(-1, FOX_W), ssd_o.reshape(-1, SSD_W), gla_o.reshape(-1, GLA_W),
                      wmo, n2, w2i, w2o, FFN_TM, l)
        outs_p[0].append(kt.reshape(nbp, FOX_HEADS, FOX_HD, t_pad))
        outs_p[1].append(vt.reshape(nbp, FOX_HEADS, FOX_HD, t_pad))
        outs_p[2].append(jnp.transpose(smt[:, 0:FOX_HEADS, :t_real], (0, 2, 1)))
        outs_p[3].append(ssm_p)
        outs_p[4].append(convp[:, 8 - (SSD_CONV - 1):])
        outs_p[5].append(gla_p)

        xs = _ffn(xs, n1, w1i, w1o, LANES, l)
        qs, krs, vrs, pt, cum, convs = _proj_s(xs, nm, wproj, qg, kg, fb, dtb, cw, cb, consts["g64"],
                                               conv_in, l)
        fox_os = _fox_s(tbl, qs, krs, vrs, cum, lfpool, consts["mstrict"], consts["pgsuf"],
                        k_cache, v_cache, l)
        ssd_ot, ssm_s = _ssd_s(pt, a_b, d_b, snrm_b, ssm_in, l)
        gla_ot, gla_s = _gla_s(pt, wgt, gb_b, gn_b, gla_in, l)
        xs = _mix_ffn_t(xs, fox_os, ssd_ot, gla_ot, wmo, n2, w2i, w2o, l)
        kts = pt[:, R_K:R_K + FOX_W].reshape(n_t, FOX_HEADS, FOX_HD, nbs)
        vts = pt[:, R_V:R_V + FOX_W].reshape(n_t, FOX_HEADS, FOX_HD, nbs)
        outs_s[0].append(jnp.transpose(kts, (3, 0, 1, 2)))
        outs_s[1].append(jnp.transpose(vts, (3, 0, 1, 2)))
        outs_s[2].append(jnp.transpose(pt[:, R_SM + SM_F:R_SM + SM_F + FOX_HEADS], (2, 0, 1)))
        outs_s[3].append(jnp.transpose(ssm_s[0], (3, 0, 1, 2)))
        outs_s[4].append(jnp.transpose(convs, (1, 0, 2)))
        outs_s[5].append(jnp.transpose(gla_s[0], (3, 0, 1, 2)))

    y_prompt = xp.reshape(nbp, t_pad, D_MODEL)[:, N_META:t_real]
    y_sample = jnp.transpose(xs.reshape(n_t, nbs, D_MODEL), (1, 0, 2))
    k_p = jnp.transpose(jnp.stack(outs_p[0])[..., :t_real], (0, 1, 4, 2, 3))
    v_p = jnp.transpose(jnp.stack(outs_p[1])[..., :t_real], (0, 1, 4, 2, 3))
    lf_p, ssm_p, conv_p, gla_p = [jnp.stack(a) for a in outs_p[2:]]
    k_s, v_s, lf_s, ssm_s, conv_s, gla_s = [jnp.stack(a) for a in outs_s]
    return (y_prompt, y_sample, k_p, v_p, lf_p, ssm_p, conv_p, gla_p, k_s, v_s, lf_s, ssm_s, conv_s, gla_s)
```

```python
import functools
import math

import jax
import jax.numpy as jnp
import numpy as np
from jax import lax
from jax.experimental import pallas as pl
from jax.experimental.pallas import tpu as pltpu

F32 = jnp.float32
BF16 = jnp.bfloat16

D_MODEL = 1024
N_META = 16
D_FF = 2816
EPS = 1e-6
FOX_HEADS = 4
FOX_HD = 64
FOX_W = FOX_HEADS * FOX_HD
SSD_HEADS = 8
SSD_HD = 64
SSD_W = SSD_HEADS * SSD_HD
SSD_GROUPS = 2
SSD_STATE = 64
SSD_CONV = 4
SSD_BC = SSD_GROUPS * SSD_STATE
SSD_CONV_DIM = SSD_W + 2 * SSD_BC
GLA_HEADS = 4
GLA_DK = 32
GLA_DV = 64
GLA_KW = GLA_HEADS * GLA_DK
GLA_W = GLA_HEADS * GLA_DV
GLA_RANK = 16
GLA_TAU = 16.0
D_MIX = FOX_W + SSD_W + GLA_W
IN_SIZES = (FOX_W, FOX_W, FOX_W, FOX_HEADS, SSD_W, SSD_CONV_DIM, SSD_HEADS,
            GLA_KW, GLA_KW, GLA_W, GLA_RANK, GLA_W)
IN_SPLITS = tuple(int(v) for v in np.cumsum(IN_SIZES)[:-1])

LOG2E = 1.4426950408889634
FOX_AUG = FOX_HEADS * 128

LANES = 128
CHUNK = 128
GLA_SUB = 32
FOX_KW = 512
FOX_QW = 512
FFN_CK = 256
FFN_TM = 512
PROJ_GROUP = 4
MIX_GROUP = 4
VMEM_LIMIT = 56 * 1024 * 1024

C_Q, C_K, C_V = 0, 256, 512
C_Z = 768
C_XBC = 1280
C_GQ, C_GK, C_GV, C_GG = 2048, 2176, 2304, 2560
C_SM = 2816
N_PROJ = 2944
SM_F = 0
SM_DT = 4
SM_LR = 12
R_K, R_V, R_Z, R_XBC = 0, 256, 512, 1024
R_GQ, R_GK, R_GV, R_GG, R_SM = 1792, 1920, 2048, 2304, 2560
N_PT = 2688


def _dot(a, b):
    return jnp.dot(a.astype(BF16), b.astype(BF16), preferred_element_type=F32)


def _dot_nt(a, b):
    return lax.dot_general(a.astype(BF16), b.astype(BF16), (((1,), (1,)), ((), ())),
                           preferred_element_type=F32)


def _dot_tn(a, b):
    return lax.dot_general(a.astype(BF16), b.astype(BF16), (((0,), (0,)), ((), ())),
                           preferred_element_type=F32)


def _split(a):
    hi = a.astype(BF16)
    lo = (a - hi.astype(F32)).astype(BF16)
    return hi, lo


def _dot_hl(a, m):
    hi, lo = _split(a)
    return (jnp.dot(hi, m, preferred_element_type=F32) + jnp.dot(lo, m, preferred_element_type=F32))


def _dot_hl_left(m, a):
    hi, lo = _split(a)
    return (jnp.dot(m, hi, preferred_element_type=F32) + jnp.dot(m, lo, preferred_element_type=F32))


def _silu(x):
    return x * jax.nn.sigmoid(x)


def _softplus(x):
    return jnp.maximum(x, 0.0) + jnp.log1p(jnp.exp(-jnp.abs(x)))


def _log_sigmoid(x):
    return jnp.minimum(x, 0.0) - jnp.log1p(jnp.exp(-jnp.abs(x)))


def _rms(x, g):
    return x * lax.rsqrt(jnp.mean(x * x, axis=-1, keepdims=True) + EPS) * g


def _const_spec(shape):
    n = len(shape)
    return pl.BlockSpec(shape, lambda *_: (0,) * n)


def _lspec(shape, l):
    n = len(shape)
    return pl.BlockSpec((None,) + tuple(shape), lambda *_: (l,) + (0,) * n)


def _params(sem):
    return pltpu.CompilerParams(dimension_semantics=sem, vmem_limit_bytes=VMEM_LIMIT)


def _swiglu_half(x, g_ref, win_ref, wout_ref):
    h = _rms(x, g_ref[...]).astype(BF16)
    acc = None
    for c in range(D_FF // FFN_CK):
        g = jnp.dot(h, win_ref[:, c * FFN_CK:(c + 1) * FFN_CK], preferred_element_type=F32)
        u = jnp.dot(h, win_ref[:, D_FF + c * FFN_CK:D_FF + (c + 1) * FFN_CK],
                    preferred_element_type=F32)
        a = (_silu(g) * u).astype(BF16)
        part = jnp.dot(a, wout_ref[c * FFN_CK:(c + 1) * FFN_CK, :], preferred_element_type=F32)
        acc = part if acc is None else acc + part
    return x + 0.5 * acc


def _ffn_kernel(x_ref, g_ref, win_ref, wout_ref, o_ref):
    o_ref[...] = _swiglu_half(x_ref[...], g_ref, win_ref, wout_ref)


def _mix_ffn_kernel(x_ref, fo_ref, so_ref, go_ref, wmo_ref, g_ref, win_ref, wout_ref, o_ref):
    x = x_ref[...]
    x = x + jnp.dot(fo_ref[...], wmo_ref[0:FOX_W, :], preferred_element_type=F32)
    x = x + jnp.dot(so_ref[...], wmo_ref[FOX_W:FOX_W + SSD_W, :], preferred_element_type=F32)
    x = x + jnp.dot(go_ref[...], wmo_ref[FOX_W + SSD_W:D_MIX, :], preferred_element_type=F32)
    o_ref[...] = _swiglu_half(x, g_ref, win_ref, wout_ref)


def _mix_ffn_t_kernel(x_ref, fo_ref, so_ref, go_ref, wmo_ref, g_ref, win_ref, wout_ref, o_ref):
    n_t = fo_ref.shape[0]
    x = x_ref[...]
    fo = fo_ref[...].reshape(n_t * LANES, FOX_W)
    so = jnp.concatenate([so_ref[t].T for t in range(n_t)], axis=0)
    go = jnp.concatenate([go_ref[t].T for t in range(n_t)], axis=0)
    x = x + _dot(fo, wmo_ref[0:FOX_W, :])
    x = x + _dot(so, wmo_ref[FOX_W:FOX_W + SSD_W, :])
    x = x + _dot(go, wmo_ref[FOX_W + SSD_W:D_MIX, :])
    o_ref[...] = _swiglu_half(x, g_ref, win_ref, wout_ref)


def _ffn_weight_specs(l):
    return [_lspec((1, D_MODEL), l), _lspec((D_MODEL, 2 * D_FF), l), _lspec((D_FF, D_MODEL), l)]


def _ffn(x, g, win, wout, tm, l):
    rows = x.shape[0]
    row = lambda w: pl.BlockSpec((tm, w), lambda i: (i, 0))
    return pl.pallas_call(
        _ffn_kernel, out_shape=jax.ShapeDtypeStruct((rows, D_MODEL), F32), grid=(rows // tm,),
        in_specs=[row(D_MODEL)] + _ffn_weight_specs(l), out_specs=row(D_MODEL),
        compiler_params=_params(("parallel",)), name="ffn")(x, g, win, wout)


def _mix_ffn(x, fo, so, go, wmo, g, win, wout, tm, l):
    rows = x.shape[0]
    row = lambda w: pl.BlockSpec((tm, w), lambda i: (i, 0))
    return pl.pallas_call(
        _mix_ffn_kernel, out_shape=jax.ShapeDtypeStruct((rows, D_MODEL), F32), grid=(rows // tm,),
        in_specs=[row(D_MODEL), row(FOX_W), row(SSD_W), row(GLA_W), _lspec((D_MIX, D_MODEL), l)]
        + _ffn_weight_specs(l), out_specs=row(D_MODEL),
        compiler_params=_params(("parallel",)), name="mix_ffn")(x, fo, so, go, wmo, g, win, wout)


def _mix_ffn_t(x, fo, so_t, go_t, wmo, g, win, wout, l):
    rows = x.shape[0]
    n_t = rows // LANES
    row = pl.BlockSpec((rows, D_MODEL), lambda i: (0, 0))
    per_t = lambda w, n: pl.BlockSpec((n_t, w, n), lambda i: (0, 0, 0))
    return pl.pallas_call(
        _mix_ffn_t_kernel, out_shape=jax.ShapeDtypeStruct((rows, D_MODEL), F32), grid=(1,),
        in_specs=[row, per_t(LANES, FOX_W), per_t(SSD_W, LANES), per_t(GLA_W, LANES),
                  _lspec((D_MIX, D_MODEL), l)] + _ffn_weight_specs(l), out_specs=row,
        compiler_params=_params(("parallel",)), name="mix_ffn_t")(x, fo, so_t, go_t, wmo, g, win, wout)


def _head_norm(x, gain_row, g64_ref):
    msq = _dot_hl(x * x, g64_ref[...])
    return x * lax.rsqrt(msq + EPS) * gain_row


def _small_block(sm, fb_ref, dtb_ref):
    logf = _log_sigmoid(sm + fb_ref[...])
    dt = _softplus(sm + dtb_ref[...])
    return logf, dt


def _proj_p_kernel(x_ref, nrm_ref, w_ref, qg_ref, kg_ref, fb_ref, dtb_ref, cw_ref, cb_ref, g64_ref,
                   ltri_ref,
                   qat_ref, ka_ref, kt_ref, vt_ref, vt16_ref, z_ref, xbc_ref, gqk_ref, gv_ref, gg_ref,
                   smr_ref, smt_ref, convp_ref,
                   xb_scr, carry_scr, *, t_real, grp):
    j = pl.program_id(1)

    @pl.when(j == 0)
    def _():
        xb_scr[:, 0:8, :] = jnp.zeros((grp, 8, SSD_CONV_DIM), F32)
        carry_scr[...] = jnp.zeros((grp, 8, LANES), F32)

    h = _rms(x_ref[...].reshape(grp * CHUNK, D_MODEL), nrm_ref[...]).astype(BF16)
    proj = lambda c0, n: jnp.dot(h, w_ref[:, c0:c0 + n], preferred_element_type=F32)
    seq = lambda a, g: a[g * CHUNK:(g + 1) * CHUNK]

    sm_all = proj(C_SM, LANES)
    lane = lax.broadcasted_iota(jnp.int32, (CHUNK, LANES), 1)
    row = lax.broadcasted_iota(jnp.int32, (CHUNK, LANES), 0)
    logf_all, dt_all = _small_block(sm_all, fb_ref, dtb_ref)
    f_rows = []
    for g in range(grp):
        logf = jnp.where(lane < SM_DT, seq(logf_all, g), 0.0)
        dt = jnp.where(j * CHUNK + row < t_real, seq(dt_all, g), 0.0)
        f = _dot_hl_left(ltri_ref[...], logf) + carry_scr[g, 0:1, :]
        carry_scr[g] = jnp.broadcast_to(f[CHUNK - 1:CHUNK, :], (8, LANES))
        smr_ref[g] = jnp.where(lane < SM_LR, dt, seq(sm_all, g))
        smt_ref[g] = jnp.where(lane < SM_DT, logf, dt).T[0:16, :]
        f_rows.append(f)

    q_all = _head_norm(proj(C_Q, FOX_W), qg_ref[...], g64_ref) * (FOX_HD ** -0.5 * LOG2E)
    k_all = _head_norm(proj(C_K, FOX_W), kg_ref[...], g64_ref)
    v_all = proj(C_V, FOX_W)
    l64 = lax.broadcasted_iota(jnp.int32, (CHUNK, FOX_HD), 1)
    for g in range(grp):
        q, k = seq(q_all, g), seq(k_all, g)
        qa, ka = [], []
        for hh in range(FOX_HEADS):
            y = f_rows[g][:, hh:hh + 1] * LOG2E
            hi = y.astype(BF16).astype(F32)
            mid = (y - hi).astype(BF16).astype(F32)
            lo = y - hi - mid
            parts = jnp.where(l64 == 0, hi, jnp.where(l64 == 1, mid, lo))
            ext_q = jnp.where(l64 < 3, parts, jnp.where(l64 < 6, 1.0, 0.0))
            parts = jnp.where(l64 == 3, hi, jnp.where(l64 == 4, mid, lo))
            ext_k = jnp.where(l64 < 3, 1.0, jnp.where(l64 < 6, -parts, 0.0))
            qa += [q[:, hh * FOX_HD:(hh + 1) * FOX_HD], ext_q]
            ka += [k[:, hh * FOX_HD:(hh + 1) * FOX_HD], ext_k]
        qat_ref[g] = jnp.concatenate(qa, axis=1).T.astype(BF16)
        ka_ref[g] = jnp.concatenate(ka, axis=1).astype(BF16)
        kt_ref[g] = k.T
        vt = seq(v_all, g).T
        vt_ref[g] = vt
        vt16_ref[g] = vt.astype(BF16)

    xbc_all = proj(C_XBC, SSD_CONV_DIM)
    r_end = t_real - (t_real - 1) // CHUNK * CHUNK
    for g in range(grp):
        xb_scr[g, 8:8 + CHUNK, :] = seq(xbc_all, g)
        conv = cb_ref[...]
        for w in range(SSD_CONV):
            conv = conv + xb_scr[g, pl.ds(8 - (SSD_CONV - 1) + w, CHUNK), :] * cw_ref[w:w + 1, :]
        xbc_ref[g] = _silu(conv)
        convp_ref[g] = xb_scr[g, r_end:r_end + 8, :]
        xb_scr[g, 0:8, :] = xb_scr[g, CHUNK:CHUNK + 8, :]

    z_ref[...] = proj(C_Z, SSD_W).reshape(grp, CHUNK, SSD_W)
    gqk_ref[...] = proj(C_GQ, 2 * GLA_KW).reshape(grp, CHUNK, 2 * GLA_KW)
    gv_ref[...] = proj(C_GV, GLA_W).reshape(grp, CHUNK, GLA_W)
    gg_ref[...] = proj(C_GG, GLA_W).reshape(grp, CHUNK, GLA_W)


def _proj_p(x3, nrm, w, qg, kg, fb, dtb, cw, cb, g64, ltri, t_real, l):
    nb, t_pad, _ = x3.shape
    nj = t_pad // CHUNK
    grp = math.gcd(nb, PROJ_GROUP)
    rows = lambda w_: pl.BlockSpec((grp, CHUNK, w_), lambda b, j: (b, j, 0))
    cols = lambda h_: pl.BlockSpec((grp, h_, CHUNK), lambda b, j: (b, 0, j))
    out_shape = [
        jax.ShapeDtypeStruct((nb, FOX_AUG, t_pad), BF16),
        jax.ShapeDtypeStruct((nb, t_pad, FOX_AUG), BF16),
        jax.ShapeDtypeStruct((nb, FOX_W, t_pad), F32),
        jax.ShapeDtypeStruct((nb, FOX_W, t_pad), F32),
        jax.ShapeDtypeStruct((nb, FOX_W, t_pad), BF16),
        jax.ShapeDtypeStruct((nb, t_pad, SSD_W), F32),
        jax.ShapeDtypeStruct((nb, t_pad, SSD_CONV_DIM), F32),
        jax.ShapeDtypeStruct((nb, t_pad, 2 * GLA_KW), F32),
        jax.ShapeDtypeStruct((nb, t_pad, GLA_W), F32),
        jax.ShapeDtypeStruct((nb, t_pad, GLA_W), F32),
        jax.ShapeDtypeStruct((nb, t_pad, LANES), F32),
        jax.ShapeDtypeStruct((nb, 16, t_pad), F32),
        jax.ShapeDtypeStruct((nb, 8, SSD_CONV_DIM), F32),
    ]
    out_specs = [cols(FOX_AUG), rows(FOX_AUG), cols(FOX_W), cols(FOX_W), cols(FOX_W), rows(SSD_W),
                 rows(SSD_CONV_DIM), rows(2 * GLA_KW), rows(GLA_W), rows(GLA_W), rows(LANES), cols(16),
                 pl.BlockSpec((grp, 8, SSD_CONV_DIM), lambda b, j: (b, 0, 0))]
    in_specs = [rows(D_MODEL), _lspec((1, D_MODEL), l), _lspec((D_MODEL, N_PROJ), l),
                _lspec((1, FOX_W), l), _lspec((1, FOX_W), l), _lspec((1, LANES), l),
                _lspec((1, LANES), l), _lspec((8, SSD_CONV_DIM), l), _lspec((1, SSD_CONV_DIM), l),
                _const_spec((FOX_W, FOX_W)), _const_spec((CHUNK, CHUNK))]
    return pl.pallas_call(
        functools.partial(_proj_p_kernel, t_real=t_real, grp=grp), out_shape=out_shape,
        grid=(nb // grp, nj), in_specs=in_specs, out_specs=out_specs,
        scratch_shapes=[pltpu.VMEM((grp, 8 + CHUNK, SSD_CONV_DIM), F32),
                        pltpu.VMEM((grp, 8, LANES), F32)],
        compiler_params=_params(("parallel", "arbitrary")), name="proj_p")(
            x3, nrm, w, qg, kg, fb, dtb, cw, cb, g64, ltri)


def _proj_s_kernel(x_ref, nrm_ref, w_ref, qg_ref, kg_ref, fb_ref, dtb_ref, cw_ref, cb_ref, g64_ref,
                   cst_ref,
                   q_ref, kr_ref, vr_ref, pt_ref, cum_ref, convs_ref,
                   hist_scr, carry_scr, *, n_t):
    t = pl.program_id(0)

    @pl.when(t == 0)
    def _():
        hist_scr[0:SSD_CONV - 1] = cst_ref[...]
        carry_scr[...] = jnp.zeros((LANES, LANES), F32)

    h = _rms(x_ref[...], nrm_ref[...]).astype(BF16)
    p = jnp.dot(h, w_ref[...], preferred_element_type=F32)

    q_ref[0] = _head_norm(p[:, C_Q:C_Q + FOX_W], qg_ref[...], g64_ref) * (FOX_HD ** -0.5)
    k = _head_norm(p[:, C_K:C_K + FOX_W], kg_ref[...], g64_ref)
    v = p[:, C_V:C_V + FOX_W]
    kr_ref[0] = k
    vr_ref[0] = v
    for c in range(FOX_W // LANES):
        pt_ref[0, R_K + c * LANES:R_K + (c + 1) * LANES, :] = k[:, c * LANES:(c + 1) * LANES].T
        pt_ref[0, R_V + c * LANES:R_V + (c + 1) * LANES, :] = v[:, c * LANES:(c + 1) * LANES].T

    hist_scr[pl.ds(SSD_CONV - 1 + t, 1)] = p[:, C_XBC:C_XBC + SSD_CONV_DIM][None]
    conv = cb_ref[...]
    for w in range(SSD_CONV):
        conv = conv + hist_scr[t + w] * cw_ref[w:w + 1, :]
    xbc = _silu(conv)

    @pl.when(t == n_t - 1)
    def _():
        convs_ref[...] = hist_scr[n_t:n_t + SSD_CONV - 1]

    sm = p[:, C_SM:C_SM + LANES]
    lane = lax.broadcasted_iota(jnp.int32, (LANES, LANES), 1)
    logf, dt = _small_block(sm, fb_ref, dtb_ref)
    cum = carry_scr[...] + jnp.where(lane < SM_DT, logf, 0.0)
    carry_scr[...] = cum
    cum_ref[0] = cum
    smc = jnp.where(lane < SM_DT, logf, jnp.where(lane < SM_LR, dt, sm))

    def put_t(r0, val):
        for c in range(val.shape[1] // LANES):
            pt_ref[0, r0 + c * LANES:r0 + (c + 1) * LANES, :] = val[:, c * LANES:(c + 1) * LANES].T

    put_t(R_Z, p[:, C_Z:C_Z + SSD_W])
    put_t(R_XBC, xbc)
    put_t(R_GQ, p[:, C_GQ:C_GQ + GLA_KW] * (GLA_DK ** -0.5))
    put_t(R_GK, p[:, C_GK:C_GK + GLA_KW])
    put_t(R_GV, p[:, C_GV:C_GV + GLA_W])
    put_t(R_GG, p[:, C_GG:C_GG + GLA_W])
    put_t(R_SM, smc)


def _proj_s(x, nrm, w, qg, kg, fb, dtb, cw, cb, g64, conv_state, l):
    n_t = x.shape[0] // LANES
    per_t = lambda a, b: pl.BlockSpec((1, a, b), lambda t: (t, 0, 0))
    out_shape = [
        jax.ShapeDtypeStruct((n_t, LANES, FOX_W), F32),
        jax.ShapeDtypeStruct((n_t, LANES, FOX_W), F32),
        jax.ShapeDtypeStruct((n_t, LANES, FOX_W), F32),
        jax.ShapeDtypeStruct((n_t, N_PT, LANES), F32),
        jax.ShapeDtypeStruct((n_t, LANES, LANES), F32),
        jax.ShapeDtypeStruct((SSD_CONV - 1, LANES, SSD_CONV_DIM), F32),
    ]
    out_specs = [per_t(LANES, FOX_W), per_t(LANES, FOX_W), per_t(LANES, FOX_W), per_t(N_PT, LANES),
                 per_t(LANES, LANES), _const_spec((SSD_CONV - 1, LANES, SSD_CONV_DIM))]
    in_specs = [pl.BlockSpec((LANES, D_MODEL), lambda t: (t, 0)), _lspec((1, D_MODEL), l),
                _lspec((D_MODEL, N_PROJ), l), _lspec((1, FOX_W), l), _lspec((1, FOX_W), l),
                _lspec((1, LANES), l), _lspec((1, LANES), l), _lspec((8, SSD_CONV_DIM), l),
                _lspec((1, SSD_CONV_DIM), l), _const_spec((FOX_W, FOX_W)),
                _lspec((SSD_CONV - 1, LANES, SSD_CONV_DIM), l)]
    return pl.pallas_call(
        functools.partial(_proj_s_kernel, n_t=n_t), out_shape=out_shape, grid=(n_t,),
        in_specs=in_specs, out_specs=out_specs,
        scratch_shapes=[pltpu.VMEM((n_t + SSD_CONV - 1, LANES, SSD_CONV_DIM), F32),
                        pltpu.VMEM((LANES, LANES), F32)],
        compiler_params=_params(("arbitrary",)), name="proj_s")(
            x, nrm, w, qg, kg, fb, dtb, cw, cb, g64, conv_state)


def _fox_p_kernel(qat_ref, ka_ref, vt_ref, o_ref, *, kw, qw, tile0):
    qi = pl.program_id(1) + tile0
    nsub = kw // qw
    n_full = qi // nsub

    def scores(start, width):
        return tuple(jnp.dot(ka_ref[0, pl.ds(start, width), h * LANES:(h + 1) * LANES],
                             qat_ref[0, h * LANES:(h + 1) * LANES, :], preferred_element_type=F32)
                     for h in range(FOX_HEADS))

    def update(ss, start, width, carry, masked):
        prs, stats = [], []
        for h in range(FOX_HEADS):
            m, l, _ = carry[h]
            s = ss[h]
            if masked:
                kpos = lax.broadcasted_iota(jnp.int32, (width, qw), 0) + start
                qpos = lax.broadcasted_iota(jnp.int32, (width, qw), 1) + qi * qw
                s = jnp.where(kpos <= qpos, s, -jnp.inf)
            m_new = jnp.maximum(m, jnp.max(s, axis=0, keepdims=True))
            alpha = jnp.exp2(m - m_new)
            pr = jnp.exp2(s - m_new)
            stats.append((m_new, alpha * l + jnp.sum(pr, axis=0, keepdims=True), alpha))
            prs.append(pr.astype(BF16))
        out = []
        for h in range(FOX_HEADS):
            vt = vt_ref[0, h * FOX_HD:(h + 1) * FOX_HD, pl.ds(start, width)]
            m_new, l, alpha = stats[h]
            out.append((m_new, l, alpha * carry[h][2] + jnp.dot(vt, prs[h], preferred_element_type=F32)))
        return tuple(out)

    init = tuple((jnp.full((1, qw), -jnp.inf, F32), jnp.zeros((1, qw), F32),
                  jnp.zeros((FOX_HD, qw), F32)) for _ in range(FOX_HEADS))

    def body(i, carry):
        at = pl.multiple_of(i * kw, kw)
        return update(scores(at, kw), at, kw, carry, False)

    carry = lax.fori_loop(0, n_full, body, init)
    start = pl.multiple_of(n_full * kw, kw)
    for r in range(nsub):
        @pl.when(lax.rem(qi, nsub) == r)
        def _(r=r):
            width = (r + 1) * qw
            fin = update(scores(start, width), start, width, carry, True)
            o_t = jnp.concatenate([acc / l for (_, l, acc) in fin], axis=0)
            o_ref[0] = o_t.T.astype(BF16)


def _fox_p_tiles(qat, ka, vt16, qw, tile0, n_tiles):
    nb, _, t_pad = qat.shape
    return pl.pallas_call(
        functools.partial(_fox_p_kernel, kw=FOX_KW, qw=qw, tile0=tile0),
        out_shape=jax.ShapeDtypeStruct((nb, n_tiles * qw, FOX_W), BF16), grid=(nb, n_tiles),
        in_specs=[pl.BlockSpec((1, FOX_AUG, qw), lambda b, i: (b, 0, i + tile0)),
                  pl.BlockSpec((1, t_pad, FOX_AUG), lambda b, i: (b, 0, 0)),
                  pl.BlockSpec((1, FOX_W, t_pad), lambda b, i: (b, 0, 0))],
        out_specs=pl.BlockSpec((1, qw, FOX_W), lambda b, i: (b, i, 0)),
        compiler_params=_params(("parallel", "arbitrary")), name="fox_p")(qat, ka, vt16)


def _fox_p(qat, ka, vt16):
    t_pad = qat.shape[2]
    n_wide = t_pad // FOX_QW
    rest = (t_pad - n_wide * FOX_QW) // CHUNK
    parts = []
    if n_wide:
        parts.append(_fox_p_tiles(qat, ka, vt16, FOX_QW, 0, n_wide))
    if rest:
        parts.append(_fox_p_tiles(qat, ka, vt16, CHUNK, n_wide * FOX_QW // CHUNK, rest))
    return parts[0] if len(parts) == 1 else jnp.concatenate(parts, axis=1)


def _fox_s_kernel(tbl_ref, q_ref, kr_ref, vr_ref, cum_ref, lf_ref, mstrict_ref, pgsuf_ref,
                  k_hbm, v_hbm, o_ref, kbuf, vbuf, lfst, sem, *, layer, n_pages, n_t):
    b = pl.program_id(0)
    nb = pl.num_programs(0)
    slot = lax.rem(b, 2)

    def copies(seq, sl):
        out = []
        for pg in range(n_pages):
            page = tbl_ref[seq * n_pages + pg]
            out.append(pltpu.make_async_copy(k_hbm.at[layer, page], kbuf.at[sl, pg], sem.at[0, sl]))
            out.append(pltpu.make_async_copy(v_hbm.at[layer, page], vbuf.at[sl, pg], sem.at[1, sl]))
        return out

    @pl.when(b == 0)
    def _():
        lfst[...] = jnp.zeros(lfst.shape, F32)
        for c in copies(0, 0):
            c.start()

    @pl.when(b + 1 < nb)
    def _():
        for c in copies(b + 1, 1 - slot):
            c.start()

    for pg in range(n_pages):
        page = tbl_ref[b * n_pages + pg]
        lfst[pg * 8:pg * 8 + FOX_HEADS, :] = lf_ref[0, page // 2, pl.ds(lax.rem(page, 2) * FOX_HEADS,
                                                                      FOX_HEADS), :]
    lf = lfst[...]
    within = _dot_hl(lf, mstrict_ref[...])
    tot = jnp.broadcast_to(jnp.sum(lf, axis=-1, keepdims=True), lf.shape)
    dsuf = within + _dot_hl_left(pgsuf_ref[...], tot)

    lane = lax.broadcasted_iota(jnp.int32, (8, FOX_W), 1)
    rowh = lax.broadcasted_iota(jnp.int32, (8, FOX_W), 0)
    headmask = (lane // FOX_HD) == rowh
    qexp, cumcol = [], []
    cum_rows = jnp.concatenate([cum_ref[t, pl.ds(b, 1), :] for t in range(n_t)]
                               + [jnp.zeros((LANES - n_t, LANES), F32)], axis=0)
    cum_t = cum_rows.T[0:8, 0:8]
    for t in range(n_t):
        qrow = q_ref[t, pl.ds(b, 1), :]
        qexp.append(jnp.where(headmask, jnp.broadcast_to(qrow, (8, FOX_W)), 0.0))
        cumcol.append(cum_t[:, t:t + 1])
    qexp = jnp.concatenate(qexp, axis=0).astype(BF16)
    cumcol = jnp.concatenate(cumcol, axis=0)
    rows = 8 * n_t

    kn = jnp.concatenate([kr_ref[t, pl.ds(b, 1), :] for t in range(n_t)]
                         + [jnp.zeros((8 - n_t, FOX_W), F32)], axis=0)
    vn = jnp.concatenate([vr_ref[t, pl.ds(b, 1), :] for t in range(n_t)]
                         + [jnp.zeros((8 - n_t, FOX_W), F32)], axis=0)
    s_new = _dot_nt(qexp, kn) + cumcol - jnp.concatenate([cum_t] * n_t, axis=0)
    tq = lax.broadcasted_iota(jnp.int32, (rows, 8), 0) // 8
    tk = lax.broadcasted_iota(jnp.int32, (rows, 8), 1)
    s_new = jnp.where(tk <= tq, s_new, -jnp.inf)

    for c in copies(b, slot):
        c.wait()

    s_pg = []
    for pg in range(n_pages):
        bias = jnp.concatenate([dsuf[pg * 8:(pg + 1) * 8, :]] * n_t, axis=0) + cumcol
        s_pg.append(_dot(qexp, kbuf[slot, pg]) + bias)
    m = s_pg[0]
    for s in s_pg[1:]:
        m = jnp.maximum(m, s)
    m = jnp.maximum(jnp.max(m, axis=-1, keepdims=True), jnp.max(s_new, axis=-1, keepdims=True))
    p_new = jnp.exp(s_new - m)
    l = jnp.sum(p_new, axis=-1, keepdims=True)
    acc = _dot(p_new, vn)
    lsum = None
    for pg in range(n_pages):
        pr = jnp.exp(s_pg[pg] - m)
        lsum = pr if lsum is None else lsum + pr
        acc = acc + _dot_nt(pr, vbuf[slot, pg])
    l = l + jnp.sum(lsum, axis=-1, keepdims=True)
    o = acc / l
    for t in range(n_t):
        ot = jnp.where(headmask, o[t * 8:(t + 1) * 8, :], 0.0)
        o_ref[t, pl.ds(b, 1), :] = jnp.sum(ot, axis=0, keepdims=True)


def _fox_s(tbl, q, kr, vr, cum, lfpool, mstrict, pgsuf, k_cache, v_cache, layer):
    n_t, nb, _ = q.shape
    n_pages = tbl.shape[0] // nb
    whole = lambda a: pl.BlockSpec(a.shape, lambda b, tbl_: (0,) * a.ndim)
    grid_spec = pltpu.PrefetchScalarGridSpec(
        num_scalar_prefetch=1, grid=(nb,),
        in_specs=[whole(q), whole(kr), whole(vr), whole(cum),
                  pl.BlockSpec((1,) + lfpool.shape[1:], lambda b, tbl_: (layer, 0, 0, 0)),
                  whole(mstrict), whole(pgsuf),
                  pl.BlockSpec(memory_space=pl.ANY), pl.BlockSpec(memory_space=pl.ANY)],
        out_specs=pl.BlockSpec((n_t, nb, FOX_W), lambda b, tbl_: (0, 0, 0)),
        scratch_shapes=[pltpu.VMEM((2, n_pages, FOX_W, LANES), F32),
                        pltpu.VMEM((2, n_pages, FOX_W, LANES), F32),
                        pltpu.VMEM((n_pages * 8, LANES), F32),
                        pltpu.SemaphoreType.DMA((2, 2))])
    return pl.pallas_call(
        functools.partial(_fox_s_kernel, layer=layer, n_pages=n_pages, n_t=n_t),
        out_shape=jax.ShapeDtypeStruct((n_t, nb, FOX_W), F32), grid_spec=grid_spec,
        compiler_params=_params(("arbitrary",)), name="fox_s")(
            tbl, q, kr, vr, cum, lfpool, mstrict, pgsuf, k_cache, v_cache)


def _ssd_p_kernel(z_ref, xbc_ref, smr_ref, smt_ref, arow_ref, acol_ref, ltri_ref, utri_ref, exph_ref,
                  dvec_ref, nrm_ref, o_ref, h_ref, ht_scr, *, grp):
    j = pl.program_id(1)
    nj = pl.num_programs(1)

    @pl.when(j == 0)
    def _():
        ht_scr[...] = jnp.zeros(ht_scr.shape, F32)

    finals = [_ssd_p_chunk(s, z_ref, xbc_ref, smr_ref, smt_ref, arow_ref, acol_ref, ltri_ref, utri_ref,
                           exph_ref, dvec_ref, nrm_ref, o_ref, ht_scr) for s in range(grp)]

    @pl.when(j == nj - 1)
    def _():
        rep = SSD_HEADS // SSD_GROUPS
        for s in range(grp):
            h_t = finals[s].T
            for hh in range(SSD_HEADS):
                g = hh // rep
                h_ref[s, hh] = h_t[hh * SSD_HD:(hh + 1) * SSD_HD, g * SSD_STATE:(g + 1) * SSD_STATE]


def _ssd_p_chunk(s, z_ref, xbc_ref, smr_ref, smt_ref, arow_ref, acol_ref, ltri_ref, utri_ref, exph_ref,
                 dvec_ref, nrm_ref, o_ref, ht_scr):
    smr = smr_ref[s]
    exph = exph_ref[...]
    cs_rows = _dot_hl_left(ltri_ref[...], smr * arow_ref[...])
    cs_t = _dot_hl(smt_ref[s] * acol_ref[...], utri_ref[...])
    dt_x = _dot_hl(smr, exph)
    c_hi, c_lo = _split(cs_rows)
    c_lo2 = (cs_rows - c_hi.astype(F32) - c_lo.astype(F32)).astype(BF16)
    cs_x = (jnp.dot(c_hi, exph, preferred_element_type=F32)
            + jnp.dot(c_lo, exph, preferred_element_type=F32)
            + jnp.dot(c_lo2, exph, preferred_element_type=F32))
    end_x = cs_x[CHUNK - 1:CHUNK, :]
    xbc = xbc_ref[s]
    x = xbc[:, 0:SSD_W]
    b_all = xbc[:, SSD_W:SSD_W + SSD_BC].astype(BF16)
    c_all = xbc[:, SSD_W + SSD_BC:SSD_W + 2 * SSD_BC].astype(BF16)
    xdt = x * dt_x
    xdt16 = xdt.astype(BF16)
    ht = ht_scr[s]
    y_state = jnp.dot(c_all, ht.astype(BF16), preferred_element_type=F32)
    upd = _dot_tn(b_all, xdt * jnp.exp(end_x - cs_x))
    cbs = [_dot_nt(c_all[:, g * SSD_STATE:(g + 1) * SSD_STATE], b_all[:, g * SSD_STATE:(g + 1) * SSD_STATE])
           for g in range(SSD_GROUPS)]

    gmask = (lax.broadcasted_iota(jnp.int32, (SSD_BC, SSD_W), 0) // SSD_STATE
             == lax.broadcasted_iota(jnp.int32, (SSD_BC, SSD_W), 1) // (SSD_W // SSD_GROUPS))
    ht_new = ht * jnp.exp(end_x) + jnp.where(gmask, upd, 0.0)
    ht_scr[s] = ht_new

    tril = (lax.broadcasted_iota(jnp.int32, (CHUNK, CHUNK), 1)
            <= lax.broadcasted_iota(jnp.int32, (CHUNK, CHUNK), 0))
    low = lax.broadcasted_iota(jnp.int32, (CHUNK, LANES), 1) < SSD_HD
    rep = SSD_HEADS // SSD_GROUPS
    pairs = []
    for pr in range(SSD_HEADS // 2):
        xp = xdt16[:, pr * LANES:(pr + 1) * LANES]
        ys = []
        for hh in (2 * pr, 2 * pr + 1):
            col = SM_DT + hh
            lm = jnp.exp(jnp.where(tril, cs_rows[:, col:col + 1] - cs_t[col:col + 1, :], -jnp.inf))
            ys.append(jnp.dot((cbs[hh // rep] * lm).astype(BF16), xp, preferred_element_type=F32))
        pairs.append(jnp.where(low, ys[0], ys[1]))
    y = jnp.concatenate(pairs, axis=1) + y_state * jnp.exp(cs_x)
    y = (y + x * dvec_ref[...]) * _silu(z_ref[s])
    gw = SSD_W // SSD_GROUPS
    for g in range(SSD_GROUPS):
        yg = y[:, g * gw:(g + 1) * gw]
        yn = yg * lax.rsqrt(jnp.mean(yg * yg, axis=-1, keepdims=True) + EPS)
        o_ref[s, :, g * gw:(g + 1) * gw] = (yn * nrm_ref[:, g * gw:(g + 1) * gw]).astype(BF16)
    return ht_new


def _ssd_p(z, xbc, smr, smt, arow, acol, ltri, utri, exph, dvec, nrm, l):
    nb, t_pad, _ = z.shape
    nj = t_pad // CHUNK
    grp = math.gcd(nb, MIX_GROUP)
    rows = lambda w_: pl.BlockSpec((grp, CHUNK, w_), lambda b, j: (b, j, 0))
    return pl.pallas_call(
        functools.partial(_ssd_p_kernel, grp=grp),
        out_shape=[jax.ShapeDtypeStruct((nb, t_pad, SSD_W), BF16),
                   jax.ShapeDtypeStruct((nb, SSD_HEADS, SSD_HD, SSD_STATE), F32)],
        grid=(nb // grp, nj),
        in_specs=[rows(SSD_W), rows(SSD_CONV_DIM), rows(LANES),
                  pl.BlockSpec((grp, 16, CHUNK), lambda b, j: (b, 0, j)),
                  _lspec((1, LANES), l), _lspec((16, 1), l), _const_spec((CHUNK, CHUNK)),
                  _const_spec((CHUNK, CHUNK)), _const_spec((LANES, SSD_W)),
                  _lspec((1, SSD_W), l), _lspec((1, SSD_W), l)],
        out_specs=[rows(SSD_W),
                   pl.BlockSpec((grp, SSD_HEADS, SSD_HD, SSD_STATE), lambda b, j: (b, 0, 0, 0))],
        scratch_shapes=[pltpu.VMEM((grp, SSD_BC, SSD_W), F32)],
        compiler_params=_params(("parallel", "arbitrary")), name="ssd_p")(
            z, xbc, smr, smt, arow, acol, ltri, utri, exph, dvec, nrm)


def _ssd_s_kernel(x_ref, b_ref, c_ref, sm_ref, xg_ref, zg_ref, a_ref, d_ref, nrm_ref, st_ref,
                  o_ref, so_ref, ybuf, *, n_t):
    hh = pl.program_id(0)
    rep = SSD_HEADS // SSD_GROUPS
    gw = SSD_W // SSD_GROUPS
    dt = [sm_ref[t, pl.ds(SM_DT + hh, 1), :] for t in range(n_t)]
    dec = [jnp.exp(dt[t] * a_ref[pl.ds(hh, 1), :]) for t in range(n_t)]

    def body(p, carry):
        hp = st_ref[0, 0, p]
        for t in range(n_t):
            hp = hp * dec[t] + b_ref[t] * (x_ref[t, pl.ds(p, 1), :] * dt[t])
            ybuf[t, pl.ds(hh * SSD_HD + p, 1), :] = jnp.sum(c_ref[t] * hp, axis=0, keepdims=True)
        so_ref[0, 0, p] = hp
        return carry

    lax.fori_loop(0, SSD_HD, body, 0)

    @pl.when(lax.rem(hh, rep) == rep - 1)
    def _():
        r0 = pl.multiple_of((hh // rep) * gw, gw)
        for t in range(n_t):
            y = ybuf[t, pl.ds(r0, gw), :] + xg_ref[t] * d_ref[pl.ds(r0, gw), :]
            y = y * _silu(zg_ref[t])
            yn = y * lax.rsqrt(jnp.mean(y * y, axis=0, keepdims=True) + EPS)
            o_ref[t, pl.ds(r0, gw), :] = yn * nrm_ref[pl.ds(r0, gw), :]


def _ssd_s(pt, a_b, d_b, nrm_b, state, layer):
    n_t = pt.shape[0]
    rep = SSD_HEADS // SSD_GROUPS
    gw = SSD_W // SSD_GROUPS
    blk = lambda h_, f: pl.BlockSpec((n_t, h_, LANES), lambda hh: (0, f(hh), 0))
    st_spec = pl.BlockSpec((1, 1, SSD_HD, SSD_STATE, LANES), lambda hh: (layer, hh, 0, 0, 0))
    return pl.pallas_call(
        functools.partial(_ssd_s_kernel, n_t=n_t),
        out_shape=[jax.ShapeDtypeStruct((n_t, SSD_W, LANES), F32),
                   jax.ShapeDtypeStruct((1, SSD_HEADS, SSD_HD, SSD_STATE, LANES), F32)],
        grid=(SSD_HEADS,),
        in_specs=[blk(SSD_HD, lambda hh: R_XBC // SSD_HD + hh),
                  blk(SSD_STATE, lambda hh: (R_XBC + SSD_W) // SSD_STATE + hh // rep),
                  blk(SSD_STATE, lambda hh: (R_XBC + SSD_W + SSD_BC) // SSD_STATE + hh // rep),
                  blk(LANES, lambda hh: R_SM // LANES),
                  blk(gw, lambda hh: R_XBC // gw + hh // rep),
                  blk(gw, lambda hh: R_Z // gw + hh // rep),
                  _lspec((SSD_HEADS, LANES), layer), _lspec((SSD_W, LANES), layer),
                  _lspec((SSD_W, LANES), layer), st_spec],
        out_specs=[_const_spec((n_t, SSD_W, LANES)),
                   pl.BlockSpec((1, 1, SSD_HD, SSD_STATE, LANES), lambda hh: (0, hh, 0, 0, 0))],
        scratch_shapes=[pltpu.VMEM((n_t, SSD_W, LANES), F32)],
        compiler_params=_params(("arbitrary",)), name="ssd_s")(
            pt, pt, pt, pt, pt, pt, a_b, d_b, nrm_b, state)


def _gla_p_kernel(gqk_ref, gv_ref, gg_ref, smr_ref, wg_ref, gb_ref, bl_ref, be_ref, g64_ref, gn_ref,
                  o_ref, s_ref, st_scr, *, t_real, grp):
    j = pl.program_id(1)
    nj = pl.num_programs(1)

    @pl.when(j == 0)
    def _():
        st_scr[...] = jnp.zeros(st_scr.shape, F32)

    finals = [_gla_p_chunk(s, j, gqk_ref, gv_ref, gg_ref, smr_ref, wg_ref, gb_ref, bl_ref, be_ref,
                           g64_ref, gn_ref, o_ref, st_scr, t_real) for s in range(grp)]

    @pl.when(j == nj - 1)
    def _():
        for s in range(grp):
            s_kv = finals[s].T
            for h in range(GLA_HEADS):
                s_ref[s, h] = s_kv[h * GLA_DK:(h + 1) * GLA_DK, h * GLA_DV:(h + 1) * GLA_DV]


def _gla_p_chunk(s, j, gqk_ref, gv_ref, gg_ref, smr_ref, wg_ref, gb_ref, bl_ref, be_ref, g64_ref,
                 gn_ref, o_ref, st_scr, t_real):
    sub = GLA_SUB
    n_sub = CHUNK // sub
    row = lax.broadcasted_iota(jnp.int32, (CHUNK, GLA_KW), 0)
    valid = j * CHUNK + row < t_real
    glog = _log_sigmoid(_dot(smr_ref[s], wg_ref[...]) + gb_ref[...]) * (1.0 / GLA_TAU)
    glog = jnp.where(valid, glog, 0.0)
    gqk = gqk_ref[s]
    gq = gqk[:, 0:GLA_KW] * (GLA_DK ** -0.5)
    gk = jnp.where(valid, gqk[:, GLA_KW:2 * GLA_KW], 0.0)
    v = gv_ref[s]
    bcl = _dot_hl_left(bl_ref[...], glog)
    tot = _dot_hl_left(be_ref[...], glog)
    qe = (gq * jnp.exp(bcl)).astype(BF16)
    kp = gk * jnp.exp(-bcl)
    kend = (gk * jnp.exp(tot - bcl)).astype(BF16)
    dec = jnp.exp(tot)

    r_k = lax.broadcasted_iota(jnp.int32, (GLA_HEADS * sub, GLA_KW), 0) // sub
    c_k = lax.broadcasted_iota(jnp.int32, (GLA_HEADS * sub, GLA_KW), 1) // GLA_DK
    r_v = lax.broadcasted_iota(jnp.int32, (GLA_HEADS * sub, GLA_W), 0) // sub
    c_v = lax.broadcasted_iota(jnp.int32, (GLA_HEADS * sub, GLA_W), 1) // GLA_DV
    r_s = lax.broadcasted_iota(jnp.int32, (GLA_W, GLA_KW), 0) // GLA_DV
    c_s = lax.broadcasted_iota(jnp.int32, (GLA_W, GLA_KW), 1) // GLA_DK
    causal = (lax.broadcasted_iota(jnp.int32, (sub, GLA_HEADS * sub), 1) % sub
              <= lax.broadcasted_iota(jnp.int32, (sub, GLA_HEADS * sub), 0))
    sls = [slice(i * sub, (i + 1) * sub) for i in range(n_sub)]
    atts, upds = [], []
    for sl in sls:
        kbd = jnp.where(r_k == c_k, jnp.concatenate([kp[sl]] * GLA_HEADS, axis=0), 0.0)
        atts.append(_dot_nt(qe[sl], kbd))
        upds.append(_dot_tn(v[sl], kend[sl]))
    sts = [st_scr[s]]
    for i in range(n_sub):
        sts.append(sts[i] * dec[i * sub:i * sub + 1, :] + jnp.where(r_s == c_s, upds[i], 0.0))
    st_scr[s] = sts[n_sub]
    outs = []
    for i, sl in enumerate(sls):
        vbd = jnp.where(r_v == c_v, jnp.concatenate([v[sl]] * GLA_HEADS, axis=0), 0.0)
        outs.append(_dot(jnp.where(causal, atts[i], 0.0), vbd) + _dot_nt(qe[sl], sts[i]))
    o = jnp.concatenate(outs, axis=0)
    msq = _dot_hl(o * o, g64_ref[...])
    o_ref[s] = (o * lax.rsqrt(msq + EPS) * gn_ref[...] * _silu(gg_ref[s])).astype(BF16)
    return sts[n_sub]


def _gla_p(gqk, gv, gg, smr, wgp, gb, bl, be, g64, gn, t_real, l):
    nb, t_pad, _ = gv.shape
    nj = t_pad // CHUNK
    grp = math.gcd(nb, MIX_GROUP)
    rows = lambda w_: pl.BlockSpec((grp, CHUNK, w_), lambda b, j: (b, j, 0))
    return pl.pallas_call(
        functools.partial(_gla_p_kernel, t_real=t_real, grp=grp),
        out_shape=[jax.ShapeDtypeStruct((nb, t_pad, GLA_W), BF16),
                   jax.ShapeDtypeStruct((nb, GLA_HEADS, GLA_DK, GLA_DV), F32)],
        grid=(nb // grp, nj),
        in_specs=[rows(2 * GLA_KW), rows(GLA_W), rows(GLA_W), rows(LANES),
                  _lspec((LANES, GLA_KW), l), _lspec((1, GLA_KW), l),
                  _const_spec((CHUNK, CHUNK)), _const_spec((CHUNK, CHUNK)),
                  _const_spec((GLA_W, GLA_W)), _lspec((1, GLA_W), l)],
        out_specs=[rows(GLA_W),
                   pl.BlockSpec((grp, GLA_HEADS, GLA_DK, GLA_DV), lambda b, j: (b, 0, 0, 0))],
        scratch_shapes=[pltpu.VMEM((grp, GLA_W, GLA_KW), F32)],
        compiler_params=_params(("parallel", "arbitrary")), name="gla_p")(
            gqk, gv, gg, smr, wgp, gb, bl, be, g64, gn)


def _gla_s_kernel(q_ref, k_ref, v_ref, gg_ref, sm_ref, wgt_ref, gb_ref, gn_ref, st_ref,
                  o_ref, so_ref, eg_scr, acc_scr, *, n_t):
    for t in range(n_t):
        glog = _log_sigmoid(_dot(wgt_ref[...], sm_ref[t]) + gb_ref[...]) * (1.0 / GLA_TAU)
        eg_scr[t] = jnp.exp(glog)
        acc_scr[t] = jnp.zeros((GLA_DV, LANES), F32)

    def body(kk, carry):
        s = st_ref[0, 0, kk]
        for t in range(n_t):
            s = s * eg_scr[t, pl.ds(kk, 1), :] + k_ref[t, pl.ds(kk, 1), :] * v_ref[t]
            acc_scr[t] = acc_scr[t] + q_ref[t, pl.ds(kk, 1), :] * s
        so_ref[0, 0, kk] = s
        return carry

    lax.fori_loop(0, GLA_DK, body, 0)
    for t in range(n_t):
        o = acc_scr[t]
        on = o * lax.rsqrt(jnp.mean(o * o, axis=0, keepdims=True) + EPS)
        o_ref[t] = on * gn_ref[...] * _silu(gg_ref[t])


def _gla_s(pt, wgt, gb_b, gn_b, state, layer):
    n_t = pt.shape[0]
    blk = lambda h_, f: pl.BlockSpec((n_t, h_, LANES), lambda hh: (0, f(hh), 0))
    return pl.pallas_call(
        functools.partial(_gla_s_kernel, n_t=n_t),
        out_shape=[jax.ShapeDtypeStruct((n_t, GLA_W, LANES), F32),
                   jax.ShapeDtypeStruct((1, GLA_HEADS, GLA_DK, GLA_DV, LANES), F32)],
        grid=(GLA_HEADS,),
        in_specs=[blk(GLA_DK, lambda hh: R_GQ // GLA_DK + hh),
                  blk(GLA_DK, lambda hh: R_GK // GLA_DK + hh),
                  blk(GLA_DV, lambda hh: R_GV // GLA_DV + hh),
                  blk(GLA_DV, lambda hh: R_GG // GLA_DV + hh),
                  blk(LANES, lambda hh: R_SM // LANES),
                  pl.BlockSpec((None, GLA_DK, LANES), lambda hh: (layer, hh, 0)),
                  pl.BlockSpec((None, GLA_DK, LANES), lambda hh: (layer, hh, 0)),
                  _lspec((GLA_DV, LANES), layer),
                  pl.BlockSpec((1, 1, GLA_DK, GLA_DV, LANES), lambda hh: (layer, hh, 0, 0, 0))],
        out_specs=[blk(GLA_DV, lambda hh: hh),
                   pl.BlockSpec((1, 1, GLA_DK, GLA_DV, LANES), lambda hh: (0, hh, 0, 0, 0))],
        scratch_shapes=[pltpu.VMEM((n_t, GLA_DK, LANES), F32), pltpu.VMEM((n_t, GLA_DV, LANES), F32)],
        compiler_params=_params(("arbitrary",)), name="gla_s")(
            pt, pt, pt, pt, pt, wgt, gb_b, gn_b, state)


def _tri_consts(n_pages):
    i = np.arange(CHUNK)
    ltri = (i[None, :] <= i[:, None]).astype(np.float32)
    utri = ltri.T
    mstrict = (i[:, None] > i[None, :]).astype(np.float32)
    same = (i[:, None] // GLA_SUB) == (i[None, :] // GLA_SUB)
    bl = (same & (i[None, :] <= i[:, None])).astype(np.float32)
    be = same.astype(np.float32)
    r = np.arange(n_pages * 8)
    pgsuf = ((r[:, None] % 8 == r[None, :] % 8) & (r[None, :] // 8 > r[:, None] // 8)).astype(np.float32)
    h = np.arange(FOX_W)
    g64 = ((h[:, None] // 64) == (h[None, :] // 64)).astype(np.float32) / 64.0
    cw = np.arange(SSD_W)
    exph = (i[:, None] == SM_DT + cw[None, :] // SSD_HD).astype(np.float32)
    c = lambda a: jnp.asarray(a, BF16)
    return dict(ltri=c(ltri), utri=c(utri), mstrict=c(mstrict), bl=c(bl), be=c(be), pgsuf=c(pgsuf),
                g64=c(g64), exph=c(exph))


def kernel(x_prompt, x_sample, cache_fox_k, cache_fox_v, cache_fox_logf, state_ssm, state_conv,
           state_gla, page_table, meta_tokens, ffn1_norm, ffn1_w_in, ffn1_w_out, mix_norm, w_mix_in,
           fox_q_norm, fox_k_norm, fox_f_bias, ssd_conv_w, ssd_conv_b, ssd_dt_bias, ssd_a_log, ssd_d,
           ssd_norm, gla_w_gate, gla_gate_bias, gla_norm, w_mix_out, ffn2_norm, ffn2_w_in, ffn2_w_out):
    nbp, seq, _ = x_prompt.shape
    nbs, n_t, _ = x_sample.shape
    depth = ffn1_norm.shape[0]
    assert nbs == LANES and seq % CHUNK == 0
    t_real = N_META + seq
    t_pad = -(-t_real // CHUNK) * CHUNK
    n_pool, page_size = cache_fox_k.shape[1], cache_fox_k.shape[2]
    assert page_size == LANES and n_pool % 2 == 0
    n_pages = page_table.shape[1]
    consts = _tri_consts(n_pages)

    meta = jnp.broadcast_to(meta_tokens[None], (nbp, N_META, D_MODEL))
    xp = jnp.concatenate([meta, x_prompt, jnp.zeros((nbp, t_pad - t_real, D_MODEL), F32)], axis=1)
    xp = xp.reshape(nbp * t_pad, D_MODEL)
    xs = jnp.transpose(x_sample, (1, 0, 2)).reshape(n_t * nbs, D_MODEL)

    k_cache = jnp.transpose(cache_fox_k, (0, 1, 3, 4, 2)).reshape(depth, n_pool, FOX_W, page_size)
    v_cache = jnp.transpose(cache_fox_v, (0, 1, 3, 4, 2)).reshape(depth, n_pool, FOX_W, page_size)
    lfpool = jnp.transpose(cache_fox_logf, (0, 1, 3, 2)).reshape(depth, n_pool // 2, 8, page_size)
    ssm_in = jnp.transpose(state_ssm, (0, 2, 3, 4, 1))
    gla_in = jnp.transpose(state_gla, (0, 2, 3, 4, 1))
    conv_in = jnp.transpose(state_conv, (0, 2, 1, 3))
    tbl = page_table.reshape(-1).astype(jnp.int32)

    w1i, w1o = ffn1_w_in.astype(BF16), ffn1_w_out.astype(BF16)
    w2i, w2o = ffn2_w_in.astype(BF16), ffn2_w_out.astype(BF16)
    wmo = w_mix_out.astype(BF16)
    col = [0] + list(IN_SPLITS) + [w_mix_in.shape[-1]]
    part = lambda i: w_mix_in[:, :, col[i]:col[i + 1]]
    fq, fk, fv, ff, sz, sxbc, sdt, gq, gk, gv, glr, gg = [part(i) for i in range(len(IN_SIZES))]
    n_small = FOX_HEADS + SSD_HEADS + GLA_RANK
    wproj = jnp.concatenate([fq, fk, fv, sz, sxbc, gq, gk, gv, gg, ff, sdt, glr,
                             jnp.zeros((depth, D_MODEL, LANES - n_small), F32)], axis=-1).astype(BF16)
    n1, n2, nm = ffn1_norm[:, None], ffn2_norm[:, None], mix_norm[:, None]
    qg = jnp.tile(fox_q_norm, (1, FOX_HEADS))[:, None]
    kg = jnp.tile(fox_k_norm, (1, FOX_HEADS))[:, None]
    a = -jnp.exp(ssd_a_log)
    lanes_pad = lambda v, off: jnp.pad(v, ((0, 0), (off, LANES - off - v.shape[1])))[:, None]
    fb, dtb, arow = lanes_pad(fox_f_bias, SM_F), lanes_pad(ssd_dt_bias, SM_DT), lanes_pad(a, SM_DT)
    acol = jnp.pad(a, ((0, 0), (SM_DT, 16 - SM_DT - SSD_HEADS)))[:, :, None]
    cw = jnp.pad(ssd_conv_w, ((0, 0), (0, 8 - SSD_CONV), (0, 0)))
    cb = ssd_conv_b[:, None]
    d_rep = jnp.repeat(ssd_d, SSD_HD, axis=1)
    dvec, snrm = d_rep[:, None], ssd_norm[:, None]
    wgp = jnp.pad(gla_w_gate, ((0, 0), (SM_LR, LANES - SM_LR - GLA_RANK), (0, 0))).astype(BF16)
    gb = gla_gate_bias[:, None]
    gn = jnp.tile(gla_norm, (1, GLA_HEADS))[:, None]
    on_lanes = lambda v: jnp.broadcast_to(v[:, :, None], v.shape + (LANES,))
    a_b, d_b, snrm_b = on_lanes(a), on_lanes(d_rep), on_lanes(ssd_norm)
    wgt = jnp.pad(jnp.transpose(gla_w_gate, (0, 2, 1)),
                  ((0, 0), (0, 0), (SM_LR, LANES - SM_LR - GLA_RANK))).astype(BF16)
    gb_b, gn_b = on_lanes(gla_gate_bias), on_lanes(gla_norm)

    outs_p = [[] for _ in range(6)]
    outs_s = [[] for _ in range(6)]
    for l in range(depth):
        xp = _ffn(xp, n1, w1i, w1o, FFN_TM, l)
        (qat, kaug, kt, vt, vt16, z, xbc, gqk, gvv, ggg, smr, smt, convp) = _proj_p(
            xp.reshape(nbp, t_pad, D_MODEL), nm, wproj, qg, kg, fb, dtb, cw, cb, consts["g64"],
            consts["ltri"], t_real, l)
        fox_o = _fox_p(qat, kaug, vt16)
        ssd_o, ssm_p = _ssd_p(z, xbc, smr, smt, arow, acol, consts["ltri"], consts["utri"],
                              consts["exph"], dvec, snrm, l)
        gla_o, gla_p = _gla_p(gqk, gvv, ggg, smr, wgp, gb, consts["bl"], consts["be"], consts["g64"],
                              gn, t_real, l)
        xp = _mix_ffn(xp, fox_o.reshape(-1, FOX_W), ssd_o.reshape(-1, SSD_W), gla_o.reshape(-1, GLA_W),
                      wmo, n2, w2i, w2o, FFN_TM, l)
        outs_p[0].append(kt.reshape(nbp, FOX_HEADS, FOX_HD, t_pad))
        outs_p[1].append(vt.reshape(nbp, FOX_HEADS, FOX_HD, t_pad))
        outs_p[2].append(jnp.transpose(smt[:, 0:FOX_HEADS, :t_real], (0, 2, 1)))
        outs_p[3].append(ssm_p)
        outs_p[4].append(convp[:, 8 - (SSD_CONV - 1):])
        outs_p[5].append(gla_p)

        xs = _ffn(xs, n1, w1i, w1o, xs.shape[0], l)
        qs, krs, vrs, pt, cum, convs = _proj_s(xs, nm, wproj, qg, kg, fb, dtb, cw, cb, consts["g64"],
                                               conv_in, l)
        fox_os = _fox_s(tbl, qs, krs, vrs, cum, lfpool, consts["mstrict"], consts["pgsuf"],
                        k_cache, v_cache, l)
        ssd_ot, ssm_s = _ssd_s(pt, a_b, d_b, snrm_b, ssm_in, l)
        gla_ot, gla_s = _gla_s(pt, wgt, gb_b, gn_b, gla_in, l)
        xs = _mix_ffn_t(xs, fox_os, ssd_ot, gla_ot, wmo, n2, w2i, w2o, l)
        kts = pt[:, R_K:R_K + FOX_W].reshape(n_t, FOX_HEADS, FOX_HD, nbs)
        vts = pt[:, R_V:R_V + FOX_W].reshape(n_t, FOX_HEADS, FOX_HD, nbs)
        outs_s[0].append(jnp.transpose(kts, (3, 0, 1, 2)))
        outs_s[1].append(jnp.transpose(vts, (3, 0, 1, 2)))
        outs_s[2].append(jnp.transpose(pt[:, R_SM + SM_F:R_SM + SM_F + FOX_HEADS], (2, 0, 1)))
        outs_s[3].append(jnp.transpose(ssm_s[0], (3, 0, 1, 2)))
        outs_s[4].append(jnp.transpose(convs, (1, 0, 2)))
        outs_s[5].append(jnp.transpose(gla_s[0], (3, 0, 1, 2)))

    y_prompt = xp.reshape(nbp, t_pad, D_MODEL)[:, N_META:t_real]
    y_sample = jnp.transpose(xs.reshape(n_t, nbs, D_MODEL), (1, 0, 2))
    k_p = jnp.transpose(jnp.stack(outs_p[0])[..., :t_real], (0, 1, 4, 2, 3))
    v_p = jnp.transpose(jnp.stack(outs_p[1])[..., :t_real], (0, 1, 4, 2, 3))
    lf_p, ssm_p, conv_p, gla_p = [jnp.stack(a) for a in outs_p[2:]]
    k_s, v_s, lf_s, ssm_s, conv_s, gla_s = [jnp.stack(a) for a in outs_s]
    return (y_prompt, y_sample, k_p, v_p, lf_p, ssm_p, conv_p, gla_p, k_s, v_s, lf_s, ssm_s, conv_s, gla_s)
```

```python
import functools
import math

import jax
import jax.numpy as jnp
import numpy as np
from jax import lax
from jax.experimental import pallas as pl
from jax.experimental.pallas import tpu as pltpu

F32 = jnp.float32
BF16 = jnp.bfloat16

D_MODEL = 1024
N_META = 16
D_FF = 2816
EPS = 1e-6
FOX_HEADS = 4
FOX_HD = 64
FOX_W = FOX_HEADS * FOX_HD
SSD_HEADS = 8
SSD_HD = 64
SSD_W = SSD_HEADS * SSD_HD
SSD_GROUPS = 2
SSD_STATE = 64
SSD_CONV = 4
SSD_BC = SSD_GROUPS * SSD_STATE
SSD_CONV_DIM = SSD_W + 2 * SSD_BC
GLA_HEADS = 4
GLA_DK = 32
GLA_DV = 64
GLA_KW = GLA_HEADS * GLA_DK
GLA_W = GLA_HEADS * GLA_DV
GLA_RANK = 16
GLA_TAU = 16.0
D_MIX = FOX_W + SSD_W + GLA_W
IN_SIZES = (FOX_W, FOX_W, FOX_W, FOX_HEADS, SSD_W, SSD_CONV_DIM, SSD_HEADS,
            GLA_KW, GLA_KW, GLA_W, GLA_RANK, GLA_W)
IN_SPLITS = tuple(int(v) for v in np.cumsum(IN_SIZES)[:-1])

LOG2E = 1.4426950408889634
FOX_AUG = FOX_HEADS * 128

LANES = 128
CHUNK = 128
GLA_SUB = 32
FOX_KW = 512
FOX_QW = 512
FFN_CK = 256
FFN_TM = 512
PROJ_GROUP = 4
MIX_GROUP = 8
VMEM_LIMIT = 60 * 1024 * 1024

C_Q, C_K, C_V = 0, 256, 512
C_Z = 768
C_XBC = 1280
C_GQ, C_GK, C_GV, C_GG = 2048, 2176, 2304, 2560
C_SM = 2816
N_PROJ = 2944
SM_F = 0
SM_DT = 4
SM_LR = 12
R_K, R_V, R_Z, R_XBC = 0, 256, 512, 1024
R_GQ, R_GK, R_GV, R_GG, R_SM = 1792, 1920, 2048, 2304, 2560
N_PT = 2688


def _dot(a, b):
    return jnp.dot(a.astype(BF16), b.astype(BF16), preferred_element_type=F32)


def _dot_nt(a, b):
    return lax.dot_general(a.astype(BF16), b.astype(BF16), (((1,), (1,)), ((), ())),
                           preferred_element_type=F32)


def _dot_tn(a, b):
    return lax.dot_general(a.astype(BF16), b.astype(BF16), (((0,), (0,)), ((), ())),
                           preferred_element_type=F32)


def _split(a):
    hi = a.astype(BF16)
    lo = (a - hi.astype(F32)).astype(BF16)
    return hi, lo


def _dot_hl(a, m):
    hi, lo = _split(a)
    return (jnp.dot(hi, m, preferred_element_type=F32) + jnp.dot(lo, m, preferred_element_type=F32))


def _dot_hl_left(m, a):
    hi, lo = _split(a)
    return (jnp.dot(m, hi, preferred_element_type=F32) + jnp.dot(m, lo, preferred_element_type=F32))


def _silu(x):
    return x * jax.nn.sigmoid(x)


def _softplus(x):
    return jnp.maximum(x, 0.0) + jnp.log1p(jnp.exp(-jnp.abs(x)))


def _log_sigmoid(x):
    return jnp.minimum(x, 0.0) - jnp.log1p(jnp.exp(-jnp.abs(x)))


def _rms(x, g):
    return x * lax.rsqrt(jnp.mean(x * x, axis=-1, keepdims=True) + EPS) * g


def _const_spec(shape):
    n = len(shape)
    return pl.BlockSpec(shape, lambda *_: (0,) * n)


def _lspec(shape, l):
    n = len(shape)
    return pl.BlockSpec((None,) + tuple(shape), lambda *_: (l,) + (0,) * n)


def _wspec(shape, l):
    n = len(shape)
    return pl.BlockSpec((None,) + tuple(shape), lambda *_: (l,) + (0,) * n, pipeline_mode=pl.Buffered(1))


def _params(sem):
    return pltpu.CompilerParams(dimension_semantics=sem, vmem_limit_bytes=VMEM_LIMIT)


def _swiglu_half(x, g_ref, win_ref, wout_ref):
    h = _rms(x, g_ref[...]).astype(BF16)
    acc = None
    for c in range(D_FF // FFN_CK):
        g = jnp.dot(h, win_ref[:, c * FFN_CK:(c + 1) * FFN_CK], preferred_element_type=F32)
        u = jnp.dot(h, win_ref[:, D_FF + c * FFN_CK:D_FF + (c + 1) * FFN_CK],
                    preferred_element_type=F32)
        a = (_silu(g) * u).astype(BF16)
        part = jnp.dot(a, wout_ref[c * FFN_CK:(c + 1) * FFN_CK, :], preferred_element_type=F32)
        acc = part if acc is None else acc + part
    return x + 0.5 * acc


def _ffn_kernel(*refs, mix, lead):
    rows = (lambda r: r[0]) if lead else (lambda r: r[...])
    x = rows(refs[0])
    if mix:
        fo_ref, so_ref, go_ref, wmo_ref = refs[1:5]
        if mix == "rows":
            fo, so, go = rows(fo_ref), rows(so_ref), rows(go_ref)
        else:
            n_t = fo_ref.shape[0]
            fo = fo_ref[...].reshape(n_t * LANES, FOX_W)
            so = jnp.concatenate([so_ref[t].T for t in range(n_t)], axis=0)
            go = jnp.concatenate([go_ref[t].T for t in range(n_t)], axis=0)
        x = x + _dot(fo, wmo_ref[0:FOX_W, :])
        x = x + _dot(so, wmo_ref[FOX_W:FOX_W + SSD_W, :])
        x = x + _dot(go, wmo_ref[FOX_W + SSD_W:D_MIX, :])
    g_ref, win_ref, wout_ref, o_ref = refs[-4:]
    o_ref[...] = _swiglu_half(x, g_ref, win_ref, wout_ref)


def _ffn(x, tm, ffn, mix=None, window=None):
    if window:
        nseq, t_in, row0, t_out = window
        grid = (nseq, t_out // tm)
        view = lambda a: a.reshape(nseq, t_in, a.shape[-1])
        row = lambda w: pl.BlockSpec((pl.Element(1), pl.Element(tm), pl.Element(w)),
                                     lambda b, i: (b, pl.multiple_of(row0 + i * tm, math.gcd(row0, tm)), 0))
        out_shape = jax.ShapeDtypeStruct((nseq, t_out, D_MODEL), F32)
        out_spec = pl.BlockSpec((None, tm, D_MODEL), lambda b, i: (b, i, 0))
    else:
        rows = x.shape[0]
        grid = (rows // tm,)
        view = lambda a: a
        row = lambda w: pl.BlockSpec((tm, w), lambda i: (i, 0))
        out_shape = jax.ShapeDtypeStruct((rows, D_MODEL), F32)
        out_spec = row(D_MODEL)
    args, specs = [view(x)], [row(D_MODEL)]
    if mix:
        kind, fo, so, go, wmo, l = mix
        if kind == "rows":
            args += [view(fo), view(so), view(go)]
            specs += [row(FOX_W), row(SSD_W), row(GLA_W)]
        else:
            assert not window and tm == x.shape[0]
            args += [fo, so, go]
            specs += [pl.BlockSpec(a.shape, lambda i: (0, 0, 0)) for a in (fo, so, go)]
        args.append(wmo)
        specs.append(_wspec((D_MIX, D_MODEL), l))
    g, win, wout, l = ffn
    args += [g, win, wout]
    specs += [_lspec((1, D_MODEL), l), _wspec((D_MODEL, 2 * D_FF), l), _wspec((D_FF, D_MODEL), l)]
    return pl.pallas_call(
        functools.partial(_ffn_kernel, mix=mix[0] if mix else None, lead=bool(window)),
        out_shape=out_shape, grid=grid,
        in_specs=specs, out_specs=out_spec,
        compiler_params=_params(("parallel",) * len(grid)), name="mix_ffn" if mix else "ffn")(*args)


def _head_norm(x, gain_row, g64_ref):
    msq = _dot_hl(x * x, g64_ref[...])
    return x * lax.rsqrt(msq + EPS) * gain_row


def _small_block(sm, fb_ref, dtb_ref):
    logf = _log_sigmoid(sm + fb_ref[...])
    dt = _softplus(sm + dtb_ref[...])
    return logf, dt


def _proj_p_kernel(x_ref, nrm_ref, w_ref, qg_ref, kg_ref, fb_ref, dtb_ref, cw_ref, cb_ref, g64_ref,
                   ltri_ref,
                   qat_ref, ka_ref, kt_ref, vt_ref, vt16_ref, z_ref, xbc_ref, gqk_ref, gv_ref, gg_ref,
                   smr_ref, smt_ref, convp_ref,
                   xb_scr, carry_scr, *, t_real, grp):
    j = pl.program_id(1)

    @pl.when(j == 0)
    def _():
        xb_scr[:, 0:8, :] = jnp.zeros((grp, 8, SSD_CONV_DIM), F32)
        carry_scr[...] = jnp.zeros((grp, 8, LANES), F32)

    h = _rms(x_ref[...].reshape(grp * CHUNK, D_MODEL), nrm_ref[...]).astype(BF16)
    proj = lambda c0, n: jnp.dot(h, w_ref[:, c0:c0 + n], preferred_element_type=F32)
    seq = lambda a, g: a[g * CHUNK:(g + 1) * CHUNK]

    sm_all = proj(C_SM, LANES)
    lane = lax.broadcasted_iota(jnp.int32, (CHUNK, LANES), 1)
    row = lax.broadcasted_iota(jnp.int32, (CHUNK, LANES), 0)
    logf_all, dt_all = _small_block(sm_all, fb_ref, dtb_ref)
    f_rows = []
    for g in range(grp):
        logf = jnp.where(lane < SM_DT, seq(logf_all, g), 0.0)
        dt = jnp.where(j * CHUNK + row < t_real, seq(dt_all, g), 0.0)
        f = _dot_hl_left(ltri_ref[...], logf) + carry_scr[g, 0:1, :]
        carry_scr[g] = jnp.broadcast_to(f[CHUNK - 1:CHUNK, :], (8, LANES))
        smr_ref[g] = jnp.where(lane < SM_LR, dt, seq(sm_all, g))
        smt_ref[g] = jnp.where(lane < SM_DT, logf, dt).T[0:16, :]
        f_rows.append(f)

    q_all = _head_norm(proj(C_Q, FOX_W), qg_ref[...], g64_ref) * (FOX_HD ** -0.5 * LOG2E)
    k_all = _head_norm(proj(C_K, FOX_W), kg_ref[...], g64_ref)
    v_all = proj(C_V, FOX_W)
    l64 = lax.broadcasted_iota(jnp.int32, (CHUNK, FOX_HD), 1)
    for g in range(grp):
        q, k = seq(q_all, g), seq(k_all, g)
        qa, ka = [], []
        for hh in range(FOX_HEADS):
            y = f_rows[g][:, hh:hh + 1] * LOG2E
            hi = y.astype(BF16).astype(F32)
            mid = (y - hi).astype(BF16).astype(F32)
            lo = y - hi - mid
            parts = jnp.where(l64 == 0, hi, jnp.where(l64 == 1, mid, lo))
            ext_q = jnp.where(l64 < 3, parts, jnp.where(l64 < 6, 1.0, 0.0))
            parts = jnp.where(l64 == 3, hi, jnp.where(l64 == 4, mid, lo))
            ext_k = jnp.where(l64 < 3, 1.0, jnp.where(l64 < 6, -parts, 0.0))
            qa += [q[:, hh * FOX_HD:(hh + 1) * FOX_HD], ext_q]
            ka += [k[:, hh * FOX_HD:(hh + 1) * FOX_HD], ext_k]
        qat_ref[g] = jnp.concatenate(qa, axis=1).T.astype(BF16)
        ka_ref[g] = jnp.concatenate(ka, axis=1).astype(BF16)
        kt_ref[g] = k.T
        vt = seq(v_all, g).T
        vt_ref[g] = vt
        vt16_ref[g] = vt.astype(BF16)

    xbc_all = proj(C_XBC, SSD_CONV_DIM)
    r_end = t_real - (t_real - 1) // CHUNK * CHUNK
    for g in range(grp):
        xb_scr[g, 8:8 + CHUNK, :] = seq(xbc_all, g)
        conv = cb_ref[...]
        for w in range(SSD_CONV):
            conv = conv + xb_scr[g, pl.ds(8 - (SSD_CONV - 1) + w, CHUNK), :] * cw_ref[w:w + 1, :]
        xbc_ref[g] = _silu(conv)
        convp_ref[g] = xb_scr[g, r_end:r_end + 8, :]
        xb_scr[g, 0:8, :] = xb_scr[g, CHUNK:CHUNK + 8, :]

    z_ref[...] = proj(C_Z, SSD_W).reshape(grp, CHUNK, SSD_W)
    gqk_ref[...] = proj(C_GQ, 2 * GLA_KW).reshape(grp, CHUNK, 2 * GLA_KW)
    gv_ref[...] = proj(C_GV, GLA_W).reshape(grp, CHUNK, GLA_W)
    gg_ref[...] = proj(C_GG, GLA_W).reshape(grp, CHUNK, GLA_W)


def _proj_p(x3, nrm, w, qg, kg, fb, dtb, cw, cb, g64, ltri, t_real, l):
    nb, t_pad, _ = x3.shape
    nj = t_pad // CHUNK
    grp = math.gcd(nb, PROJ_GROUP)
    rows = lambda w_: pl.BlockSpec((grp, CHUNK, w_), lambda b, j: (b, j, 0))
    cols = lambda h_: pl.BlockSpec((grp, h_, CHUNK), lambda b, j: (b, 0, j))
    out_shape = [
        jax.ShapeDtypeStruct((nb, FOX_AUG, t_pad), BF16),
        jax.ShapeDtypeStruct((nb, t_pad, FOX_AUG), BF16),
        jax.ShapeDtypeStruct((nb, FOX_W, t_pad), F32),
        jax.ShapeDtypeStruct((nb, FOX_W, t_pad), F32),
        jax.ShapeDtypeStruct((nb, FOX_W, t_pad), BF16),
        jax.ShapeDtypeStruct((nb, t_pad, SSD_W), F32),
        jax.ShapeDtypeStruct((nb, t_pad, SSD_CONV_DIM), F32),
        jax.ShapeDtypeStruct((nb, t_pad, 2 * GLA_KW), F32),
        jax.ShapeDtypeStruct((nb, t_pad, GLA_W), F32),
        jax.ShapeDtypeStruct((nb, t_pad, GLA_W), F32),
        jax.ShapeDtypeStruct((nb, t_pad, LANES), F32),
        jax.ShapeDtypeStruct((nb, 16, t_pad), F32),
        jax.ShapeDtypeStruct((nb, 8, SSD_CONV_DIM), F32),
    ]
    out_specs = [cols(FOX_AUG), rows(FOX_AUG), cols(FOX_W), cols(FOX_W), cols(FOX_W), rows(SSD_W),
                 rows(SSD_CONV_DIM), rows(2 * GLA_KW), rows(GLA_W), rows(GLA_W), rows(LANES), cols(16),
                 pl.BlockSpec((grp, 8, SSD_CONV_DIM), lambda b, j: (b, 0, 0))]
    in_specs = [rows(D_MODEL), _lspec((1, D_MODEL), l), _lspec((D_MODEL, N_PROJ), l),
                _lspec((1, FOX_W), l), _lspec((1, FOX_W), l), _lspec((1, LANES), l),
                _lspec((1, LANES), l), _lspec((8, SSD_CONV_DIM), l), _lspec((1, SSD_CONV_DIM), l),
                _const_spec((FOX_W, FOX_W)), _const_spec((CHUNK, CHUNK))]
    return pl.pallas_call(
        functools.partial(_proj_p_kernel, t_real=t_real, grp=grp), out_shape=out_shape,
        grid=(nb // grp, nj), in_specs=in_specs, out_specs=out_specs,
        scratch_shapes=[pltpu.VMEM((grp, 8 + CHUNK, SSD_CONV_DIM), F32),
                        pltpu.VMEM((grp, 8, LANES), F32)],
        compiler_params=_params(("parallel", "arbitrary")), name="proj_p")(
            x3, nrm, w, qg, kg, fb, dtb, cw, cb, g64, ltri)


def _proj_s_kernel(x_ref, nrm_ref, w_ref, qg_ref, kg_ref, fb_ref, dtb_ref, cw_ref, cb_ref, g64_ref,
                   cst_ref,
                   q_ref, kr_ref, vr_ref, pt_ref, cum_ref, convs_ref,
                   hist_scr, carry_scr, *, n_t):
    t = pl.program_id(0)

    @pl.when(t == 0)
    def _():
        hist_scr[0:SSD_CONV - 1] = cst_ref[...]
        carry_scr[...] = jnp.zeros((LANES, LANES), F32)

    h = _rms(x_ref[...], nrm_ref[...]).astype(BF16)
    p = jnp.dot(h, w_ref[...], preferred_element_type=F32)

    q_ref[0] = _head_norm(p[:, C_Q:C_Q + FOX_W], qg_ref[...], g64_ref) * (FOX_HD ** -0.5)
    k = _head_norm(p[:, C_K:C_K + FOX_W], kg_ref[...], g64_ref)
    v = p[:, C_V:C_V + FOX_W]
    kr_ref[0] = k
    vr_ref[0] = v
    for c in range(FOX_W // LANES):
        pt_ref[0, R_K + c * LANES:R_K + (c + 1) * LANES, :] = k[:, c * LANES:(c + 1) * LANES].T
        pt_ref[0, R_V + c * LANES:R_V + (c + 1) * LANES, :] = v[:, c * LANES:(c + 1) * LANES].T

    hist_scr[pl.ds(SSD_CONV - 1 + t, 1)] = p[:, C_XBC:C_XBC + SSD_CONV_DIM][None]
    conv = cb_ref[...]
    for w in range(SSD_CONV):
        conv = conv + hist_scr[t + w] * cw_ref[w:w + 1, :]
    xbc = _silu(conv)

    @pl.when(t == n_t - 1)
    def _():
        convs_ref[...] = hist_scr[n_t:n_t + SSD_CONV - 1]

    sm = p[:, C_SM:C_SM + LANES]
    lane = lax.broadcasted_iota(jnp.int32, (LANES, LANES), 1)
    logf, dt = _small_block(sm, fb_ref, dtb_ref)
    cum = carry_scr[...] + jnp.where(lane < SM_DT, logf, 0.0)
    carry_scr[...] = cum
    cum_ref[0] = cum
    smc = jnp.where(lane < SM_DT, logf, jnp.where(lane < SM_LR, dt, sm))

    def put_t(r0, val):
        for c in range(val.shape[1] // LANES):
            pt_ref[0, r0 + c * LANES:r0 + (c + 1) * LANES, :] = val[:, c * LANES:(c + 1) * LANES].T

    put_t(R_Z, p[:, C_Z:C_Z + SSD_W])
    put_t(R_XBC, xbc)
    put_t(R_GQ, p[:, C_GQ:C_GQ + GLA_KW] * (GLA_DK ** -0.5))
    put_t(R_GK, p[:, C_GK:C_GK + GLA_KW])
    put_t(R_GV, p[:, C_GV:C_GV + GLA_W])
    put_t(R_GG, p[:, C_GG:C_GG + GLA_W])
    put_t(R_SM, smc)


def _proj_s(x, nrm, w, qg, kg, fb, dtb, cw, cb, g64, conv_state, l):
    n_t = x.shape[0] // LANES
    per_t = lambda a, b: pl.BlockSpec((1, a, b), lambda t: (t, 0, 0))
    out_shape = [
        jax.ShapeDtypeStruct((n_t, LANES, FOX_W), F32),
        jax.ShapeDtypeStruct((n_t, LANES, FOX_W), F32),
        jax.ShapeDtypeStruct((n_t, LANES, FOX_W), F32),
        jax.ShapeDtypeStruct((n_t, N_PT, LANES), F32),
        jax.ShapeDtypeStruct((n_t, LANES, LANES), F32),
        jax.ShapeDtypeStruct((SSD_CONV - 1, LANES, SSD_CONV_DIM), F32),
    ]
    out_specs = [per_t(LANES, FOX_W), per_t(LANES, FOX_W), per_t(LANES, FOX_W), per_t(N_PT, LANES),
                 per_t(LANES, LANES), _const_spec((SSD_CONV - 1, LANES, SSD_CONV_DIM))]
    in_specs = [pl.BlockSpec((LANES, D_MODEL), lambda t: (t, 0)), _lspec((1, D_MODEL), l),
                _lspec((D_MODEL, N_PROJ), l), _lspec((1, FOX_W), l), _lspec((1, FOX_W), l),
                _lspec((1, LANES), l), _lspec((1, LANES), l), _lspec((8, SSD_CONV_DIM), l),
                _lspec((1, SSD_CONV_DIM), l), _const_spec((FOX_W, FOX_W)),
                _lspec((SSD_CONV - 1, LANES, SSD_CONV_DIM), l)]
    return pl.pallas_call(
        functools.partial(_proj_s_kernel, n_t=n_t), out_shape=out_shape, grid=(n_t,),
        in_specs=in_specs, out_specs=out_specs,
        scratch_shapes=[pltpu.VMEM((n_t + SSD_CONV - 1, LANES, SSD_CONV_DIM), F32),
                        pltpu.VMEM((LANES, LANES), F32)],
        compiler_params=_params(("arbitrary",)), name="proj_s")(
            x, nrm, w, qg, kg, fb, dtb, cw, cb, g64, conv_state)


def _fox_p_kernel(qat_ref, ka_ref, vt_ref, o_ref, *, kw, qw, tile0):
    qi = pl.program_id(1) + tile0
    nsub = kw // qw
    n_full = qi // nsub

    def scores(start, width):
        return tuple(jnp.dot(ka_ref[0, pl.ds(start, width), h * LANES:(h + 1) * LANES],
                             qat_ref[0, h * LANES:(h + 1) * LANES, :], preferred_element_type=F32)
                     for h in range(FOX_HEADS))

    def update(ss, start, width, carry, masked):
        prs, stats = [], []
        for h in range(FOX_HEADS):
            m, l, _ = carry[h]
            s = ss[h]
            if masked:
                kpos = lax.broadcasted_iota(jnp.int32, (width, qw), 0) + start
                qpos = lax.broadcasted_iota(jnp.int32, (width, qw), 1) + qi * qw
                s = jnp.where(kpos <= qpos, s, -jnp.inf)
            m_new = jnp.maximum(m, jnp.max(s, axis=0, keepdims=True))
            alpha = jnp.exp2(m - m_new)
            pr = jnp.exp2(s - m_new)
            stats.append((m_new, alpha * l + jnp.sum(pr, axis=0, keepdims=True), alpha))
            prs.append(pr.astype(BF16))
        out = []
        for h in range(FOX_HEADS):
            vt = vt_ref[0, h * FOX_HD:(h + 1) * FOX_HD, pl.ds(start, width)]
            m_new, l, alpha = stats[h]
            out.append((m_new, l, alpha * carry[h][2] + jnp.dot(vt, prs[h], preferred_element_type=F32)))
        return tuple(out)

    init = tuple((jnp.full((1, qw), -jnp.inf, F32), jnp.zeros((1, qw), F32),
                  jnp.zeros((FOX_HD, qw), F32)) for _ in range(FOX_HEADS))

    def body(i, carry):
        at = pl.multiple_of(i * kw, kw)
        return update(scores(at, kw), at, kw, carry, False)

    carry = lax.fori_loop(0, n_full, body, init)
    start = pl.multiple_of(n_full * kw, kw)
    for r in range(nsub):
        @pl.when(lax.rem(qi, nsub) == r)
        def _(r=r):
            width = (r + 1) * qw
            fin = update(scores(start, width), start, width, carry, True)
            o_t = jnp.concatenate([acc / l for (_, l, acc) in fin], axis=0)
            o_ref[0] = o_t.T.astype(BF16)


def _fox_p_tiles(qat, ka, vt16, qw, tile0, n_tiles):
    nb, _, t_pad = qat.shape
    return pl.pallas_call(
        functools.partial(_fox_p_kernel, kw=FOX_KW, qw=qw, tile0=tile0),
        out_shape=jax.ShapeDtypeStruct((nb, n_tiles * qw, FOX_W), BF16), grid=(nb, n_tiles),
        in_specs=[pl.BlockSpec((1, FOX_AUG, qw), lambda b, i: (b, 0, i + tile0)),
                  pl.BlockSpec((1, t_pad, FOX_AUG), lambda b, i: (b, 0, 0)),
                  pl.BlockSpec((1, FOX_W, t_pad), lambda b, i: (b, 0, 0))],
        out_specs=pl.BlockSpec((1, qw, FOX_W), lambda b, i: (b, i, 0)),
        compiler_params=_params(("parallel", "arbitrary")), name="fox_p")(qat, ka, vt16)


def _fox_p(qat, ka, vt16):
    t_pad = qat.shape[2]
    n_wide = t_pad // FOX_QW
    rest = (t_pad - n_wide * FOX_QW) // CHUNK
    parts = []
    if n_wide:
        parts.append(_fox_p_tiles(qat, ka, vt16, FOX_QW, 0, n_wide))
    if rest:
        parts.append(_fox_p_tiles(qat, ka, vt16, CHUNK, n_wide * FOX_QW // CHUNK, rest))
    return parts[0] if len(parts) == 1 else jnp.concatenate(parts, axis=1)


def _fox_s_kernel(tbl_ref, q_ref, kr_ref, vr_ref, cum_ref, lf_ref, mstrict_ref, pgsuf_ref,
                  k_hbm, v_hbm, o_ref, kbuf, vbuf, lfst, sem, *, layer, n_pages, n_t):
    b = pl.program_id(0)
    nb = pl.num_programs(0)
    slot = lax.rem(b, 2)

    def copies(seq, sl):
        out = []
        for pg in range(n_pages):
            page = tbl_ref[seq * n_pages + pg]
            out.append(pltpu.make_async_copy(k_hbm.at[layer, page], kbuf.at[sl, pg], sem.at[0, sl]))
            out.append(pltpu.make_async_copy(v_hbm.at[layer, page], vbuf.at[sl, pg], sem.at[1, sl]))
        return out

    @pl.when(b == 0)
    def _():
        lfst[...] = jnp.zeros(lfst.shape, F32)
        for c in copies(0, 0):
            c.start()

    @pl.when(b + 1 < nb)
    def _():
        for c in copies(b + 1, 1 - slot):
            c.start()

    for pg in range(n_pages):
        page = tbl_ref[b * n_pages + pg]
        lfst[pg * 8:pg * 8 + FOX_HEADS, :] = lf_ref[0, page // 2, pl.ds(lax.rem(page, 2) * FOX_HEADS,
                                                                      FOX_HEADS), :]
    lf = lfst[...]
    within = _dot_hl(lf, mstrict_ref[...])
    tot = jnp.broadcast_to(jnp.sum(lf, axis=-1, keepdims=True), lf.shape)
    dsuf = within + _dot_hl_left(pgsuf_ref[...], tot)

    lane = lax.broadcasted_iota(jnp.int32, (8, FOX_W), 1)
    rowh = lax.broadcasted_iota(jnp.int32, (8, FOX_W), 0)
    headmask = (lane // FOX_HD) == rowh
    qexp, cumcol = [], []
    cum_rows = jnp.concatenate([cum_ref[t, pl.ds(b, 1), :] for t in range(n_t)]
                               + [jnp.zeros((LANES - n_t, LANES), F32)], axis=0)
    cum_t = cum_rows.T[0:8, 0:8]
    for t in range(n_t):
        qrow = q_ref[t, pl.ds(b, 1), :]
        qexp.append(jnp.where(headmask, jnp.broadcast_to(qrow, (8, FOX_W)), 0.0))
        cumcol.append(cum_t[:, t:t + 1])
    qexp = jnp.concatenate(qexp, axis=0).astype(BF16)
    cumcol = jnp.concatenate(cumcol, axis=0)
    rows = 8 * n_t

    kn = jnp.concatenate([kr_ref[t, pl.ds(b, 1), :] for t in range(n_t)]
                         + [jnp.zeros((8 - n_t, FOX_W), F32)], axis=0)
    vn = jnp.concatenate([vr_ref[t, pl.ds(b, 1), :] for t in range(n_t)]
                         + [jnp.zeros((8 - n_t, FOX_W), F32)], axis=0)
    s_new = _dot_nt(qexp, kn) + cumcol - jnp.concatenate([cum_t] * n_t, axis=0)
    tq = lax.broadcasted_iota(jnp.int32, (rows, 8), 0) // 8
    tk = lax.broadcasted_iota(jnp.int32, (rows, 8), 1)
    s_new = jnp.where(tk <= tq, s_new, -jnp.inf)

    for c in copies(b, slot):
        c.wait()

    s_pg = []
    for pg in range(n_pages):
        bias = jnp.concatenate([dsuf[pg * 8:(pg + 1) * 8, :]] * n_t, axis=0) + cumcol
        s_pg.append(_dot(qexp, kbuf[slot, pg]) + bias)
    m = s_pg[0]
    for s in s_pg[1:]:
        m = jnp.maximum(m, s)
    m = jnp.maximum(jnp.max(m, axis=-1, keepdims=True), jnp.max(s_new, axis=-1, keepdims=True))
    p_new = jnp.exp(s_new - m)
    l = jnp.sum(p_new, axis=-1, keepdims=True)
    acc = _dot(p_new, vn)
    lsum = None
    for pg in range(n_pages):
        pr = jnp.exp(s_pg[pg] - m)
        lsum = pr if lsum is None else lsum + pr
        acc = acc + _dot_nt(pr, vbuf[slot, pg])
    l = l + jnp.sum(lsum, axis=-1, keepdims=True)
    o = acc / l
    for t in range(n_t):
        ot = jnp.where(headmask, o[t * 8:(t + 1) * 8, :], 0.0)
        o_ref[t, pl.ds(b, 1), :] = jnp.sum(ot, axis=0, keepdims=True)


def _fox_s(tbl, q, kr, vr, cum, lfpool, mstrict, pgsuf, k_cache, v_cache, layer):
    n_t, nb, _ = q.shape
    n_pages = tbl.shape[0] // nb
    whole = lambda a: pl.BlockSpec(a.shape, lambda b, tbl_: (0,) * a.ndim)
    grid_spec = pltpu.PrefetchScalarGridSpec(
        num_scalar_prefetch=1, grid=(nb,),
        in_specs=[whole(q), whole(kr), whole(vr), whole(cum),
                  pl.BlockSpec((1,) + lfpool.shape[1:], lambda b, tbl_: (layer, 0, 0, 0)),
                  whole(mstrict), whole(pgsuf),
                  pl.BlockSpec(memory_space=pl.ANY), pl.BlockSpec(memory_space=pl.ANY)],
        out_specs=pl.BlockSpec((n_t, nb, FOX_W), lambda b, tbl_: (0, 0, 0)),
        scratch_shapes=[pltpu.VMEM((2, n_pages, FOX_W, LANES), F32),
                        pltpu.VMEM((2, n_pages, FOX_W, LANES), F32),
                        pltpu.VMEM((n_pages * 8, LANES), F32),
                        pltpu.SemaphoreType.DMA((2, 2))])
    return pl.pallas_call(
        functools.partial(_fox_s_kernel, layer=layer, n_pages=n_pages, n_t=n_t),
        out_shape=jax.ShapeDtypeStruct((n_t, nb, FOX_W), F32), grid_spec=grid_spec,
        compiler_params=_params(("arbitrary",)), name="fox_s")(
            tbl, q, kr, vr, cum, lfpool, mstrict, pgsuf, k_cache, v_cache)


def _ssd_p_kernel(z_ref, xbc_ref, smr_ref, smt_ref, arow_ref, acol_ref, ltri_ref, utri_ref, exph_ref,
                  dvec_ref, nrm_ref, o_ref, h_ref, ht_scr, *, grp):
    j = pl.program_id(1)
    nj = pl.num_programs(1)

    @pl.when(j == 0)
    def _():
        ht_scr[...] = jnp.zeros(ht_scr.shape, F32)

    finals = [_ssd_p_chunk(s, z_ref, xbc_ref, smr_ref, smt_ref, arow_ref, acol_ref, ltri_ref, utri_ref,
                           exph_ref, dvec_ref, nrm_ref, o_ref, ht_scr) for s in range(grp)]

    @pl.when(j == nj - 1)
    def _():
        rep = SSD_HEADS // SSD_GROUPS
        for s in range(grp):
            h_t = finals[s].T
            for hh in range(SSD_HEADS):
                g = hh // rep
                h_ref[s, hh] = h_t[hh * SSD_HD:(hh + 1) * SSD_HD, g * SSD_STATE:(g + 1) * SSD_STATE]


def _ssd_p_chunk(s, z_ref, xbc_ref, smr_ref, smt_ref, arow_ref, acol_ref, ltri_ref, utri_ref, exph_ref,
                 dvec_ref, nrm_ref, o_ref, ht_scr):
    smr = smr_ref[s]
    exph = exph_ref[...]
    cs_rows = _dot_hl_left(ltri_ref[...], smr * arow_ref[...])
    cs_t = _dot_hl(smt_ref[s] * acol_ref[...], utri_ref[...])
    dt_x = _dot_hl(smr, exph)
    c_hi, c_lo = _split(cs_rows)
    c_lo2 = (cs_rows - c_hi.astype(F32) - c_lo.astype(F32)).astype(BF16)
    cs_x = (jnp.dot(c_hi, exph, preferred_element_type=F32)
            + jnp.dot(c_lo, exph, preferred_element_type=F32)
            + jnp.dot(c_lo2, exph, preferred_element_type=F32))
    end_x = cs_x[CHUNK - 1:CHUNK, :]
    xbc = xbc_ref[s]
    x = xbc[:, 0:SSD_W]
    b_all = xbc[:, SSD_W:SSD_W + SSD_BC].astype(BF16)
    c_all = xbc[:, SSD_W + SSD_BC:SSD_W + 2 * SSD_BC].astype(BF16)
    xdt = x * dt_x
    xdt16 = xdt.astype(BF16)
    ht = ht_scr[s]
    y_state = jnp.dot(c_all, ht.astype(BF16), preferred_element_type=F32)
    upd = _dot_tn(b_all, xdt * jnp.exp(end_x - cs_x))
    cbs = [_dot_nt(c_all[:, g * SSD_STATE:(g + 1) * SSD_STATE], b_all[:, g * SSD_STATE:(g + 1) * SSD_STATE])
           for g in range(SSD_GROUPS)]

    gmask = (lax.broadcasted_iota(jnp.int32, (SSD_BC, SSD_W), 0) // SSD_STATE
             == lax.broadcasted_iota(jnp.int32, (SSD_BC, SSD_W), 1) // (SSD_W // SSD_GROUPS))
    ht_new = ht * jnp.exp(end_x) + jnp.where(gmask, upd, 0.0)
    ht_scr[s] = ht_new

    tril = (lax.broadcasted_iota(jnp.int32, (CHUNK, CHUNK), 1)
            <= lax.broadcasted_iota(jnp.int32, (CHUNK, CHUNK), 0))
    low = lax.broadcasted_iota(jnp.int32, (CHUNK, LANES), 1) < SSD_HD
    rep = SSD_HEADS // SSD_GROUPS
    pairs = []
    for pr in range(SSD_HEADS // 2):
        xp = xdt16[:, pr * LANES:(pr + 1) * LANES]
        ys = []
        for hh in (2 * pr, 2 * pr + 1):
            col = SM_DT + hh
            lm = jnp.exp(jnp.where(tril, cs_rows[:, col:col + 1] - cs_t[col:col + 1, :], -jnp.inf))
            ys.append(jnp.dot((cbs[hh // rep] * lm).astype(BF16), xp, preferred_element_type=F32))
        pairs.append(jnp.where(low, ys[0], ys[1]))
    y = jnp.concatenate(pairs, axis=1) + y_state * jnp.exp(cs_x)
    y = (y + x * dvec_ref[...]) * _silu(z_ref[s])
    gw = SSD_W // SSD_GROUPS
    for g in range(SSD_GROUPS):
        yg = y[:, g * gw:(g + 1) * gw]
        yn = yg * lax.rsqrt(jnp.mean(yg * yg, axis=-1, keepdims=True) + EPS)
        o_ref[s, :, g * gw:(g + 1) * gw] = (yn * nrm_ref[:, g * gw:(g + 1) * gw]).astype(BF16)
    return ht_new


def _ssd_p(z, xbc, smr, smt, arow, acol, ltri, utri, exph, dvec, nrm, l):
    nb, t_pad, _ = z.shape
    nj = t_pad // CHUNK
    grp = math.gcd(nb, MIX_GROUP)
    rows = lambda w_: pl.BlockSpec((grp, CHUNK, w_), lambda b, j: (b, j, 0))
    return pl.pallas_call(
        functools.partial(_ssd_p_kernel, grp=grp),
        out_shape=[jax.ShapeDtypeStruct((nb, t_pad, SSD_W), BF16),
                   jax.ShapeDtypeStruct((nb, SSD_HEADS, SSD_HD, SSD_STATE), F32)],
        grid=(nb // grp, nj),
        in_specs=[rows(SSD_W), rows(SSD_CONV_DIM), rows(LANES),
                  pl.BlockSpec((grp, 16, CHUNK), lambda b, j: (b, 0, j)),
                  _lspec((1, LANES), l), _lspec((16, 1), l), _const_spec((CHUNK, CHUNK)),
                  _const_spec((CHUNK, CHUNK)), _const_spec((LANES, SSD_W)),
                  _lspec((1, SSD_W), l), _lspec((1, SSD_W), l)],
        out_specs=[rows(SSD_W),
                   pl.BlockSpec((grp, SSD_HEADS, SSD_HD, SSD_STATE), lambda b, j: (b, 0, 0, 0))],
        scratch_shapes=[pltpu.VMEM((grp, SSD_BC, SSD_W), F32)],
        compiler_params=_params(("parallel", "arbitrary")), name="ssd_p")(
            z, xbc, smr, smt, arow, acol, ltri, utri, exph, dvec, nrm)


def _ssd_s_kernel(x_ref, b_ref, c_ref, sm_ref, xg_ref, zg_ref, a_ref, d_ref, nrm_ref, st_ref,
                  o_ref, so_ref, ybuf, *, n_t):
    hh = pl.program_id(0)
    rep = SSD_HEADS // SSD_GROUPS
    gw = SSD_W // SSD_GROUPS
    dt = [sm_ref[t, pl.ds(SM_DT + hh, 1), :] for t in range(n_t)]
    dec = [jnp.exp(dt[t] * a_ref[pl.ds(hh, 1), :]) for t in range(n_t)]

    def body(p, carry):
        hp = st_ref[0, 0, p]
        for t in range(n_t):
            hp = hp * dec[t] + b_ref[t] * (x_ref[t, pl.ds(p, 1), :] * dt[t])
            ybuf[t, pl.ds(hh * SSD_HD + p, 1), :] = jnp.sum(c_ref[t] * hp, axis=0, keepdims=True)
        so_ref[0, 0, p] = hp
        return carry

    lax.fori_loop(0, SSD_HD, body, 0)

    @pl.when(lax.rem(hh, rep) == rep - 1)
    def _():
        r0 = pl.multiple_of((hh // rep) * gw, gw)
        for t in range(n_t):
            y = ybuf[t, pl.ds(r0, gw), :] + xg_ref[t] * d_ref[pl.ds(r0, gw), :]
            y = y * _silu(zg_ref[t])
            yn = y * lax.rsqrt(jnp.mean(y * y, axis=0, keepdims=True) + EPS)
            o_ref[t, pl.ds(r0, gw), :] = yn * nrm_ref[pl.ds(r0, gw), :]


def _ssd_s(pt, a_b, d_b, nrm_b, state, layer):
    n_t = pt.shape[0]
    rep = SSD_HEADS // SSD_GROUPS
    gw = SSD_W // SSD_GROUPS
    blk = lambda h_, f: pl.BlockSpec((n_t, h_, LANES), lambda hh: (0, f(hh), 0))
    st_spec = pl.BlockSpec((1, 1, SSD_HD, SSD_STATE, LANES), lambda hh: (layer, hh, 0, 0, 0))
    return pl.pallas_call(
        functools.partial(_ssd_s_kernel, n_t=n_t),
        out_shape=[jax.ShapeDtypeStruct((n_t, SSD_W, LANES), F32),
                   jax.ShapeDtypeStruct((1, SSD_HEADS, SSD_HD, SSD_STATE, LANES), F32)],
        grid=(SSD_HEADS,),
        in_specs=[blk(SSD_HD, lambda hh: R_XBC // SSD_HD + hh),
                  blk(SSD_STATE, lambda hh: (R_XBC + SSD_W) // SSD_STATE + hh // rep),
                  blk(SSD_STATE, lambda hh: (R_XBC + SSD_W + SSD_BC) // SSD_STATE + hh // rep),
                  blk(LANES, lambda hh: R_SM // LANES),
                  blk(gw, lambda hh: R_XBC // gw + hh // rep),
                  blk(gw, lambda hh: R_Z // gw + hh // rep),
                  _lspec((SSD_HEADS, LANES), layer), _lspec((SSD_W, LANES), layer),
                  _lspec((SSD_W, LANES), layer), st_spec],
        out_specs=[_const_spec((n_t, SSD_W, LANES)),
                   pl.BlockSpec((1, 1, SSD_HD, SSD_STATE, LANES), lambda hh: (0, hh, 0, 0, 0))],
        scratch_shapes=[pltpu.VMEM((n_t, SSD_W, LANES), F32)],
        compiler_params=_params(("arbitrary",)), name="ssd_s")(
            pt, pt, pt, pt, pt, pt, a_b, d_b, nrm_b, state)


def _gla_p_kernel(gqk_ref, gv_ref, gg_ref, smr_ref, wg_ref, gb_ref, bl_ref, be_ref, g64_ref, gn_ref,
                  o_ref, s_ref, st_scr, *, t_real, grp):
    j = pl.program_id(1)
    nj = pl.num_programs(1)

    @pl.when(j == 0)
    def _():
        st_scr[...] = jnp.zeros(st_scr.shape, F32)

    finals = [_gla_p_chunk(s, j, gqk_ref, gv_ref, gg_ref, smr_ref, wg_ref, gb_ref, bl_ref, be_ref,
                           g64_ref, gn_ref, o_ref, st_scr, t_real) for s in range(grp)]

    @pl.when(j == nj - 1)
    def _():
        for s in range(grp):
            s_kv = finals[s].T
            for h in range(GLA_HEADS):
                s_ref[s, h] = s_kv[h * GLA_DK:(h + 1) * GLA_DK, h * GLA_DV:(h + 1) * GLA_DV]


def _gla_p_chunk(s, j, gqk_ref, gv_ref, gg_ref, smr_ref, wg_ref, gb_ref, bl_ref, be_ref, g64_ref,
                 gn_ref, o_ref, st_scr, t_real):
    sub = GLA_SUB
    n_sub = CHUNK // sub
    row = lax.broadcasted_iota(jnp.int32, (CHUNK, GLA_KW), 0)
    valid = j * CHUNK + row < t_real
    glog = _log_sigmoid(_dot(smr_ref[s], wg_ref[...]) + gb_ref[...]) * (1.0 / GLA_TAU)
    glog = jnp.where(valid, glog, 0.0)
    gqk = gqk_ref[s]
    gq = gqk[:, 0:GLA_KW] * (GLA_DK ** -0.5)
    gk = jnp.where(valid, gqk[:, GLA_KW:2 * GLA_KW], 0.0)
    v = gv_ref[s]
    bcl = _dot_hl_left(bl_ref[...], glog)
    tot = _dot_hl_left(be_ref[0], glog)
    mid = _dot_hl_left(be_ref[1], glog)
    qe = (gq * jnp.exp(bcl)).astype(BF16)
    qd = (gq * jnp.exp(bcl - mid)).astype(BF16)
    kp = gk * jnp.exp(mid - bcl)
    kend = (gk * jnp.exp(tot - bcl)).astype(BF16)
    dec = jnp.exp(tot)

    r_k = lax.broadcasted_iota(jnp.int32, (GLA_HEADS * sub, GLA_KW), 0) // sub
    c_k = lax.broadcasted_iota(jnp.int32, (GLA_HEADS * sub, GLA_KW), 1) // GLA_DK
    r_v = lax.broadcasted_iota(jnp.int32, (GLA_HEADS * sub, GLA_W), 0) // sub
    c_v = lax.broadcasted_iota(jnp.int32, (GLA_HEADS * sub, GLA_W), 1) // GLA_DV
    r_s = lax.broadcasted_iota(jnp.int32, (GLA_W, GLA_KW), 0) // GLA_DV
    c_s = lax.broadcasted_iota(jnp.int32, (GLA_W, GLA_KW), 1) // GLA_DK
    causal = (lax.broadcasted_iota(jnp.int32, (sub, GLA_HEADS * sub), 1) % sub
              <= lax.broadcasted_iota(jnp.int32, (sub, GLA_HEADS * sub), 0))
    sls = [slice(i * sub, (i + 1) * sub) for i in range(n_sub)]
    atts, upds = [], []
    for sl in sls:
        kbd = jnp.where(r_k == c_k, jnp.concatenate([kp[sl]] * GLA_HEADS, axis=0), 0.0)
        atts.append(_dot_nt(qd[sl], kbd))
        upds.append(_dot_tn(v[sl], kend[sl]))
    sts = [st_scr[s]]
    for i in range(n_sub):
        sts.append(sts[i] * dec[i * sub:i * sub + 1, :] + jnp.where(r_s == c_s, upds[i], 0.0))
    st_scr[s] = sts[n_sub]
    outs = []
    for i, sl in enumerate(sls):
        vbd = jnp.where(r_v == c_v, jnp.concatenate([v[sl]] * GLA_HEADS, axis=0), 0.0)
        outs.append(_dot(jnp.where(causal, atts[i], 0.0), vbd) + _dot_nt(qe[sl], sts[i]))
    o = jnp.concatenate(outs, axis=0)
    msq = _dot_hl(o * o, g64_ref[...])
    o_ref[s] = (o * lax.rsqrt(msq + EPS) * gn_ref[...] * _silu(gg_ref[s])).astype(BF16)
    return sts[n_sub]


def _gla_p(gqk, gv, gg, smr, wgp, gb, bl, be, g64, gn, t_real, l):
    nb, t_pad, _ = gv.shape
    nj = t_pad // CHUNK
    grp = math.gcd(nb, MIX_GROUP)
    rows = lambda w_: pl.BlockSpec((grp, CHUNK, w_), lambda b, j: (b, j, 0))
    return pl.pallas_call(
        functools.partial(_gla_p_kernel, t_real=t_real, grp=grp),
        out_shape=[jax.ShapeDtypeStruct((nb, t_pad, GLA_W), BF16),
                   jax.ShapeDtypeStruct((nb, GLA_HEADS, GLA_DK, GLA_DV), F32)],
        grid=(nb // grp, nj),
        in_specs=[rows(2 * GLA_KW), rows(GLA_W), rows(GLA_W), rows(LANES),
                  _lspec((LANES, GLA_KW), l), _lspec((1, GLA_KW), l),
                  _const_spec((CHUNK, CHUNK)), _const_spec((2, CHUNK, CHUNK)),
                  _const_spec((GLA_W, GLA_W)), _lspec((1, GLA_W), l)],
        out_specs=[rows(GLA_W),
                   pl.BlockSpec((grp, GLA_HEADS, GLA_DK, GLA_DV), lambda b, j: (b, 0, 0, 0))],
        scratch_shapes=[pltpu.VMEM((grp, GLA_W, GLA_KW), F32)],
        compiler_params=_params(("parallel", "arbitrary")), name="gla_p")(
            gqk, gv, gg, smr, wgp, gb, bl, be, g64, gn)


def _gla_s_kernel(q_ref, k_ref, v_ref, gg_ref, sm_ref, wgt_ref, gb_ref, gn_ref, st_ref,
                  o_ref, so_ref, eg_scr, acc_scr, *, n_t):
    for t in range(n_t):
        glog = _log_sigmoid(_dot(wgt_ref[...], sm_ref[t]) + gb_ref[...]) * (1.0 / GLA_TAU)
        eg_scr[t] = jnp.exp(glog)
        acc_scr[t] = jnp.zeros((GLA_DV, LANES), F32)

    def body(kk, carry):
        s = st_ref[0, 0, kk]
        for t in range(n_t):
            s = s * eg_scr[t, pl.ds(kk, 1), :] + k_ref[t, pl.ds(kk, 1), :] * v_ref[t]
            acc_scr[t] = acc_scr[t] + q_ref[t, pl.ds(kk, 1), :] * s
        so_ref[0, 0, kk] = s
        return carry

    lax.fori_loop(0, GLA_DK, body, 0)
    for t in range(n_t):
        o = acc_scr[t]
        on = o * lax.rsqrt(jnp.mean(o * o, axis=0, keepdims=True) + EPS)
        o_ref[t] = on * gn_ref[...] * _silu(gg_ref[t])


def _gla_s(pt, wgt, gb_b, gn_b, state, layer):
    n_t = pt.shape[0]
    blk = lambda h_, f: pl.BlockSpec((n_t, h_, LANES), lambda hh: (0, f(hh), 0))
    return pl.pallas_call(
        functools.partial(_gla_s_kernel, n_t=n_t),
        out_shape=[jax.ShapeDtypeStruct((n_t, GLA_W, LANES), F32),
                   jax.ShapeDtypeStruct((1, GLA_HEADS, GLA_DK, GLA_DV, LANES), F32)],
        grid=(GLA_HEADS,),
        in_specs=[blk(GLA_DK, lambda hh: R_GQ // GLA_DK + hh),
                  blk(GLA_DK, lambda hh: R_GK // GLA_DK + hh),
                  blk(GLA_DV, lambda hh: R_GV // GLA_DV + hh),
                  blk(GLA_DV, lambda hh: R_GG // GLA_DV + hh),
                  blk(LANES, lambda hh: R_SM // LANES),
                  pl.BlockSpec((None, GLA_DK, LANES), lambda hh: (layer, hh, 0)),
                  pl.BlockSpec((None, GLA_DK, LANES), lambda hh: (layer, hh, 0)),
                  _lspec((GLA_DV, LANES), layer),
                  pl.BlockSpec((1, 1, GLA_DK, GLA_DV, LANES), lambda hh: (layer, hh, 0, 0, 0))],
        out_specs=[blk(GLA_DV, lambda hh: hh),
                   pl.BlockSpec((1, 1, GLA_DK, GLA_DV, LANES), lambda hh: (0, hh, 0, 0, 0))],
        scratch_shapes=[pltpu.VMEM((n_t, GLA_DK, LANES), F32), pltpu.VMEM((n_t, GLA_DV, LANES), F32)],
        compiler_params=_params(("arbitrary",)), name="gla_s")(
            pt, pt, pt, pt, pt, wgt, gb_b, gn_b, state)


def _tri_consts(n_pages):
    i = np.arange(CHUNK)
    ltri = (i[None, :] <= i[:, None]).astype(np.float32)
    utri = ltri.T
    mstrict = (i[:, None] > i[None, :]).astype(np.float32)
    same = (i[:, None] // GLA_SUB) == (i[None, :] // GLA_SUB)
    bl = (same & (i[None, :] <= i[:, None])).astype(np.float32)
    to_mid = i[None, :] <= (i[:, None] // GLA_SUB) * GLA_SUB + GLA_SUB // 2 - 1
    be = np.stack([same, same & to_mid]).astype(np.float32)
    r = np.arange(n_pages * 8)
    pgsuf = ((r[:, None] % 8 == r[None, :] % 8) & (r[None, :] // 8 > r[:, None] // 8)).astype(np.float32)
    h = np.arange(FOX_W)
    g64 = ((h[:, None] // 64) == (h[None, :] // 64)).astype(np.float32) / 64.0
    cw = np.arange(SSD_W)
    exph = (i[:, None] == SM_DT + cw[None, :] // SSD_HD).astype(np.float32)
    c = lambda a: jnp.asarray(a, BF16)
    return dict(ltri=c(ltri), utri=c(utri), mstrict=c(mstrict), bl=c(bl), be=c(be), pgsuf=c(pgsuf),
                g64=c(g64), exph=c(exph))


def kernel(x_prompt, x_sample, cache_fox_k, cache_fox_v, cache_fox_logf, state_ssm, state_conv,
           state_gla, page_table, meta_tokens, ffn1_norm, ffn1_w_in, ffn1_w_out, mix_norm, w_mix_in,
           fox_q_norm, fox_k_norm, fox_f_bias, ssd_conv_w, ssd_conv_b, ssd_dt_bias, ssd_a_log, ssd_d,
           ssd_norm, gla_w_gate, gla_gate_bias, gla_norm, w_mix_out, ffn2_norm, ffn2_w_in, ffn2_w_out):
    nbp, seq, _ = x_prompt.shape
    nbs, n_t, _ = x_sample.shape
    depth = ffn1_norm.shape[0]
    assert nbs == LANES and seq % CHUNK == 0
    t_real = N_META + seq
    t_pad = -(-t_real // CHUNK) * CHUNK
    n_pool, page_size = cache_fox_k.shape[1], cache_fox_k.shape[2]
    assert page_size == LANES and n_pool % 2 == 0
    n_pages = page_table.shape[1]
    consts = _tri_consts(n_pages)

    meta = jnp.broadcast_to(meta_tokens[None], (nbp, N_META, D_MODEL))
    xp = jnp.concatenate([meta, x_prompt, jnp.zeros((nbp, t_pad - t_real, D_MODEL), F32)], axis=1)
    xp = xp.reshape(nbp * t_pad, D_MODEL)
    xs = jnp.transpose(x_sample, (1, 0, 2)).reshape(n_t * nbs, D_MODEL)

    k_cache = jnp.transpose(cache_fox_k, (0, 1, 3, 4, 2)).reshape(depth, n_pool, FOX_W, page_size)
    v_cache = jnp.transpose(cache_fox_v, (0, 1, 3, 4, 2)).reshape(depth, n_pool, FOX_W, page_size)
    lfpool = jnp.transpose(cache_fox_logf, (0, 1, 3, 2)).reshape(depth, n_pool // 2, 8, page_size)
    ssm_in = jnp.transpose(state_ssm, (0, 2, 3, 4, 1))
    gla_in = jnp.transpose(state_gla, (0, 2, 3, 4, 1))
    conv_in = jnp.transpose(state_conv, (0, 2, 1, 3))
    tbl = page_table.reshape(-1).astype(jnp.int32)

    w1i, w1o = ffn1_w_in.astype(BF16), ffn1_w_out.astype(BF16)
    w2i, w2o = ffn2_w_in.astype(BF16), ffn2_w_out.astype(BF16)
    wmo = w_mix_out.astype(BF16)
    col = [0] + list(IN_SPLITS) + [w_mix_in.shape[-1]]
    part = lambda i: w_mix_in[:, :, col[i]:col[i + 1]]
    fq, fk, fv, ff, sz, sxbc, sdt, gq, gk, gv, glr, gg = [part(i) for i in range(len(IN_SIZES))]
    n_small = FOX_HEADS + SSD_HEADS + GLA_RANK
    wproj = jnp.concatenate([fq, fk, fv, sz, sxbc, gq, gk, gv, gg, ff, sdt, glr,
                             jnp.zeros((depth, D_MODEL, LANES - n_small), F32)], axis=-1).astype(BF16)
    n1, n2, nm = ffn1_norm[:, None], ffn2_norm[:, None], mix_norm[:, None]
    qg = jnp.tile(fox_q_norm, (1, FOX_HEADS))[:, None]
    kg = jnp.tile(fox_k_norm, (1, FOX_HEADS))[:, None]
    a = -jnp.exp(ssd_a_log)
    lanes_pad = lambda v, off: jnp.pad(v, ((0, 0), (off, LANES - off - v.shape[1])))[:, None]
    fb, dtb, arow = lanes_pad(fox_f_bias, SM_F), lanes_pad(ssd_dt_bias, SM_DT), lanes_pad(a, SM_DT)
    acol = jnp.pad(a, ((0, 0), (SM_DT, 16 - SM_DT - SSD_HEADS)))[:, :, None]
    cw = jnp.pad(ssd_conv_w, ((0, 0), (0, 8 - SSD_CONV), (0, 0)))
    cb = ssd_conv_b[:, None]
    d_rep = jnp.repeat(ssd_d, SSD_HD, axis=1)
    dvec, snrm = d_rep[:, None], ssd_norm[:, None]
    wgp = jnp.pad(gla_w_gate, ((0, 0), (SM_LR, LANES - SM_LR - GLA_RANK), (0, 0))).astype(BF16)
    gb = gla_gate_bias[:, None]
    gn = jnp.tile(gla_norm, (1, GLA_HEADS))[:, None]
    on_lanes = lambda v: jnp.broadcast_to(v[:, :, None], v.shape + (LANES,))
    a_b, d_b, snrm_b = on_lanes(a), on_lanes(d_rep), on_lanes(ssd_norm)
    wgt = jnp.pad(jnp.transpose(gla_w_gate, (0, 2, 1)),
                  ((0, 0), (0, 0), (SM_LR, LANES - SM_LR - GLA_RANK))).astype(BF16)
    gb_b, gn_b = on_lanes(gla_gate_bias), on_lanes(gla_norm)

    outs_p = [[] for _ in range(6)]
    outs_s = [[] for _ in range(6)]
    for l in range(depth):
        last = l == depth - 1
        xp = _ffn(xp, FFN_TM, (n1, w1i, w1o, l))
        (qat, kaug, kt, vt, vt16, z, xbc, gqk, gvv, ggg, smr, smt, convp) = _proj_p(
            xp.reshape(nbp, t_pad, D_MODEL), nm, wproj, qg, kg, fb, dtb, cw, cb, consts["g64"],
            consts["ltri"], t_real, l)
        fox_o = _fox_p(qat, kaug, vt16)
        ssd_o, ssm_p = _ssd_p(z, xbc, smr, smt, arow, acol, consts["ltri"], consts["utri"],
                              consts["exph"], dvec, snrm, l)
        gla_o, gla_p = _gla_p(gqk, gvv, ggg, smr, wgp, gb, consts["bl"], consts["be"], consts["g64"],
                              gn, t_real, l)
        mix = ("rows", fox_o.reshape(-1, FOX_W), ssd_o.reshape(-1, SSD_W), gla_o.reshape(-1, GLA_W), wmo, l)
        window = (nbp, t_pad, N_META, seq) if last and seq % FFN_TM == 0 else None
        xp = _ffn(xp, FFN_TM, (n2, w2i, w2o, l), mix=mix, window=window)
        outs_p[0].append(kt.reshape(nbp, FOX_HEADS, FOX_HD, t_pad))
        outs_p[1].append(vt.reshape(nbp, FOX_HEADS, FOX_HD, t_pad))
        outs_p[2].append(jnp.transpose(smt[:, 0:FOX_HEADS, :t_real], (0, 2, 1)))
        outs_p[3].append(ssm_p)
        outs_p[4].append(convp[:, 8 - (SSD_CONV - 1):])
        outs_p[5].append(gla_p)

        xs = _ffn(xs, xs.shape[0], (n1, w1i, w1o, l))
        qs, krs, vrs, pt, cum, convs = _proj_s(xs, nm, wproj, qg, kg, fb, dtb, cw, cb, consts["g64"],
                                               conv_in, l)
        fox_os = _fox_s(tbl, qs, krs, vrs, cum, lfpool, consts["mstrict"], consts["pgsuf"],
                        k_cache, v_cache, l)
        ssd_ot, ssm_s = _ssd_s(pt, a_b, d_b, snrm_b, ssm_in, l)
        gla_ot, gla_s = _gla_s(pt, wgt, gb_b, gn_b, gla_in, l)
        xs = _ffn(xs, xs.shape[0], (n2, w2i, w2o, l), mix=("features", fox_os, ssd_ot, gla_ot, wmo, l))
        kts = pt[:, R_K:R_K + FOX_W].reshape(n_t, FOX_HEADS, FOX_HD, nbs)
        vts = pt[:, R_V:R_V + FOX_W].reshape(n_t, FOX_HEADS, FOX_HD, nbs)
        outs_s[0].append(jnp.transpose(kts, (3, 0, 1, 2)))
        outs_s[1].append(jnp.transpose(vts, (3, 0, 1, 2)))
        outs_s[2].append(jnp.transpose(pt[:, R_SM + SM_F:R_SM + SM_F + FOX_HEADS], (2, 0, 1)))
        outs_s[3].append(jnp.transpose(ssm_s[0], (3, 0, 1, 2)))
        outs_s[4].append(jnp.transpose(convs, (1, 0, 2)))
        outs_s[5].append(jnp.transpose(gla_s[0], (3, 0, 1, 2)))

    y_prompt = xp if xp.ndim == 3 else xp.reshape(nbp, t_pad, D_MODEL)[:, N_META:t_real]
    y_sample = jnp.transpose(xs.reshape(n_t, nbs, D_MODEL), (1, 0, 2))
    k_p = jnp.transpose(jnp.stack(outs_p[0])[..., :t_real], (0, 1, 4, 2, 3))
    v_p = jnp.transpose(jnp.stack(outs_p[1])[..., :t_real], (0, 1, 4, 2, 3))
    lf_p, ssm_p, conv_p, gla_p = [jnp.stack(a) for a in outs_p[2:]]
    k_s, v_s, lf_s, ssm_s, conv_s, gla_s = [jnp.stack(a) for a in outs_s]
    return (y_prompt, y_sample, k_p, v_p, lf_p, ssm_p, conv_p, gla_p, k_s, v_s, lf_s, ssm_s, conv_s, gla_s)
```

```python
import functools
import math

import jax
import jax.numpy as jnp
import numpy as np
from jax import lax
from jax.experimental import pallas as pl
from jax.experimental.pallas import tpu as pltpu

F32 = jnp.float32
BF16 = jnp.bfloat16

D_MODEL = 1024
N_META = 16
D_FF = 2816
EPS = 1e-6
FOX_HEADS = 4
FOX_HD = 64
FOX_W = FOX_HEADS * FOX_HD
SSD_HEADS = 8
SSD_HD = 64
SSD_W = SSD_HEADS * SSD_HD
SSD_GROUPS = 2
SSD_STATE = 64
SSD_CONV = 4
SSD_BC = SSD_GROUPS * SSD_STATE
SSD_CONV_DIM = SSD_W + 2 * SSD_BC
GLA_HEADS = 4
GLA_DK = 32
GLA_DV = 64
GLA_KW = GLA_HEADS * GLA_DK
GLA_W = GLA_HEADS * GLA_DV
GLA_RANK = 16
GLA_TAU = 16.0
D_MIX = FOX_W + SSD_W + GLA_W
IN_SIZES = (FOX_W, FOX_W, FOX_W, FOX_HEADS, SSD_W, SSD_CONV_DIM, SSD_HEADS,
            GLA_KW, GLA_KW, GLA_W, GLA_RANK, GLA_W)
IN_SPLITS = tuple(int(v) for v in np.cumsum(IN_SIZES)[:-1])

LOG2E = 1.4426950408889634
FOX_AUG = FOX_HEADS * 128

LANES = 128
CHUNK = 128
GLA_SUB = 32
FOX_KW = 512
FOX_QW = 512
FFN_CK = 256
FFN_TM = 512
PROJ_GROUP = 4
DEC_UNROLL = 4
MIX_GROUP = 8
VMEM_LIMIT = 60 * 1024 * 1024

C_Q, C_K, C_V = 0, 256, 512
C_Z = 768
C_XBC = 1280
C_GQ, C_GK, C_GV, C_GG = 2048, 2176, 2304, 2560
C_SM = 2816
N_PROJ = 2944
SM_F = 0
SM_DT = 4
SM_LR = 12
R_K, R_V, R_Z, R_XBC = 0, 256, 512, 1024
R_GQ, R_GK, R_GV, R_GG, R_SM = 1792, 1920, 2048, 2304, 2560
N_PT = 2688


def _dot(a, b):
    return jnp.dot(a.astype(BF16), b.astype(BF16), preferred_element_type=F32)


def _dot_nt(a, b):
    return lax.dot_general(a.astype(BF16), b.astype(BF16), (((1,), (1,)), ((), ())),
                           preferred_element_type=F32)


def _dot_tn(a, b):
    return lax.dot_general(a.astype(BF16), b.astype(BF16), (((0,), (0,)), ((), ())),
                           preferred_element_type=F32)


def _split(a):
    hi = a.astype(BF16)
    lo = (a - hi.astype(F32)).astype(BF16)
    return hi, lo


def _dot_hl(a, m):
    hi, lo = _split(a)
    return (jnp.dot(hi, m, preferred_element_type=F32) + jnp.dot(lo, m, preferred_element_type=F32))


def _dot_hl_left(m, a):
    hi, lo = _split(a)
    return (jnp.dot(m, hi, preferred_element_type=F32) + jnp.dot(m, lo, preferred_element_type=F32))


def _silu(x):
    return x * jax.nn.sigmoid(x)


def _softplus(x):
    return jnp.maximum(x, 0.0) + jnp.log1p(jnp.exp(-jnp.abs(x)))


def _log_sigmoid(x):
    return jnp.minimum(x, 0.0) - jnp.log1p(jnp.exp(-jnp.abs(x)))


def _rms(x, g):
    return x * lax.rsqrt(jnp.mean(x * x, axis=-1, keepdims=True) + EPS) * g


def _const_spec(shape):
    n = len(shape)
    return pl.BlockSpec(shape, lambda *_: (0,) * n)


def _lspec(shape, l):
    n = len(shape)
    return pl.BlockSpec((None,) + tuple(shape), lambda *_: (l,) + (0,) * n)


def _wspec(shape, l):
    n = len(shape)
    return pl.BlockSpec((None,) + tuple(shape), lambda *_: (l,) + (0,) * n, pipeline_mode=pl.Buffered(1))


def _params(sem):
    return pltpu.CompilerParams(dimension_semantics=sem, vmem_limit_bytes=VMEM_LIMIT)


def _swiglu_half(x, g_ref, win_ref, wout_ref):
    h = _rms(x, g_ref[...]).astype(BF16)
    acc = None
    for c in range(D_FF // FFN_CK):
        g = jnp.dot(h, win_ref[:, c * FFN_CK:(c + 1) * FFN_CK], preferred_element_type=F32)
        u = jnp.dot(h, win_ref[:, D_FF + c * FFN_CK:D_FF + (c + 1) * FFN_CK],
                    preferred_element_type=F32)
        a = (_silu(g) * u).astype(BF16)
        part = jnp.dot(a, wout_ref[c * FFN_CK:(c + 1) * FFN_CK, :], preferred_element_type=F32)
        acc = part if acc is None else acc + part
    return x + 0.5 * acc


def _ffn_kernel(*refs, mix, lead):
    rows = (lambda r: r[0]) if lead else (lambda r: r[...])
    x = rows(refs[0])
    if mix:
        fo_ref, so_ref, go_ref, wmo_ref = refs[1:5]
        if mix == "rows":
            fo, so, go = rows(fo_ref), rows(so_ref), rows(go_ref)
        else:
            n_t = fo_ref.shape[0]
            fo = fo_ref[...].reshape(n_t * LANES, FOX_W)
            so = jnp.concatenate([so_ref[t].T for t in range(n_t)], axis=0)
            go = jnp.concatenate([go_ref[t].T for t in range(n_t)], axis=0)
        x = x + _dot(fo, wmo_ref[0:FOX_W, :])
        x = x + _dot(so, wmo_ref[FOX_W:FOX_W + SSD_W, :])
        x = x + _dot(go, wmo_ref[FOX_W + SSD_W:D_MIX, :])
    g_ref, win_ref, wout_ref, o_ref = refs[-4:]
    o_ref[...] = _swiglu_half(x, g_ref, win_ref, wout_ref)


def _ffn(x, tm, ffn, mix=None, window=None):
    if window:
        nseq, t_in, row0, t_out = window
        grid = (nseq, t_out // tm)
        view = lambda a: a.reshape(nseq, t_in, a.shape[-1])
        row = lambda w: pl.BlockSpec((pl.Element(1), pl.Element(tm), pl.Element(w)),
                                     lambda b, i: (b, pl.multiple_of(row0 + i * tm, math.gcd(row0, tm)), 0))
        out_shape = jax.ShapeDtypeStruct((nseq, t_out, D_MODEL), F32)
        out_spec = pl.BlockSpec((None, tm, D_MODEL), lambda b, i: (b, i, 0))
    else:
        rows = x.shape[0]
        grid = (rows // tm,)
        view = lambda a: a
        row = lambda w: pl.BlockSpec((tm, w), lambda i: (i, 0))
        out_shape = jax.ShapeDtypeStruct((rows, D_MODEL), F32)
        out_spec = row(D_MODEL)
    args, specs = [view(x)], [row(D_MODEL)]
    if mix:
        kind, fo, so, go, wmo, l = mix
        if kind == "rows":
            args += [view(fo), view(so), view(go)]
            specs += [row(FOX_W), row(SSD_W), row(GLA_W)]
        else:
            assert not window and tm == x.shape[0]
            args += [fo, so, go]
            specs += [pl.BlockSpec(a.shape, lambda i: (0, 0, 0)) for a in (fo, so, go)]
        args.append(wmo)
        specs.append(_wspec((D_MIX, D_MODEL), l))
    g, win, wout, l = ffn
    args += [g, win, wout]
    specs += [_lspec((1, D_MODEL), l), _wspec((D_MODEL, 2 * D_FF), l), _wspec((D_FF, D_MODEL), l)]
    return pl.pallas_call(
        functools.partial(_ffn_kernel, mix=mix[0] if mix else None, lead=bool(window)),
        out_shape=out_shape, grid=grid,
        in_specs=specs, out_specs=out_spec,
        compiler_params=_params(("parallel",) * len(grid)), name="mix_ffn" if mix else "ffn")(*args)


def _head_norm(x, gain_row, g64_ref):
    msq = _dot_hl(x * x, g64_ref[...])
    return x * lax.rsqrt(msq + EPS) * gain_row


def _small_block(sm, fb_ref, dtb_ref):
    logf = _log_sigmoid(sm + fb_ref[...])
    dt = _softplus(sm + dtb_ref[...])
    return logf, dt


def _proj_p_kernel(x_ref, nrm_ref, w_ref, qg_ref, kg_ref, fb_ref, dtb_ref, cw_ref, cb_ref, g64_ref,
                   ltri_ref,
                   qat_ref, ka_ref, kt_ref, vt_ref, vt16_ref, z_ref, xbc_ref, gqk_ref, gv_ref, gg_ref,
                   smr_ref, smt_ref, convp_ref,
                   xb_scr, carry_scr, *, t_real, grp):
    j = pl.program_id(1)

    @pl.when(j == 0)
    def _():
        xb_scr[:, 0:8, :] = jnp.zeros((grp, 8, SSD_CONV_DIM), F32)
        carry_scr[...] = jnp.zeros((grp, 8, LANES), F32)

    h = _rms(x_ref[...].reshape(grp * CHUNK, D_MODEL), nrm_ref[...]).astype(BF16)
    proj = lambda c0, n: jnp.dot(h, w_ref[:, c0:c0 + n], preferred_element_type=F32)
    seq = lambda a, g: a[g * CHUNK:(g + 1) * CHUNK]

    sm_all = proj(C_SM, LANES)
    lane = lax.broadcasted_iota(jnp.int32, (CHUNK, LANES), 1)
    row = lax.broadcasted_iota(jnp.int32, (CHUNK, LANES), 0)
    logf_all, dt_all = _small_block(sm_all, fb_ref, dtb_ref)
    f_rows = []
    for g in range(grp):
        logf = jnp.where(lane < SM_DT, seq(logf_all, g), 0.0)
        dt = jnp.where(j * CHUNK + row < t_real, seq(dt_all, g), 0.0)
        f = _dot_hl_left(ltri_ref[...], logf) + carry_scr[g, 0:1, :]
        carry_scr[g] = jnp.broadcast_to(f[CHUNK - 1:CHUNK, :], (8, LANES))
        smr_ref[g] = jnp.where(lane < SM_LR, dt, seq(sm_all, g))
        smt_ref[g] = jnp.where(lane < SM_DT, logf, dt).T[0:16, :]
        f_rows.append(f)

    q_all = _head_norm(proj(C_Q, FOX_W), qg_ref[...], g64_ref) * (FOX_HD ** -0.5 * LOG2E)
    k_all = _head_norm(proj(C_K, FOX_W), kg_ref[...], g64_ref)
    v_all = proj(C_V, FOX_W)
    l64 = lax.broadcasted_iota(jnp.int32, (CHUNK, FOX_HD), 1)
    for g in range(grp):
        q, k = seq(q_all, g), seq(k_all, g)
        qa, ka = [], []
        for hh in range(FOX_HEADS):
            y = f_rows[g][:, hh:hh + 1] * LOG2E
            hi = y.astype(BF16).astype(F32)
            mid = (y - hi).astype(BF16).astype(F32)
            lo = y - hi - mid
            parts = jnp.where(l64 == 0, hi, jnp.where(l64 == 1, mid, lo))
            ext_q = jnp.where(l64 < 3, parts, jnp.where(l64 < 6, 1.0, 0.0))
            parts = jnp.where(l64 == 3, hi, jnp.where(l64 == 4, mid, lo))
            ext_k = jnp.where(l64 < 3, 1.0, jnp.where(l64 < 6, -parts, 0.0))
            qa += [q[:, hh * FOX_HD:(hh + 1) * FOX_HD], ext_q]
            ka += [k[:, hh * FOX_HD:(hh + 1) * FOX_HD], ext_k]
        qat_ref[g] = jnp.concatenate(qa, axis=1).T.astype(BF16)
        ka_ref[g] = jnp.concatenate(ka, axis=1).astype(BF16)
        kt_ref[g] = k.T
        vt = seq(v_all, g).T
        vt_ref[g] = vt
        vt16_ref[g] = vt.astype(BF16)

    xbc_all = proj(C_XBC, SSD_CONV_DIM)
    r_end = t_real - (t_real - 1) // CHUNK * CHUNK
    for g in range(grp):
        xb_scr[g, 8:8 + CHUNK, :] = seq(xbc_all, g)
        conv = cb_ref[...]
        for w in range(SSD_CONV):
            conv = conv + xb_scr[g, pl.ds(8 - (SSD_CONV - 1) + w, CHUNK), :] * cw_ref[w:w + 1, :]
        xbc_ref[g] = _silu(conv)
        convp_ref[g] = xb_scr[g, r_end:r_end + 8, :]
        xb_scr[g, 0:8, :] = xb_scr[g, CHUNK:CHUNK + 8, :]

    z_ref[...] = proj(C_Z, SSD_W).reshape(grp, CHUNK, SSD_W)
    gqk_ref[...] = proj(C_GQ, 2 * GLA_KW).reshape(grp, CHUNK, 2 * GLA_KW)
    gv_ref[...] = proj(C_GV, GLA_W).reshape(grp, CHUNK, GLA_W)
    gg_ref[...] = proj(C_GG, GLA_W).reshape(grp, CHUNK, GLA_W)


def _proj_p(x3, nrm, w, qg, kg, fb, dtb, cw, cb, g64, ltri, t_real, l):
    nb, t_pad, _ = x3.shape
    nj = t_pad // CHUNK
    grp = math.gcd(nb, PROJ_GROUP)
    rows = lambda w_: pl.BlockSpec((grp, CHUNK, w_), lambda b, j: (b, j, 0))
    cols = lambda h_: pl.BlockSpec((grp, h_, CHUNK), lambda b, j: (b, 0, j))
    out_shape = [
        jax.ShapeDtypeStruct((nb, FOX_AUG, t_pad), BF16),
        jax.ShapeDtypeStruct((nb, t_pad, FOX_AUG), BF16),
        jax.ShapeDtypeStruct((nb, FOX_W, t_pad), F32),
        jax.ShapeDtypeStruct((nb, FOX_W, t_pad), F32),
        jax.ShapeDtypeStruct((nb, FOX_W, t_pad), BF16),
        jax.ShapeDtypeStruct((nb, t_pad, SSD_W), F32),
        jax.ShapeDtypeStruct((nb, t_pad, SSD_CONV_DIM), F32),
        jax.ShapeDtypeStruct((nb, t_pad, 2 * GLA_KW), F32),
        jax.ShapeDtypeStruct((nb, t_pad, GLA_W), F32),
        jax.ShapeDtypeStruct((nb, t_pad, GLA_W), F32),
        jax.ShapeDtypeStruct((nb, t_pad, LANES), F32),
        jax.ShapeDtypeStruct((nb, 16, t_pad), F32),
        jax.ShapeDtypeStruct((nb, 8, SSD_CONV_DIM), F32),
    ]
    out_specs = [cols(FOX_AUG), rows(FOX_AUG), cols(FOX_W), cols(FOX_W), cols(FOX_W), rows(SSD_W),
                 rows(SSD_CONV_DIM), rows(2 * GLA_KW), rows(GLA_W), rows(GLA_W), rows(LANES), cols(16),
                 pl.BlockSpec((grp, 8, SSD_CONV_DIM), lambda b, j: (b, 0, 0))]
    in_specs = [rows(D_MODEL), _lspec((1, D_MODEL), l), _wspec((D_MODEL, N_PROJ), l),
                _lspec((1, FOX_W), l), _lspec((1, FOX_W), l), _lspec((1, LANES), l),
                _lspec((1, LANES), l), _lspec((8, SSD_CONV_DIM), l), _lspec((1, SSD_CONV_DIM), l),
                _const_spec((FOX_W, FOX_W)), _const_spec((CHUNK, CHUNK))]
    return pl.pallas_call(
        functools.partial(_proj_p_kernel, t_real=t_real, grp=grp), out_shape=out_shape,
        grid=(nb // grp, nj), in_specs=in_specs, out_specs=out_specs,
        scratch_shapes=[pltpu.VMEM((grp, 8 + CHUNK, SSD_CONV_DIM), F32),
                        pltpu.VMEM((grp, 8, LANES), F32)],
        compiler_params=_params(("parallel", "arbitrary")), name="proj_p")(
            x3, nrm, w, qg, kg, fb, dtb, cw, cb, g64, ltri)


def _proj_s_kernel(x_ref, nrm_ref, w_ref, qg_ref, kg_ref, fb_ref, dtb_ref, cw_ref, cb_ref, g64_ref,
                   cst_ref,
                   q_ref, kr_ref, vr_ref, pt_ref, cum_ref, convs_ref,
                   hist_scr, carry_scr, *, n_t):
    t = pl.program_id(0)

    @pl.when(t == 0)
    def _():
        hist_scr[0:SSD_CONV - 1] = cst_ref[...]
        carry_scr[...] = jnp.zeros((LANES, LANES), F32)

    h = _rms(x_ref[...], nrm_ref[...]).astype(BF16)
    p = jnp.dot(h, w_ref[...], preferred_element_type=F32)

    q_ref[0] = _head_norm(p[:, C_Q:C_Q + FOX_W], qg_ref[...], g64_ref) * (FOX_HD ** -0.5)
    k = _head_norm(p[:, C_K:C_K + FOX_W], kg_ref[...], g64_ref)
    v = p[:, C_V:C_V + FOX_W]
    kr_ref[0] = k
    vr_ref[0] = v
    for c in range(FOX_W // LANES):
        pt_ref[0, R_K + c * LANES:R_K + (c + 1) * LANES, :] = k[:, c * LANES:(c + 1) * LANES].T
        pt_ref[0, R_V + c * LANES:R_V + (c + 1) * LANES, :] = v[:, c * LANES:(c + 1) * LANES].T

    hist_scr[pl.ds(SSD_CONV - 1 + t, 1)] = p[:, C_XBC:C_XBC + SSD_CONV_DIM][None]
    conv = cb_ref[...]
    for w in range(SSD_CONV):
        conv = conv + hist_scr[t + w] * cw_ref[w:w + 1, :]
    xbc = _silu(conv)

    @pl.when(t == n_t - 1)
    def _():
        convs_ref[...] = hist_scr[n_t:n_t + SSD_CONV - 1]

    sm = p[:, C_SM:C_SM + LANES]
    lane = lax.broadcasted_iota(jnp.int32, (LANES, LANES), 1)
    logf, dt = _small_block(sm, fb_ref, dtb_ref)
    cum = carry_scr[...] + jnp.where(lane < SM_DT, logf, 0.0)
    carry_scr[...] = cum
    cum_ref[0] = cum
    smc = jnp.where(lane < SM_DT, logf, jnp.where(lane < SM_LR, dt, sm))

    def put_t(r0, val):
        for c in range(val.shape[1] // LANES):
            pt_ref[0, r0 + c * LANES:r0 + (c + 1) * LANES, :] = val[:, c * LANES:(c + 1) * LANES].T

    put_t(R_Z, p[:, C_Z:C_Z + SSD_W])
    put_t(R_XBC, xbc)
    put_t(R_GQ, p[:, C_GQ:C_GQ + GLA_KW] * (GLA_DK ** -0.5))
    put_t(R_GK, p[:, C_GK:C_GK + GLA_KW])
    put_t(R_GV, p[:, C_GV:C_GV + GLA_W])
    put_t(R_GG, p[:, C_GG:C_GG + GLA_W])
    put_t(R_SM, smc)


def _proj_s(x, nrm, w, qg, kg, fb, dtb, cw, cb, g64, conv_state, l):
    n_t = x.shape[0] // LANES
    per_t = lambda a, b: pl.BlockSpec((1, a, b), lambda t: (t, 0, 0))
    out_shape = [
        jax.ShapeDtypeStruct((n_t, LANES, FOX_W), F32),
        jax.ShapeDtypeStruct((n_t, LANES, FOX_W), F32),
        jax.ShapeDtypeStruct((n_t, LANES, FOX_W), F32),
        jax.ShapeDtypeStruct((n_t, N_PT, LANES), F32),
        jax.ShapeDtypeStruct((n_t, LANES, LANES), F32),
        jax.ShapeDtypeStruct((SSD_CONV - 1, LANES, SSD_CONV_DIM), F32),
    ]
    out_specs = [per_t(LANES, FOX_W), per_t(LANES, FOX_W), per_t(LANES, FOX_W), per_t(N_PT, LANES),
                 per_t(LANES, LANES), _const_spec((SSD_CONV - 1, LANES, SSD_CONV_DIM))]
    in_specs = [pl.BlockSpec((LANES, D_MODEL), lambda t: (t, 0)), _lspec((1, D_MODEL), l),
                _wspec((D_MODEL, N_PROJ), l), _lspec((1, FOX_W), l), _lspec((1, FOX_W), l),
                _lspec((1, LANES), l), _lspec((1, LANES), l), _lspec((8, SSD_CONV_DIM), l),
                _lspec((1, SSD_CONV_DIM), l), _const_spec((FOX_W, FOX_W)),
                _lspec((SSD_CONV - 1, LANES, SSD_CONV_DIM), l)]
    return pl.pallas_call(
        functools.partial(_proj_s_kernel, n_t=n_t), out_shape=out_shape, grid=(n_t,),
        in_specs=in_specs, out_specs=out_specs,
        scratch_shapes=[pltpu.VMEM((n_t + SSD_CONV - 1, LANES, SSD_CONV_DIM), F32),
                        pltpu.VMEM((LANES, LANES), F32)],
        compiler_params=_params(("arbitrary",)), name="proj_s")(
            x, nrm, w, qg, kg, fb, dtb, cw, cb, g64, conv_state)


def _fox_p_kernel(qat_ref, ka_ref, vt_ref, o_ref, *, kw, qw, tile0):
    qi = pl.program_id(1) + tile0
    nsub = kw // qw
    n_full = qi // nsub

    def scores(start, width):
        return tuple(jnp.dot(ka_ref[0, pl.ds(start, width), h * LANES:(h + 1) * LANES],
                             qat_ref[0, h * LANES:(h + 1) * LANES, :], preferred_element_type=F32)
                     for h in range(FOX_HEADS))

    def update(ss, start, width, carry, masked):
        prs, stats = [], []
        for h in range(FOX_HEADS):
            m, l, _ = carry[h]
            s = ss[h]
            if masked:
                kpos = lax.broadcasted_iota(jnp.int32, (width, qw), 0) + start
                qpos = lax.broadcasted_iota(jnp.int32, (width, qw), 1) + qi * qw
                s = jnp.where(kpos <= qpos, s, -jnp.inf)
            m_new = jnp.maximum(m, jnp.max(s, axis=0, keepdims=True))
            alpha = jnp.exp2(m - m_new)
            pr = jnp.exp2(s - m_new)
            stats.append((m_new, alpha * l + jnp.sum(pr, axis=0, keepdims=True), alpha))
            prs.append(pr.astype(BF16))
        out = []
        for h in range(FOX_HEADS):
            vt = vt_ref[0, h * FOX_HD:(h + 1) * FOX_HD, pl.ds(start, width)]
            m_new, l, alpha = stats[h]
            out.append((m_new, l, alpha * carry[h][2] + jnp.dot(vt, prs[h], preferred_element_type=F32)))
        return tuple(out)

    init = tuple((jnp.full((1, qw), -jnp.inf, F32), jnp.zeros((1, qw), F32),
                  jnp.zeros((FOX_HD, qw), F32)) for _ in range(FOX_HEADS))

    def body(i, carry):
        at = pl.multiple_of(i * kw, kw)
        return update(scores(at, kw), at, kw, carry, False)

    carry = lax.fori_loop(0, n_full, body, init)
    start = pl.multiple_of(n_full * kw, kw)
    for r in range(nsub):
        @pl.when(lax.rem(qi, nsub) == r)
        def _(r=r):
            width = (r + 1) * qw
            fin = update(scores(start, width), start, width, carry, True)
            o_t = jnp.concatenate([acc / l for (_, l, acc) in fin], axis=0)
            o_ref[0] = o_t.T.astype(BF16)


def _fox_p_tiles(qat, ka, vt16, qw, tile0, n_tiles):
    nb, _, t_pad = qat.shape
    return pl.pallas_call(
        functools.partial(_fox_p_kernel, kw=FOX_KW, qw=qw, tile0=tile0),
        out_shape=jax.ShapeDtypeStruct((nb, n_tiles * qw, FOX_W), BF16), grid=(nb, n_tiles),
        in_specs=[pl.BlockSpec((1, FOX_AUG, qw), lambda b, i: (b, 0, i + tile0)),
                  pl.BlockSpec((1, t_pad, FOX_AUG), lambda b, i: (b, 0, 0)),
                  pl.BlockSpec((1, FOX_W, t_pad), lambda b, i: (b, 0, 0))],
        out_specs=pl.BlockSpec((1, qw, FOX_W), lambda b, i: (b, i, 0)),
        compiler_params=_params(("parallel", "arbitrary")), name="fox_p")(qat, ka, vt16)


def _fox_p(qat, ka, vt16):
    t_pad = qat.shape[2]
    n_wide = t_pad // FOX_QW
    rest = (t_pad - n_wide * FOX_QW) // CHUNK
    parts = []
    if n_wide:
        parts.append(_fox_p_tiles(qat, ka, vt16, FOX_QW, 0, n_wide))
    if rest:
        parts.append(_fox_p_tiles(qat, ka, vt16, CHUNK, n_wide * FOX_QW // CHUNK, rest))
    return parts[0] if len(parts) == 1 else jnp.concatenate(parts, axis=1)


def _fox_s_kernel(tbl_ref, q_ref, kr_ref, vr_ref, cum_ref, lf_ref, mstrict_ref, pgsuf_ref,
                  k_hbm, v_hbm, o_ref, kbuf, vbuf, lfst, sem, *, layer, n_pages, n_t):
    b = pl.program_id(0)
    nb = pl.num_programs(0)
    slot = lax.rem(b, 2)

    def copies(seq, sl):
        out = []
        for pg in range(n_pages):
            page = tbl_ref[seq * n_pages + pg]
            out.append(pltpu.make_async_copy(k_hbm.at[layer, page], kbuf.at[sl, pg], sem.at[0, sl]))
            out.append(pltpu.make_async_copy(v_hbm.at[layer, page], vbuf.at[sl, pg], sem.at[1, sl]))
        return out

    @pl.when(b == 0)
    def _():
        lfst[...] = jnp.zeros(lfst.shape, F32)
        for c in copies(0, 0):
            c.start()

    @pl.when(b + 1 < nb)
    def _():
        for c in copies(b + 1, 1 - slot):
            c.start()

    for pg in range(n_pages):
        page = tbl_ref[b * n_pages + pg]
        lfst[pg * 8:pg * 8 + FOX_HEADS, :] = lf_ref[0, page // 2, pl.ds(lax.rem(page, 2) * FOX_HEADS,
                                                                      FOX_HEADS), :]
    lf = lfst[...]
    within = _dot_hl(lf, mstrict_ref[...])
    tot = jnp.broadcast_to(jnp.sum(lf, axis=-1, keepdims=True), lf.shape)
    dsuf = within + _dot_hl_left(pgsuf_ref[...], tot)

    lane = lax.broadcasted_iota(jnp.int32, (8, FOX_W), 1)
    rowh = lax.broadcasted_iota(jnp.int32, (8, FOX_W), 0)
    headmask = (lane // FOX_HD) == rowh
    qexp, cumcol = [], []
    cum_rows = jnp.concatenate([cum_ref[t, pl.ds(b, 1), :] for t in range(n_t)]
                               + [jnp.zeros((LANES - n_t, LANES), F32)], axis=0)
    cum_t = cum_rows.T[0:8, 0:8]
    for t in range(n_t):
        qrow = q_ref[t, pl.ds(b, 1), :]
        qexp.append(jnp.where(headmask, jnp.broadcast_to(qrow, (8, FOX_W)), 0.0))
        cumcol.append(cum_t[:, t:t + 1])
    qexp = jnp.concatenate(qexp, axis=0).astype(BF16)
    cumcol = jnp.concatenate(cumcol, axis=0)
    rows = 8 * n_t

    kn = jnp.concatenate([kr_ref[t, pl.ds(b, 1), :] for t in range(n_t)]
                         + [jnp.zeros((8 - n_t, FOX_W), F32)], axis=0)
    vn = jnp.concatenate([vr_ref[t, pl.ds(b, 1), :] for t in range(n_t)]
                         + [jnp.zeros((8 - n_t, FOX_W), F32)], axis=0)
    s_new = _dot_nt(qexp, kn) + cumcol - jnp.concatenate([cum_t] * n_t, axis=0)
    tq = lax.broadcasted_iota(jnp.int32, (rows, 8), 0) // 8
    tk = lax.broadcasted_iota(jnp.int32, (rows, 8), 1)
    s_new = jnp.where(tk <= tq, s_new, -jnp.inf)

    for c in copies(b, slot):
        c.wait()

    s_pg = []
    for pg in range(n_pages):
        bias = jnp.concatenate([dsuf[pg * 8:(pg + 1) * 8, :]] * n_t, axis=0) + cumcol
        s_pg.append(_dot(qexp, kbuf[slot, pg]) + bias)
    m = s_pg[0]
    for s in s_pg[1:]:
        m = jnp.maximum(m, s)
    m = jnp.maximum(jnp.max(m, axis=-1, keepdims=True), jnp.max(s_new, axis=-1, keepdims=True))
    p_new = jnp.exp(s_new - m)
    l = jnp.sum(p_new, axis=-1, keepdims=True)
    acc = _dot(p_new, vn)
    lsum = None
    for pg in range(n_pages):
        pr = jnp.exp(s_pg[pg] - m)
        lsum = pr if lsum is None else lsum + pr
        acc = acc + _dot_nt(pr, vbuf[slot, pg])
    l = l + jnp.sum(lsum, axis=-1, keepdims=True)
    o = acc / l
    for t in range(n_t):
        ot = jnp.where(headmask, o[t * 8:(t + 1) * 8, :], 0.0)
        o_ref[t, pl.ds(b, 1), :] = jnp.sum(ot, axis=0, keepdims=True)


def _fox_s(tbl, q, kr, vr, cum, lfpool, mstrict, pgsuf, k_cache, v_cache, layer):
    n_t, nb, _ = q.shape
    n_pages = tbl.shape[0] // nb
    whole = lambda a: pl.BlockSpec(a.shape, lambda b, tbl_: (0,) * a.ndim)
    grid_spec = pltpu.PrefetchScalarGridSpec(
        num_scalar_prefetch=1, grid=(nb,),
        in_specs=[whole(q), whole(kr), whole(vr), whole(cum),
                  pl.BlockSpec((1,) + lfpool.shape[1:], lambda b, tbl_: (layer, 0, 0, 0)),
                  whole(mstrict), whole(pgsuf),
                  pl.BlockSpec(memory_space=pl.ANY), pl.BlockSpec(memory_space=pl.ANY)],
        out_specs=pl.BlockSpec((n_t, nb, FOX_W), lambda b, tbl_: (0, 0, 0)),
        scratch_shapes=[pltpu.VMEM((2, n_pages, FOX_W, LANES), F32),
                        pltpu.VMEM((2, n_pages, FOX_W, LANES), F32),
                        pltpu.VMEM((n_pages * 8, LANES), F32),
                        pltpu.SemaphoreType.DMA((2, 2))])
    return pl.pallas_call(
        functools.partial(_fox_s_kernel, layer=layer, n_pages=n_pages, n_t=n_t),
        out_shape=jax.ShapeDtypeStruct((n_t, nb, FOX_W), F32), grid_spec=grid_spec,
        compiler_params=_params(("arbitrary",)), name="fox_s")(
            tbl, q, kr, vr, cum, lfpool, mstrict, pgsuf, k_cache, v_cache)


def _ssd_p_kernel(z_ref, xbc_ref, smr_ref, smt_ref, arow_ref, acol_ref, ltri_ref, utri_ref, exph_ref,
                  dvec_ref, nrm_ref, o_ref, h_ref, ht_scr, *, grp):
    j = pl.program_id(1)
    nj = pl.num_programs(1)

    @pl.when(j == 0)
    def _():
        ht_scr[...] = jnp.zeros(ht_scr.shape, F32)

    finals = [_ssd_p_chunk(s, z_ref, xbc_ref, smr_ref, smt_ref, arow_ref, acol_ref, ltri_ref, utri_ref,
                           exph_ref, dvec_ref, nrm_ref, o_ref, ht_scr) for s in range(grp)]

    @pl.when(j == nj - 1)
    def _():
        rep = SSD_HEADS // SSD_GROUPS
        for s in range(grp):
            h_t = finals[s].T
            for hh in range(SSD_HEADS):
                g = hh // rep
                h_ref[s, hh] = h_t[hh * SSD_HD:(hh + 1) * SSD_HD, g * SSD_STATE:(g + 1) * SSD_STATE]


def _ssd_p_chunk(s, z_ref, xbc_ref, smr_ref, smt_ref, arow_ref, acol_ref, ltri_ref, utri_ref, exph_ref,
                 dvec_ref, nrm_ref, o_ref, ht_scr):
    smr = smr_ref[s]
    exph = exph_ref[...]
    cs_rows = _dot_hl_left(ltri_ref[...], smr * arow_ref[...])
    cs_t = _dot_hl(smt_ref[s] * acol_ref[...], utri_ref[...])
    dt_x = _dot_hl(smr, exph)
    c_hi, c_lo = _split(cs_rows)
    c_lo2 = (cs_rows - c_hi.astype(F32) - c_lo.astype(F32)).astype(BF16)
    cs_x = (jnp.dot(c_hi, exph, preferred_element_type=F32)
            + jnp.dot(c_lo, exph, preferred_element_type=F32)
            + jnp.dot(c_lo2, exph, preferred_element_type=F32))
    end_x = cs_x[CHUNK - 1:CHUNK, :]
    xbc = xbc_ref[s]
    x = xbc[:, 0:SSD_W]
    b_all = xbc[:, SSD_W:SSD_W + SSD_BC].astype(BF16)
    c_all = xbc[:, SSD_W + SSD_BC:SSD_W + 2 * SSD_BC].astype(BF16)
    xdt = x * dt_x
    xdt16 = xdt.astype(BF16)
    ht = ht_scr[s]
    y_state = jnp.dot(c_all, ht.astype(BF16), preferred_element_type=F32)
    upd = _dot_tn(b_all, xdt * jnp.exp(end_x - cs_x))
    cbs = [_dot_nt(c_all[:, g * SSD_STATE:(g + 1) * SSD_STATE], b_all[:, g * SSD_STATE:(g + 1) * SSD_STATE])
           for g in range(SSD_GROUPS)]

    gmask = (lax.broadcasted_iota(jnp.int32, (SSD_BC, SSD_W), 0) // SSD_STATE
             == lax.broadcasted_iota(jnp.int32, (SSD_BC, SSD_W), 1) // (SSD_W // SSD_GROUPS))
    ht_new = ht * jnp.exp(end_x) + jnp.where(gmask, upd, 0.0)
    ht_scr[s] = ht_new

    tril = (lax.broadcasted_iota(jnp.int32, (CHUNK, CHUNK), 1)
            <= lax.broadcasted_iota(jnp.int32, (CHUNK, CHUNK), 0))
    low = lax.broadcasted_iota(jnp.int32, (CHUNK, LANES), 1) < SSD_HD
    rep = SSD_HEADS // SSD_GROUPS
    pairs = []
    for pr in range(SSD_HEADS // 2):
        xp = xdt16[:, pr * LANES:(pr + 1) * LANES]
        ys = []
        for hh in (2 * pr, 2 * pr + 1):
            col = SM_DT + hh
            lm = jnp.exp(jnp.where(tril, cs_rows[:, col:col + 1] - cs_t[col:col + 1, :], -jnp.inf))
            ys.append(jnp.dot((cbs[hh // rep] * lm).astype(BF16), xp, preferred_element_type=F32))
        pairs.append(jnp.where(low, ys[0], ys[1]))
    y = jnp.concatenate(pairs, axis=1) + y_state * jnp.exp(cs_x)
    y = (y + x * dvec_ref[...]) * _silu(z_ref[s])
    gw = SSD_W // SSD_GROUPS
    for g in range(SSD_GROUPS):
        yg = y[:, g * gw:(g + 1) * gw]
        yn = yg * lax.rsqrt(jnp.mean(yg * yg, axis=-1, keepdims=True) + EPS)
        o_ref[s, :, g * gw:(g + 1) * gw] = (yn * nrm_ref[:, g * gw:(g + 1) * gw]).astype(BF16)
    return ht_new


def _ssd_p(z, xbc, smr, smt, arow, acol, ltri, utri, exph, dvec, nrm, l):
    nb, t_pad, _ = z.shape
    nj = t_pad // CHUNK
    grp = math.gcd(nb, MIX_GROUP)
    rows = lambda w_: pl.BlockSpec((grp, CHUNK, w_), lambda b, j: (b, j, 0))
    return pl.pallas_call(
        functools.partial(_ssd_p_kernel, grp=grp),
        out_shape=[jax.ShapeDtypeStruct((nb, t_pad, SSD_W), BF16),
                   jax.ShapeDtypeStruct((nb, SSD_HEADS, SSD_HD, SSD_STATE), F32)],
        grid=(nb // grp, nj),
        in_specs=[rows(SSD_W), rows(SSD_CONV_DIM), rows(LANES),
                  pl.BlockSpec((grp, 16, CHUNK), lambda b, j: (b, 0, j)),
                  _lspec((1, LANES), l), _lspec((16, 1), l), _const_spec((CHUNK, CHUNK)),
                  _const_spec((CHUNK, CHUNK)), _const_spec((LANES, SSD_W)),
                  _lspec((1, SSD_W), l), _lspec((1, SSD_W), l)],
        out_specs=[rows(SSD_W),
                   pl.BlockSpec((grp, SSD_HEADS, SSD_HD, SSD_STATE), lambda b, j: (b, 0, 0, 0))],
        scratch_shapes=[pltpu.VMEM((grp, SSD_BC, SSD_W), F32)],
        compiler_params=_params(("parallel", "arbitrary")), name="ssd_p")(
            z, xbc, smr, smt, arow, acol, ltri, utri, exph, dvec, nrm)


def _ssd_s_kernel(x_ref, b_ref, c_ref, sm_ref, xg_ref, zg_ref, a_ref, d_ref, nrm_ref, st_ref, *rest,
                  n_t, n_prev):
    if n_prev:
        prev_ref, o_ref, so_ref, ybuf = rest
        for k in range(n_prev):
            so_ref[k, 0] = prev_ref[k, 0]
    else:
        o_ref, so_ref, ybuf = rest
    hh = pl.program_id(0)
    rep = SSD_HEADS // SSD_GROUPS
    gw = SSD_W // SSD_GROUPS
    dt = [sm_ref[t, pl.ds(SM_DT + hh, 1), :] for t in range(n_t)]
    dec = [jnp.exp(dt[t] * a_ref[pl.ds(hh, 1), :]) for t in range(n_t)]

    def body(i, carry):
        for u in range(DEC_UNROLL):
            p = i * DEC_UNROLL + u
            hp = st_ref[0, 0, p]
            for t in range(n_t):
                hp = hp * dec[t] + b_ref[t] * (x_ref[t, pl.ds(p, 1), :] * dt[t])
                ybuf[t, pl.ds(hh * SSD_HD + p, 1), :] = jnp.sum(c_ref[t] * hp, axis=0, keepdims=True)
            so_ref[n_prev, 0, p] = hp
        return carry

    lax.fori_loop(0, SSD_HD // DEC_UNROLL, body, 0)

    @pl.when(lax.rem(hh, rep) == rep - 1)
    def _():
        r0 = pl.multiple_of((hh // rep) * gw, gw)
        for t in range(n_t):
            y = ybuf[t, pl.ds(r0, gw), :] + xg_ref[t] * d_ref[pl.ds(r0, gw), :]
            y = y * _silu(zg_ref[t])
            yn = y * lax.rsqrt(jnp.mean(y * y, axis=0, keepdims=True) + EPS)
            o_ref[t, pl.ds(r0, gw), :] = yn * nrm_ref[pl.ds(r0, gw), :]


def _ssd_s(pt, a_b, d_b, nrm_b, state, layer, prev):
    n_t = pt.shape[0]
    n_prev = 0 if prev is None else prev.shape[0]
    rep = SSD_HEADS // SSD_GROUPS
    gw = SSD_W // SSD_GROUPS
    blk = lambda h_, f: pl.BlockSpec((n_t, h_, LANES), lambda hh: (0, f(hh), 0))
    st_spec = pl.BlockSpec((1, 1, SSD_HD, SSD_STATE, LANES), lambda hh: (layer, hh, 0, 0, 0))
    per_head = lambda n: pl.BlockSpec((n, 1, SSD_HD, SSD_STATE, LANES), lambda hh: (0, hh, 0, 0, 0))
    return pl.pallas_call(
        functools.partial(_ssd_s_kernel, n_t=n_t, n_prev=n_prev),
        out_shape=[jax.ShapeDtypeStruct((n_t, SSD_W, LANES), F32),
                   jax.ShapeDtypeStruct((n_prev + 1, SSD_HEADS, SSD_HD, SSD_STATE, LANES), F32)],
        grid=(SSD_HEADS,),
        in_specs=[blk(SSD_HD, lambda hh: R_XBC // SSD_HD + hh),
                  blk(SSD_STATE, lambda hh: (R_XBC + SSD_W) // SSD_STATE + hh // rep),
                  blk(SSD_STATE, lambda hh: (R_XBC + SSD_W + SSD_BC) // SSD_STATE + hh // rep),
                  blk(LANES, lambda hh: R_SM // LANES),
                  blk(gw, lambda hh: R_XBC // gw + hh // rep),
                  blk(gw, lambda hh: R_Z // gw + hh // rep),
                  _lspec((SSD_HEADS, LANES), layer), _lspec((SSD_W, LANES), layer),
                  _lspec((SSD_W, LANES), layer), st_spec] + ([per_head(n_prev)] if n_prev else []),
        out_specs=[_const_spec((n_t, SSD_W, LANES)), per_head(n_prev + 1)],
        scratch_shapes=[pltpu.VMEM((n_t, SSD_W, LANES), F32)],
        compiler_params=_params(("arbitrary",)), name="ssd_s")(
            pt, pt, pt, pt, pt, pt, a_b, d_b, nrm_b, state, *([prev] if n_prev else []))


def _gla_p_kernel(gqk_ref, gv_ref, gg_ref, smr_ref, wg_ref, gb_ref, bl_ref, be_ref, g64_ref, gn_ref,
                  o_ref, s_ref, st_scr, *, t_real, grp):
    j = pl.program_id(1)
    nj = pl.num_programs(1)

    @pl.when(j == 0)
    def _():
        st_scr[...] = jnp.zeros(st_scr.shape, F32)

    finals = [_gla_p_chunk(s, j, gqk_ref, gv_ref, gg_ref, smr_ref, wg_ref, gb_ref, bl_ref, be_ref,
                           g64_ref, gn_ref, o_ref, st_scr, t_real) for s in range(grp)]

    @pl.when(j == nj - 1)
    def _():
        for s in range(grp):
            s_kv = finals[s].T
            for h in range(GLA_HEADS):
                s_ref[s, h] = s_kv[h * GLA_DK:(h + 1) * GLA_DK, h * GLA_DV:(h + 1) * GLA_DV]


def _gla_p_chunk(s, j, gqk_ref, gv_ref, gg_ref, smr_ref, wg_ref, gb_ref, bl_ref, be_ref, g64_ref,
                 gn_ref, o_ref, st_scr, t_real):
    sub = GLA_SUB
    n_sub = CHUNK // sub
    row = lax.broadcasted_iota(jnp.int32, (CHUNK, GLA_KW), 0)
    valid = j * CHUNK + row < t_real
    glog = _log_sigmoid(_dot(smr_ref[s], wg_ref[...]) + gb_ref[...]) * (1.0 / GLA_TAU)
    glog = jnp.where(valid, glog, 0.0)
    gqk = gqk_ref[s]
    gq = gqk[:, 0:GLA_KW] * (GLA_DK ** -0.5)
    gk = jnp.where(valid, gqk[:, GLA_KW:2 * GLA_KW], 0.0)
    v = gv_ref[s]
    bcl = _dot_hl_left(bl_ref[...], glog)
    tot = _dot_hl_left(be_ref[0], glog)
    mid = _dot_hl_left(be_ref[1], glog)
    qe = (gq * jnp.exp(bcl)).astype(BF16)
    qd = (gq * jnp.exp(bcl - mid)).astype(BF16)
    kp = gk * jnp.exp(mid - bcl)
    kend = (gk * jnp.exp(tot - bcl)).astype(BF16)
    dec = jnp.exp(tot)

    r_k = lax.broadcasted_iota(jnp.int32, (GLA_HEADS * sub, GLA_KW), 0) // sub
    c_k = lax.broadcasted_iota(jnp.int32, (GLA_HEADS * sub, GLA_KW), 1) // GLA_DK
    r_v = lax.broadcasted_iota(jnp.int32, (GLA_HEADS * sub, GLA_W), 0) // sub
    c_v = lax.broadcasted_iota(jnp.int32, (GLA_HEADS * sub, GLA_W), 1) // GLA_DV
    r_s = lax.broadcasted_iota(jnp.int32, (GLA_W, GLA_KW), 0) // GLA_DV
    c_s = lax.broadcasted_iota(jnp.int32, (GLA_W, GLA_KW), 1) // GLA_DK
    causal = (lax.broadcasted_iota(jnp.int32, (sub, GLA_HEADS * sub), 1) % sub
              <= lax.broadcasted_iota(jnp.int32, (sub, GLA_HEADS * sub), 0))
    sls = [slice(i * sub, (i + 1) * sub) for i in range(n_sub)]
    atts, upds = [], []
    for sl in sls:
        kbd = jnp.where(r_k == c_k, jnp.concatenate([kp[sl]] * GLA_HEADS, axis=0), 0.0)
        atts.append(_dot_nt(qd[sl], kbd))
        upds.append(_dot_tn(v[sl], kend[sl]))
    sts = [st_scr[s]]
    for i in range(n_sub):
        sts.append(sts[i] * dec[i * sub:i * sub + 1, :] + jnp.where(r_s == c_s, upds[i], 0.0))
    st_scr[s] = sts[n_sub]
    outs = []
    for i, sl in enumerate(sls):
        vbd = jnp.where(r_v == c_v, jnp.concatenate([v[sl]] * GLA_HEADS, axis=0), 0.0)
        outs.append(_dot(jnp.where(causal, atts[i], 0.0), vbd) + _dot_nt(qe[sl], sts[i]))
    o = jnp.concatenate(outs, axis=0)
    msq = _dot_hl(o * o, g64_ref[...])
    o_ref[s] = (o * lax.rsqrt(msq + EPS) * gn_ref[...] * _silu(gg_ref[s])).astype(BF16)
    return sts[n_sub]


def _gla_p(gqk, gv, gg, smr, wgp, gb, bl, be, g64, gn, t_real, l):
    nb, t_pad, _ = gv.shape
    nj = t_pad // CHUNK
    grp = math.gcd(nb, MIX_GROUP)
    rows = lambda w_: pl.BlockSpec((grp, CHUNK, w_), lambda b, j: (b, j, 0))
    return pl.pallas_call(
        functools.partial(_gla_p_kernel, t_real=t_real, grp=grp),
        out_shape=[jax.ShapeDtypeStruct((nb, t_pad, GLA_W), BF16),
                   jax.ShapeDtypeStruct((nb, GLA_HEADS, GLA_DK, GLA_DV), F32)],
        grid=(nb // grp, nj),
        in_specs=[rows(2 * GLA_KW), rows(GLA_W), rows(GLA_W), rows(LANES),
                  _lspec((LANES, GLA_KW), l), _lspec((1, GLA_KW), l),
                  _const_spec((CHUNK, CHUNK)), _const_spec((2, CHUNK, CHUNK)),
                  _const_spec((GLA_W, GLA_W)), _lspec((1, GLA_W), l)],
        out_specs=[rows(GLA_W),
                   pl.BlockSpec((grp, GLA_HEADS, GLA_DK, GLA_DV), lambda b, j: (b, 0, 0, 0))],
        scratch_shapes=[pltpu.VMEM((grp, GLA_W, GLA_KW), F32)],
        compiler_params=_params(("parallel", "arbitrary")), name="gla_p")(
            gqk, gv, gg, smr, wgp, gb, bl, be, g64, gn)


def _gla_s_kernel(q_ref, k_ref, v_ref, gg_ref, sm_ref, wgt_ref, gb_ref, gn_ref, st_ref,
                  o_ref, so_ref, eg_scr, acc_scr, *, n_t):
    for t in range(n_t):
        glog = _log_sigmoid(_dot(wgt_ref[...], sm_ref[t]) + gb_ref[...]) * (1.0 / GLA_TAU)
        eg_scr[t] = jnp.exp(glog)
        acc_scr[t] = jnp.zeros((GLA_DV, LANES), F32)

    def body(i, carry):
        states = []
        for u in range(DEC_UNROLL):
            kk = i * DEC_UNROLL + u
            s = st_ref[0, 0, kk]
            row = []
            for t in range(n_t):
                s = s * eg_scr[t, pl.ds(kk, 1), :] + k_ref[t, pl.ds(kk, 1), :] * v_ref[t]
                row.append(q_ref[t, pl.ds(kk, 1), :] * s)
            so_ref[0, 0, kk] = s
            states.append(row)
        for t in range(n_t):
            acc = acc_scr[t]
            for u in range(DEC_UNROLL):
                acc = acc + states[u][t]
            acc_scr[t] = acc
        return carry

    lax.fori_loop(0, GLA_DK // DEC_UNROLL, body, 0)
    for t in range(n_t):
        o = acc_scr[t]
        on = o * lax.rsqrt(jnp.mean(o * o, axis=0, keepdims=True) + EPS)
        o_ref[t] = on * gn_ref[...] * _silu(gg_ref[t])


def _gla_s(pt, wgt, gb_b, gn_b, state, layer):
    n_t = pt.shape[0]
    blk = lambda h_, f: pl.BlockSpec((n_t, h_, LANES), lambda hh: (0, f(hh), 0))
    return pl.pallas_call(
        functools.partial(_gla_s_kernel, n_t=n_t),
        out_shape=[jax.ShapeDtypeStruct((n_t, GLA_W, LANES), F32),
                   jax.ShapeDtypeStruct((1, GLA_HEADS, GLA_DK, GLA_DV, LANES), F32)],
        grid=(GLA_HEADS,),
        in_specs=[blk(GLA_DK, lambda hh: R_GQ // GLA_DK + hh),
                  blk(GLA_DK, lambda hh: R_GK // GLA_DK + hh),
                  blk(GLA_DV, lambda hh: R_GV // GLA_DV + hh),
                  blk(GLA_DV, lambda hh: R_GG // GLA_DV + hh),
                  blk(LANES, lambda hh: R_SM // LANES),
                  pl.BlockSpec((None, GLA_DK, LANES), lambda hh: (layer, hh, 0)),
                  pl.BlockSpec((None, GLA_DK, LANES), lambda hh: (layer, hh, 0)),
                  _lspec((GLA_DV, LANES), layer),
                  pl.BlockSpec((1, 1, GLA_DK, GLA_DV, LANES), lambda hh: (layer, hh, 0, 0, 0))],
        out_specs=[blk(GLA_DV, lambda hh: hh),
                   pl.BlockSpec((1, 1, GLA_DK, GLA_DV, LANES), lambda hh: (0, hh, 0, 0, 0))],
        scratch_shapes=[pltpu.VMEM((n_t, GLA_DK, LANES), F32), pltpu.VMEM((n_t, GLA_DV, LANES), F32)],
        compiler_params=_params(("arbitrary",)), name="gla_s")(
            pt, pt, pt, pt, pt, wgt, gb_b, gn_b, state)


def _tri_consts(n_pages):
    i = np.arange(CHUNK)
    ltri = (i[None, :] <= i[:, None]).astype(np.float32)
    utri = ltri.T
    mstrict = (i[:, None] > i[None, :]).astype(np.float32)
    same = (i[:, None] // GLA_SUB) == (i[None, :] // GLA_SUB)
    bl = (same & (i[None, :] <= i[:, None])).astype(np.float32)
    to_mid = i[None, :] <= (i[:, None] // GLA_SUB) * GLA_SUB + GLA_SUB // 2 - 1
    be = np.stack([same, same & to_mid]).astype(np.float32)
    r = np.arange(n_pages * 8)
    pgsuf = ((r[:, None] % 8 == r[None, :] % 8) & (r[None, :] // 8 > r[:, None] // 8)).astype(np.float32)
    h = np.arange(FOX_W)
    g64 = ((h[:, None] // 64) == (h[None, :] // 64)).astype(np.float32) / 64.0
    cw = np.arange(SSD_W)
    exph = (i[:, None] == SM_DT + cw[None, :] // SSD_HD).astype(np.float32)
    c = lambda a: jnp.asarray(a, BF16)
    return dict(ltri=c(ltri), utri=c(utri), mstrict=c(mstrict), bl=c(bl), be=c(be), pgsuf=c(pgsuf),
                g64=c(g64), exph=c(exph))


def kernel(x_prompt, x_sample, cache_fox_k, cache_fox_v, cache_fox_logf, state_ssm, state_conv,
           state_gla, page_table, meta_tokens, ffn1_norm, ffn1_w_in, ffn1_w_out, mix_norm, w_mix_in,
           fox_q_norm, fox_k_norm, fox_f_bias, ssd_conv_w, ssd_conv_b, ssd_dt_bias, ssd_a_log, ssd_d,
           ssd_norm, gla_w_gate, gla_gate_bias, gla_norm, w_mix_out, ffn2_norm, ffn2_w_in, ffn2_w_out):
    nbp, seq, _ = x_prompt.shape
    nbs, n_t, _ = x_sample.shape
    depth = ffn1_norm.shape[0]
    assert nbs == LANES and seq % CHUNK == 0
    t_real = N_META + seq
    t_pad = -(-t_real // CHUNK) * CHUNK
    n_pool, page_size = cache_fox_k.shape[1], cache_fox_k.shape[2]
    assert page_size == LANES and n_pool % 2 == 0
    n_pages = page_table.shape[1]
    consts = _tri_consts(n_pages)

    meta = jnp.broadcast_to(meta_tokens[None], (nbp, N_META, D_MODEL))
    xp = jnp.concatenate([meta, x_prompt, jnp.zeros((nbp, t_pad - t_real, D_MODEL), F32)], axis=1)
    xp = xp.reshape(nbp * t_pad, D_MODEL)
    xs = jnp.transpose(x_sample, (1, 0, 2)).reshape(n_t * nbs, D_MODEL)

    k_cache = jnp.transpose(cache_fox_k, (0, 1, 3, 4, 2)).reshape(depth, n_pool, FOX_W, page_size)
    v_cache = jnp.transpose(cache_fox_v, (0, 1, 3, 4, 2)).reshape(depth, n_pool, FOX_W, page_size)
    lfpool = jnp.transpose(cache_fox_logf, (0, 1, 3, 2)).reshape(depth, n_pool // 2, 8, page_size)
    ssm_in = jnp.transpose(state_ssm, (0, 2, 3, 4, 1))
    gla_in = jnp.transpose(state_gla, (0, 2, 3, 4, 1))
    conv_in = jnp.transpose(state_conv, (0, 2, 1, 3))
    tbl = page_table.reshape(-1).astype(jnp.int32)

    w1i, w1o = ffn1_w_in.astype(BF16), ffn1_w_out.astype(BF16)
    w2i, w2o = ffn2_w_in.astype(BF16), ffn2_w_out.astype(BF16)
    wmo = w_mix_out.astype(BF16)
    col = [0] + list(IN_SPLITS) + [w_mix_in.shape[-1]]
    part = lambda i: w_mix_in[:, :, col[i]:col[i + 1]]
    fq, fk, fv, ff, sz, sxbc, sdt, gq, gk, gv, glr, gg = [part(i) for i in range(len(IN_SIZES))]
    n_small = FOX_HEADS + SSD_HEADS + GLA_RANK
    wproj = jnp.concatenate([fq, fk, fv, sz, sxbc, gq, gk, gv, gg, ff, sdt, glr,
                             jnp.zeros((depth, D_MODEL, LANES - n_small), F32)], axis=-1).astype(BF16)
    n1, n2, nm = ffn1_norm[:, None], ffn2_norm[:, None], mix_norm[:, None]
    qg = jnp.tile(fox_q_norm, (1, FOX_HEADS))[:, None]
    kg = jnp.tile(fox_k_norm, (1, FOX_HEADS))[:, None]
    a = -jnp.exp(ssd_a_log)
    lanes_pad = lambda v, off: jnp.pad(v, ((0, 0), (off, LANES - off - v.shape[1])))[:, None]
    fb, dtb, arow = lanes_pad(fox_f_bias, SM_F), lanes_pad(ssd_dt_bias, SM_DT), lanes_pad(a, SM_DT)
    acol = jnp.pad(a, ((0, 0), (SM_DT, 16 - SM_DT - SSD_HEADS)))[:, :, None]
    cw = jnp.pad(ssd_conv_w, ((0, 0), (0, 8 - SSD_CONV), (0, 0)))
    cb = ssd_conv_b[:, None]
    d_rep = jnp.repeat(ssd_d, SSD_HD, axis=1)
    dvec, snrm = d_rep[:, None], ssd_norm[:, None]
    wgp = jnp.pad(gla_w_gate, ((0, 0), (SM_LR, LANES - SM_LR - GLA_RANK), (0, 0))).astype(BF16)
    gb = gla_gate_bias[:, None]
    gn = jnp.tile(gla_norm, (1, GLA_HEADS))[:, None]
    on_lanes = lambda v: jnp.broadcast_to(v[:, :, None], v.shape + (LANES,))
    a_b, d_b, snrm_b = on_lanes(a), on_lanes(d_rep), on_lanes(ssd_norm)
    wgt = jnp.pad(jnp.transpose(gla_w_gate, (0, 2, 1)),
                  ((0, 0), (0, 0), (SM_LR, LANES - SM_LR - GLA_RANK))).astype(BF16)
    gb_b, gn_b = on_lanes(gla_gate_bias), on_lanes(gla_norm)

    outs_p = [[] for _ in range(6)]
    outs_s = [[] for _ in range(6)]
    ssm_all = None
    for l in range(depth):
        last = l == depth - 1
        xp = _ffn(xp, FFN_TM, (n1, w1i, w1o, l))
        (qat, kaug, kt, vt, vt16, z, xbc, gqk, gvv, ggg, smr, smt, convp) = _proj_p(
            xp.reshape(nbp, t_pad, D_MODEL), nm, wproj, qg, kg, fb, dtb, cw, cb, consts["g64"],
            consts["ltri"], t_real, l)
        fox_o = _fox_p(qat, kaug, vt16)
        ssd_o, ssm_p = _ssd_p(z, xbc, smr, smt, arow, acol, consts["ltri"], consts["utri"],
                              consts["exph"], dvec, snrm, l)
        gla_o, gla_p = _gla_p(gqk, gvv, ggg, smr, wgp, gb, consts["bl"], consts["be"], consts["g64"],
                              gn, t_real, l)
        mix = ("rows", fox_o.reshape(-1, FOX_W), ssd_o.reshape(-1, SSD_W), gla_o.reshape(-1, GLA_W), wmo, l)
        window = (nbp, t_pad, N_META, seq) if last and seq % FFN_TM == 0 else None
        xp = _ffn(xp, FFN_TM, (n2, w2i, w2o, l), mix=mix, window=window)
        outs_p[0].append(kt.reshape(nbp, FOX_HEADS, FOX_HD, t_pad))
        outs_p[1].append(vt.reshape(nbp, FOX_HEADS, FOX_HD, t_pad))
        outs_p[2].append(jnp.transpose(smt[:, 0:FOX_HEADS, :t_real], (0, 2, 1)))
        outs_p[3].append(ssm_p)
        outs_p[4].append(convp[:, 8 - (SSD_CONV - 1):])
        outs_p[5].append(gla_p)

        xs = _ffn(xs, xs.shape[0], (n1, w1i, w1o, l))
        qs, krs, vrs, pt, cum, convs = _proj_s(xs, nm, wproj, qg, kg, fb, dtb, cw, cb, consts["g64"],
                                               conv_in, l)
        fox_os = _fox_s(tbl, qs, krs, vrs, cum, lfpool, consts["mstrict"], consts["pgsuf"],
                        k_cache, v_cache, l)
        ssd_ot, ssm_all = _ssd_s(pt, a_b, d_b, snrm_b, ssm_in, l, ssm_all)
        gla_ot, gla_s = _gla_s(pt, wgt, gb_b, gn_b, gla_in, l)
        xs = _ffn(xs, xs.shape[0], (n2, w2i, w2o, l), mix=("features", fox_os, ssd_ot, gla_ot, wmo, l))
        kts = pt[:, R_K:R_K + FOX_W].reshape(n_t, FOX_HEADS, FOX_HD, nbs)
        vts = pt[:, R_V:R_V + FOX_W].reshape(n_t, FOX_HEADS, FOX_HD, nbs)
        outs_s[0].append(jnp.transpose(kts, (3, 0, 1, 2)))
        outs_s[1].append(jnp.transpose(vts, (3, 0, 1, 2)))
        outs_s[2].append(jnp.transpose(pt[:, R_SM + SM_F:R_SM + SM_F + FOX_HEADS], (2, 0, 1)))
        outs_s[4].append(jnp.transpose(convs, (1, 0, 2)))
        outs_s[5].append(jnp.transpose(gla_s[0], (3, 0, 1, 2)))

    y_prompt = xp if xp.ndim == 3 else xp.reshape(nbp, t_pad, D_MODEL)[:, N_META:t_real]
    y_sample = jnp.transpose(xs.reshape(n_t, nbs, D_MODEL), (1, 0, 2))
    k_p = jnp.transpose(jnp.stack(outs_p[0])[..., :t_real], (0, 1, 4, 2, 3))
    v_p = jnp.transpose(jnp.stack(outs_p[1])[..., :t_real], (0, 1, 4, 2, 3))
    lf_p, ssm_p, conv_p, gla_p = [jnp.stack(a) for a in outs_p[2:]]
    k_s, v_s, lf_s, conv_s, gla_s = [jnp.stack(outs_s[i]) for i in (0, 1, 2, 4, 5)]
    ssm_s = jnp.transpose(ssm_all, (0, 4, 1, 2, 3))
    return (y_prompt, y_sample, k_p, v_p, lf_p, ssm_p, conv_p, gla_p, k_s, v_s, lf_s, ssm_s, conv_s, gla_s)
```

```python
import functools
import math

import jax
import jax.numpy as jnp
import numpy as np
from jax import lax
from jax.experimental import pallas as pl
from jax.experimental.pallas import tpu as pltpu

F32 = jnp.float32
BF16 = jnp.bfloat16

D_MODEL = 1024
N_META = 16
D_FF = 2816
EPS = 1e-6
FOX_HEADS = 4
FOX_HD = 64
FOX_W = FOX_HEADS * FOX_HD
SSD_HEADS = 8
SSD_HD = 64
SSD_W = SSD_HEADS * SSD_HD
SSD_GROUPS = 2
SSD_STATE = 64
SSD_CONV = 4
SSD_BC = SSD_GROUPS * SSD_STATE
SSD_CONV_DIM = SSD_W + 2 * SSD_BC
GLA_HEADS = 4
GLA_DK = 32
GLA_DV = 64
GLA_KW = GLA_HEADS * GLA_DK
GLA_W = GLA_HEADS * GLA_DV
GLA_RANK = 16
GLA_TAU = 16.0
D_MIX = FOX_W + SSD_W + GLA_W
IN_SIZES = (FOX_W, FOX_W, FOX_W, FOX_HEADS, SSD_W, SSD_CONV_DIM, SSD_HEADS,
            GLA_KW, GLA_KW, GLA_W, GLA_RANK, GLA_W)
IN_SPLITS = tuple(int(v) for v in np.cumsum(IN_SIZES)[:-1])

LOG2E = 1.4426950408889634
FOX_AUG = FOX_HEADS * 128

LANES = 128
CHUNK = 128
GLA_SUB = 32
FOX_KW = 512
FOX_QW = 512
FFN_CK = 256
FFN_TM = 512
PROJ_GROUP = 8
DEC_UNROLL = 4
MIX_GROUP = 8
VMEM_LIMIT = 60 * 1024 * 1024

C_Q, C_K, C_V = 0, 256, 512
C_Z = 768
C_XBC = 1280
C_GQ, C_GK, C_GV, C_GG = 2048, 2176, 2304, 2560
C_SM = 2816
N_PROJ = 2944
SM_F = 0
SM_DT = 4
SM_LR = 12
R_K, R_V, R_Z, R_XBC = 0, 256, 512, 1024
R_GQ, R_GK, R_GV, R_GG, R_SM = 1792, 1920, 2048, 2304, 2560
N_PT = 2688


def _dot(a, b):
    return jnp.dot(a.astype(BF16), b.astype(BF16), preferred_element_type=F32)


def _dot_nt(a, b):
    return lax.dot_general(a.astype(BF16), b.astype(BF16), (((1,), (1,)), ((), ())),
                           preferred_element_type=F32)


def _dot_tn(a, b):
    return lax.dot_general(a.astype(BF16), b.astype(BF16), (((0,), (0,)), ((), ())),
                           preferred_element_type=F32)


def _split(a):
    hi = a.astype(BF16)
    lo = (a - hi.astype(F32)).astype(BF16)
    return hi, lo


def _dot_hl(a, m):
    hi, lo = _split(a)
    return (jnp.dot(hi, m, preferred_element_type=F32) + jnp.dot(lo, m, preferred_element_type=F32))


def _dot_hl_left(m, a):
    hi, lo = _split(a)
    return (jnp.dot(m, hi, preferred_element_type=F32) + jnp.dot(m, lo, preferred_element_type=F32))


def _silu(x):
    return x * jax.nn.sigmoid(x)


def _softplus(x):
    return jnp.maximum(x, 0.0) + jnp.log1p(jnp.exp(-jnp.abs(x)))


def _log_sigmoid(x):
    return jnp.minimum(x, 0.0) - jnp.log1p(jnp.exp(-jnp.abs(x)))


def _rms(x, g):
    return x * lax.rsqrt(jnp.mean(x * x, axis=-1, keepdims=True) + EPS) * g


def _const_spec(shape):
    n = len(shape)
    return pl.BlockSpec(shape, lambda *_: (0,) * n)


def _lspec(shape, l):
    n = len(shape)
    return pl.BlockSpec((None,) + tuple(shape), lambda *_: (l,) + (0,) * n)


def _wspec(shape, l):
    n = len(shape)
    return pl.BlockSpec((None,) + tuple(shape), lambda *_: (l,) + (0,) * n, pipeline_mode=pl.Buffered(1))


def _params(sem):
    return pltpu.CompilerParams(dimension_semantics=sem, vmem_limit_bytes=VMEM_LIMIT)


def _swiglu_half(x, g_ref, win_ref, wout_ref):
    h = _rms(x, g_ref[...]).astype(BF16)
    acc = None
    for c in range(D_FF // FFN_CK):
        g = jnp.dot(h, win_ref[:, c * FFN_CK:(c + 1) * FFN_CK], preferred_element_type=F32)
        u = jnp.dot(h, win_ref[:, D_FF + c * FFN_CK:D_FF + (c + 1) * FFN_CK],
                    preferred_element_type=F32)
        a = (_silu(g) * u).astype(BF16)
        part = jnp.dot(a, wout_ref[c * FFN_CK:(c + 1) * FFN_CK, :], preferred_element_type=F32)
        acc = part if acc is None else acc + part
    return x + 0.5 * acc


def _ffn_kernel(*refs, mix, lead):
    rows = (lambda r: r[0]) if lead else (lambda r: r[...])
    x = rows(refs[0])
    if mix:
        fo_ref, so_ref, go_ref, wmo_ref = refs[1:5]
        if mix == "rows":
            fo, so, go = rows(fo_ref), rows(so_ref), rows(go_ref)
        else:
            n_t = fo_ref.shape[0]
            fo = fo_ref[...].reshape(n_t * LANES, FOX_W)
            so = jnp.concatenate([so_ref[t].T for t in range(n_t)], axis=0)
            go = jnp.concatenate([go_ref[t].T for t in range(n_t)], axis=0)
        x = x + _dot(fo, wmo_ref[0:FOX_W, :])
        x = x + _dot(so, wmo_ref[FOX_W:FOX_W + SSD_W, :])
        x = x + _dot(go, wmo_ref[FOX_W + SSD_W:D_MIX, :])
    g_ref, win_ref, wout_ref, o_ref = refs[-4:]
    y = _swiglu_half(x, g_ref, win_ref, wout_ref)
    if len(o_ref.shape) == 3:
        o_ref[0] = y
    else:
        o_ref[...] = y


def _ffn_into(x3, dest, tm, ffn, row0):
    nseq, t_in, _ = x3.shape
    g, win, wout, l = ffn
    out_spec = pl.BlockSpec((pl.Element(1), pl.Element(tm), pl.Element(D_MODEL)),
                            lambda b, i: (b, pl.multiple_of(row0 + i * tm, math.gcd(row0, tm)), 0))
    return pl.pallas_call(
        functools.partial(_ffn_kernel, mix=None, lead=True),
        out_shape=jax.ShapeDtypeStruct(dest.shape, F32), grid=(nseq, t_in // tm),
        in_specs=[pl.BlockSpec((1, tm, D_MODEL), lambda b, i: (b, i, 0)), pl.BlockSpec(memory_space=pl.ANY),
                  _lspec((1, D_MODEL), l), _wspec((D_MODEL, 2 * D_FF), l), _wspec((D_FF, D_MODEL), l)],
        out_specs=out_spec, input_output_aliases={1: 0},
        compiler_params=_params(("parallel", "parallel")), name="ffn")(x3, dest, g, win, wout)


def _ffn(x, tm, ffn, mix=None, window=None):
    if window:
        nseq, t_in, row0, t_out = window
        grid = (nseq, t_out // tm)
        view = lambda a: a.reshape(nseq, t_in, a.shape[-1])
        row = lambda w: pl.BlockSpec((pl.Element(1), pl.Element(tm), pl.Element(w)),
                                     lambda b, i: (b, pl.multiple_of(row0 + i * tm, math.gcd(row0, tm)), 0))
        out_shape = jax.ShapeDtypeStruct((nseq, t_out, D_MODEL), F32)
        out_spec = pl.BlockSpec((None, tm, D_MODEL), lambda b, i: (b, i, 0))
    else:
        rows = x.shape[0]
        grid = (rows // tm,)
        view = lambda a: a
        row = lambda w: pl.BlockSpec((tm, w), lambda i: (i, 0))
        out_shape = jax.ShapeDtypeStruct((rows, D_MODEL), F32)
        out_spec = row(D_MODEL)
    args, specs = [view(x)], [row(D_MODEL)]
    if mix:
        kind, fo, so, go, wmo, l = mix
        if kind == "rows":
            args += [view(fo), view(so), view(go)]
            specs += [row(FOX_W), row(SSD_W), row(GLA_W)]
        else:
            assert not window and tm == x.shape[0]
            args += [fo, so, go]
            specs += [pl.BlockSpec(a.shape, lambda i: (0, 0, 0)) for a in (fo, so, go)]
        args.append(wmo)
        specs.append(_wspec((D_MIX, D_MODEL), l))
    g, win, wout, l = ffn
    args += [g, win, wout]
    specs += [_lspec((1, D_MODEL), l), _wspec((D_MODEL, 2 * D_FF), l), _wspec((D_FF, D_MODEL), l)]
    return pl.pallas_call(
        functools.partial(_ffn_kernel, mix=mix[0] if mix else None, lead=bool(window)),
        out_shape=out_shape, grid=grid,
        in_specs=specs, out_specs=out_spec,
        compiler_params=_params(("parallel",) * len(grid)), name="mix_ffn" if mix else "ffn")(*args)


def _head_norm(x, gain_row, g64_ref):
    msq = _dot_hl(x * x, g64_ref[...])
    return x * lax.rsqrt(msq + EPS) * gain_row


def _small_block(sm, fb_ref, dtb_ref):
    logf = _log_sigmoid(sm + fb_ref[...])
    dt = _softplus(sm + dtb_ref[...])
    return logf, dt


def _proj_p_kernel(x_ref, nrm_ref, w_ref, qg_ref, kg_ref, fb_ref, dtb_ref, cw_ref, cb_ref, g64_ref,
                   ltri_ref,
                   qat_ref, ka_ref, kt_ref, vt_ref, vt16_ref, z_ref, xbc_ref, gqk_ref, gv_ref, gg_ref,
                   smr_ref, smt_ref, convp_ref,
                   xb_scr, carry_scr, *, t_real, grp):
    j = pl.program_id(1)

    @pl.when(j == 0)
    def _():
        xb_scr[:, 0:8, :] = jnp.zeros((grp, 8, SSD_CONV_DIM), F32)
        carry_scr[...] = jnp.zeros((grp, 8, LANES), F32)

    h = _rms(x_ref[...].reshape(grp * CHUNK, D_MODEL), nrm_ref[...]).astype(BF16)
    proj = lambda c0, n: jnp.dot(h, w_ref[:, c0:c0 + n], preferred_element_type=F32)
    seq = lambda a, g: a[g * CHUNK:(g + 1) * CHUNK]

    sm_all = proj(C_SM, LANES)
    lane = lax.broadcasted_iota(jnp.int32, (CHUNK, LANES), 1)
    row = lax.broadcasted_iota(jnp.int32, (CHUNK, LANES), 0)
    logf_all, dt_all = _small_block(sm_all, fb_ref, dtb_ref)
    f_rows = []
    for g in range(grp):
        logf = jnp.where(lane < SM_DT, seq(logf_all, g), 0.0)
        dt = jnp.where(j * CHUNK + row < t_real, seq(dt_all, g), 0.0)
        f = _dot_hl_left(ltri_ref[...], logf) + carry_scr[g, 0:1, :]
        carry_scr[g] = jnp.broadcast_to(f[CHUNK - 1:CHUNK, :], (8, LANES))
        smr_ref[g] = jnp.where(lane < SM_LR, dt, seq(sm_all, g))
        smt_ref[g] = jnp.where(lane < SM_DT, logf, dt).T[0:16, :]
        f_rows.append(f)

    q_all = _head_norm(proj(C_Q, FOX_W), qg_ref[...], g64_ref) * (FOX_HD ** -0.5 * LOG2E)
    k_all = _head_norm(proj(C_K, FOX_W), kg_ref[...], g64_ref)
    v_all = proj(C_V, FOX_W)
    l64 = lax.broadcasted_iota(jnp.int32, (CHUNK, FOX_HD), 1)
    for g in range(grp):
        q, k = seq(q_all, g), seq(k_all, g)
        qa, ka = [], []
        for hh in range(FOX_HEADS):
            y = f_rows[g][:, hh:hh + 1] * LOG2E
            hi = y.astype(BF16).astype(F32)
            mid = (y - hi).astype(BF16).astype(F32)
            lo = y - hi - mid
            parts = jnp.where(l64 == 0, hi, jnp.where(l64 == 1, mid, lo))
            ext_q = jnp.where(l64 < 3, parts, jnp.where(l64 < 6, 1.0, 0.0))
            parts = jnp.where(l64 == 3, hi, jnp.where(l64 == 4, mid, lo))
            ext_k = jnp.where(l64 < 3, 1.0, jnp.where(l64 < 6, -parts, 0.0))
            qa += [q[:, hh * FOX_HD:(hh + 1) * FOX_HD], ext_q]
            ka += [k[:, hh * FOX_HD:(hh + 1) * FOX_HD], ext_k]
        qat_ref[g] = jnp.concatenate(qa, axis=1).T.astype(BF16)
        ka_ref[g] = jnp.concatenate(ka, axis=1).astype(BF16)
        kt_ref[g] = k.T
        vt = seq(v_all, g).T
        vt_ref[g] = vt
        vt16_ref[g] = vt.astype(BF16)

    xbc_all = proj(C_XBC, SSD_CONV_DIM)
    r_end = t_real - (t_real - 1) // CHUNK * CHUNK
    for g in range(grp):
        xb_scr[g, 8:8 + CHUNK, :] = seq(xbc_all, g)
        conv = cb_ref[...]
        for w in range(SSD_CONV):
            conv = conv + xb_scr[g, pl.ds(8 - (SSD_CONV - 1) + w, CHUNK), :] * cw_ref[w:w + 1, :]
        xbc_ref[g] = _silu(conv)
        convp_ref[g] = xb_scr[g, r_end:r_end + 8, :]
        xb_scr[g, 0:8, :] = xb_scr[g, CHUNK:CHUNK + 8, :]

    z_ref[...] = proj(C_Z, SSD_W).reshape(grp, CHUNK, SSD_W)
    gqk_ref[...] = proj(C_GQ, 2 * GLA_KW).reshape(grp, CHUNK, 2 * GLA_KW)
    gv_ref[...] = proj(C_GV, GLA_W).reshape(grp, CHUNK, GLA_W)
    gg_ref[...] = proj(C_GG, GLA_W).reshape(grp, CHUNK, GLA_W)


def _proj_p(x3, nrm, w, qg, kg, fb, dtb, cw, cb, g64, ltri, t_real, l):
    nb, t_pad, _ = x3.shape
    nj = t_pad // CHUNK
    grp = math.gcd(nb, PROJ_GROUP)
    rows = lambda w_: pl.BlockSpec((grp, CHUNK, w_), lambda b, j: (b, j, 0))
    cols = lambda h_: pl.BlockSpec((grp, h_, CHUNK), lambda b, j: (b, 0, j))
    out_shape = [
        jax.ShapeDtypeStruct((nb, FOX_AUG, t_pad), BF16),
        jax.ShapeDtypeStruct((nb, t_pad, FOX_AUG), BF16),
        jax.ShapeDtypeStruct((nb, FOX_W, t_pad), F32),
        jax.ShapeDtypeStruct((nb, FOX_W, t_pad), F32),
        jax.ShapeDtypeStruct((nb, FOX_W, t_pad), BF16),
        jax.ShapeDtypeStruct((nb, t_pad, SSD_W), F32),
        jax.ShapeDtypeStruct((nb, t_pad, SSD_CONV_DIM), F32),
        jax.ShapeDtypeStruct((nb, t_pad, 2 * GLA_KW), F32),
        jax.ShapeDtypeStruct((nb, t_pad, GLA_W), F32),
        jax.ShapeDtypeStruct((nb, t_pad, GLA_W), F32),
        jax.ShapeDtypeStruct((nb, t_pad, LANES), F32),
        jax.ShapeDtypeStruct((nb, 16, t_pad), F32),
        jax.ShapeDtypeStruct((nb, 8, SSD_CONV_DIM), F32),
    ]
    out_specs = [cols(FOX_AUG), rows(FOX_AUG), cols(FOX_W), cols(FOX_W), cols(FOX_W), rows(SSD_W),
                 rows(SSD_CONV_DIM), rows(2 * GLA_KW), rows(GLA_W), rows(GLA_W), rows(LANES), cols(16),
                 pl.BlockSpec((grp, 8, SSD_CONV_DIM), lambda b, j: (b, 0, 0))]
    in_specs = [rows(D_MODEL), _lspec((1, D_MODEL), l), _wspec((D_MODEL, N_PROJ), l),
                _lspec((1, FOX_W), l), _lspec((1, FOX_W), l), _lspec((1, LANES), l),
                _lspec((1, LANES), l), _lspec((8, SSD_CONV_DIM), l), _lspec((1, SSD_CONV_DIM), l),
                _const_spec((FOX_W, FOX_W)), _const_spec((CHUNK, CHUNK))]
    return pl.pallas_call(
        functools.partial(_proj_p_kernel, t_real=t_real, grp=grp), out_shape=out_shape,
        grid=(nb // grp, nj), in_specs=in_specs, out_specs=out_specs,
        scratch_shapes=[pltpu.VMEM((grp, 8 + CHUNK, SSD_CONV_DIM), F32),
                        pltpu.VMEM((grp, 8, LANES), F32)],
        compiler_params=_params(("parallel", "arbitrary")), name="proj_p")(
            x3, nrm, w, qg, kg, fb, dtb, cw, cb, g64, ltri)


def _proj_s_kernel(x_ref, nrm_ref, w_ref, qg_ref, kg_ref, fb_ref, dtb_ref, cw_ref, cb_ref, g64_ref,
                   cst_ref,
                   q_ref, kr_ref, vr_ref, pt_ref, cum_ref, convs_ref,
                   hist_scr, carry_scr, *, n_t):
    t = pl.program_id(0)

    @pl.when(t == 0)
    def _():
        hist_scr[0:SSD_CONV - 1] = cst_ref[...]
        carry_scr[...] = jnp.zeros((LANES, LANES), F32)

    h = _rms(x_ref[...], nrm_ref[...]).astype(BF16)
    p = jnp.dot(h, w_ref[...], preferred_element_type=F32)

    q_ref[0] = _head_norm(p[:, C_Q:C_Q + FOX_W], qg_ref[...], g64_ref) * (FOX_HD ** -0.5)
    k = _head_norm(p[:, C_K:C_K + FOX_W], kg_ref[...], g64_ref)
    v = p[:, C_V:C_V + FOX_W]
    kr_ref[0] = k
    vr_ref[0] = v
    for c in range(FOX_W // LANES):
        pt_ref[0, R_K + c * LANES:R_K + (c + 1) * LANES, :] = k[:, c * LANES:(c + 1) * LANES].T
        pt_ref[0, R_V + c * LANES:R_V + (c + 1) * LANES, :] = v[:, c * LANES:(c + 1) * LANES].T

    hist_scr[pl.ds(SSD_CONV - 1 + t, 1)] = p[:, C_XBC:C_XBC + SSD_CONV_DIM][None]
    conv = cb_ref[...]
    for w in range(SSD_CONV):
        conv = conv + hist_scr[t + w] * cw_ref[w:w + 1, :]
    xbc = _silu(conv)

    @pl.when(t == n_t - 1)
    def _():
        convs_ref[...] = hist_scr[n_t:n_t + SSD_CONV - 1]

    sm = p[:, C_SM:C_SM + LANES]
    lane = lax.broadcasted_iota(jnp.int32, (LANES, LANES), 1)
    logf, dt = _small_block(sm, fb_ref, dtb_ref)
    cum = carry_scr[...] + jnp.where(lane < SM_DT, logf, 0.0)
    carry_scr[...] = cum
    cum_ref[0] = cum
    smc = jnp.where(lane < SM_DT, logf, jnp.where(lane < SM_LR, dt, sm))

    def put_t(r0, val):
        for c in range(val.shape[1] // LANES):
            pt_ref[0, r0 + c * LANES:r0 + (c + 1) * LANES, :] = val[:, c * LANES:(c + 1) * LANES].T

    put_t(R_Z, p[:, C_Z:C_Z + SSD_W])
    put_t(R_XBC, xbc)
    put_t(R_GQ, p[:, C_GQ:C_GQ + GLA_KW] * (GLA_DK ** -0.5))
    put_t(R_GK, p[:, C_GK:C_GK + GLA_KW])
    put_t(R_GV, p[:, C_GV:C_GV + GLA_W])
    put_t(R_GG, p[:, C_GG:C_GG + GLA_W])
    put_t(R_SM, smc)


def _proj_s(x, nrm, w, qg, kg, fb, dtb, cw, cb, g64, conv_state, l):
    n_t = x.shape[0] // LANES
    per_t = lambda a, b: pl.BlockSpec((1, a, b), lambda t: (t, 0, 0))
    out_shape = [
        jax.ShapeDtypeStruct((n_t, LANES, FOX_W), F32),
        jax.ShapeDtypeStruct((n_t, LANES, FOX_W), F32),
        jax.ShapeDtypeStruct((n_t, LANES, FOX_W), F32),
        jax.ShapeDtypeStruct((n_t, N_PT, LANES), F32),
        jax.ShapeDtypeStruct((n_t, LANES, LANES), F32),
        jax.ShapeDtypeStruct((SSD_CONV - 1, LANES, SSD_CONV_DIM), F32),
    ]
    out_specs = [per_t(LANES, FOX_W), per_t(LANES, FOX_W), per_t(LANES, FOX_W), per_t(N_PT, LANES),
                 per_t(LANES, LANES), _const_spec((SSD_CONV - 1, LANES, SSD_CONV_DIM))]
    in_specs = [pl.BlockSpec((LANES, D_MODEL), lambda t: (t, 0)), _lspec((1, D_MODEL), l),
                _wspec((D_MODEL, N_PROJ), l), _lspec((1, FOX_W), l), _lspec((1, FOX_W), l),
                _lspec((1, LANES), l), _lspec((1, LANES), l), _lspec((8, SSD_CONV_DIM), l),
                _lspec((1, SSD_CONV_DIM), l), _const_spec((FOX_W, FOX_W)),
                _lspec((SSD_CONV - 1, LANES, SSD_CONV_DIM), l)]
    return pl.pallas_call(
        functools.partial(_proj_s_kernel, n_t=n_t), out_shape=out_shape, grid=(n_t,),
        in_specs=in_specs, out_specs=out_specs,
        scratch_shapes=[pltpu.VMEM((n_t + SSD_CONV - 1, LANES, SSD_CONV_DIM), F32),
                        pltpu.VMEM((LANES, LANES), F32)],
        compiler_params=_params(("arbitrary",)), name="proj_s")(
            x, nrm, w, qg, kg, fb, dtb, cw, cb, g64, conv_state)


def _fox_p_kernel(qat_ref, ka_ref, vt_ref, o_ref, *, kw, qw, tile0):
    qi = pl.program_id(1) + tile0
    nsub = kw // qw
    n_full = qi // nsub

    def scores(start, width):
        return tuple(jnp.dot(ka_ref[0, pl.ds(start, width), h * LANES:(h + 1) * LANES],
                             qat_ref[0, h * LANES:(h + 1) * LANES, :], preferred_element_type=F32)
                     for h in range(FOX_HEADS))

    def update(ss, start, width, carry, masked):
        prs, stats = [], []
        for h in range(FOX_HEADS):
            m, l, _ = carry[h]
            s = ss[h]
            if masked:
                kpos = lax.broadcasted_iota(jnp.int32, (width, qw), 0) + start
                qpos = lax.broadcasted_iota(jnp.int32, (width, qw), 1) + qi * qw
                s = jnp.where(kpos <= qpos, s, -jnp.inf)
            m_new = jnp.maximum(m, jnp.max(s, axis=0, keepdims=True))
            alpha = jnp.exp2(m - m_new)
            pr = jnp.exp2(s - m_new)
            stats.append((m_new, alpha * l + jnp.sum(pr, axis=0, keepdims=True), alpha))
            prs.append(pr.astype(BF16))
        out = []
        for h in range(FOX_HEADS):
            vt = vt_ref[0, h * FOX_HD:(h + 1) * FOX_HD, pl.ds(start, width)]
            m_new, l, alpha = stats[h]
            out.append((m_new, l, alpha * carry[h][2] + jnp.dot(vt, prs[h], preferred_element_type=F32)))
        return tuple(out)

    init = tuple((jnp.full((1, qw), -jnp.inf, F32), jnp.zeros((1, qw), F32),
                  jnp.zeros((FOX_HD, qw), F32)) for _ in range(FOX_HEADS))

    def body(i, carry):
        at = pl.multiple_of(i * kw, kw)
        return update(scores(at, kw), at, kw, carry, False)

    carry = lax.fori_loop(0, n_full, body, init)
    start = pl.multiple_of(n_full * kw, kw)
    for r in range(nsub):
        @pl.when(lax.rem(qi, nsub) == r)
        def _(r=r):
            width = (r + 1) * qw
            fin = update(scores(start, width), start, width, carry, True)
            o_t = jnp.concatenate([acc / l for (_, l, acc) in fin], axis=0)
            o_ref[0] = o_t.T.astype(BF16)


def _fox_p_tiles(qat, ka, vt16, qw, tile0, n_tiles):
    nb, _, t_pad = qat.shape
    return pl.pallas_call(
        functools.partial(_fox_p_kernel, kw=FOX_KW, qw=qw, tile0=tile0),
        out_shape=jax.ShapeDtypeStruct((nb, n_tiles * qw, FOX_W), BF16), grid=(nb, n_tiles),
        in_specs=[pl.BlockSpec((1, FOX_AUG, qw), lambda b, i: (b, 0, i + tile0)),
                  pl.BlockSpec((1, t_pad, FOX_AUG), lambda b, i: (b, 0, 0)),
                  pl.BlockSpec((1, FOX_W, t_pad), lambda b, i: (b, 0, 0))],
        out_specs=pl.BlockSpec((1, qw, FOX_W), lambda b, i: (b, i, 0)),
        compiler_params=_params(("parallel", "arbitrary")), name="fox_p")(qat, ka, vt16)


def _fox_p(qat, ka, vt16):
    t_pad = qat.shape[2]
    n_wide = t_pad // FOX_QW
    rest = (t_pad - n_wide * FOX_QW) // CHUNK
    parts = []
    if n_wide:
        parts.append(_fox_p_tiles(qat, ka, vt16, FOX_QW, 0, n_wide))
    if rest:
        parts.append(_fox_p_tiles(qat, ka, vt16, CHUNK, n_wide * FOX_QW // CHUNK, rest))
    return parts[0] if len(parts) == 1 else jnp.concatenate(parts, axis=1)


def _fox_s_kernel(tbl_ref, q_ref, kr_ref, vr_ref, cum_ref, lf_ref, mstrict_ref, pgsuf_ref,
                  k_hbm, v_hbm, o_ref, kbuf, vbuf, lfst, sem, *, layer, n_pages, n_t):
    b = pl.program_id(0)
    nb = pl.num_programs(0)
    slot = lax.rem(b, 2)

    def copies(seq, sl):
        out = []
        for pg in range(n_pages):
            page = tbl_ref[seq * n_pages + pg]
            out.append(pltpu.make_async_copy(k_hbm.at[layer, page], kbuf.at[sl, pg], sem.at[0, sl]))
            out.append(pltpu.make_async_copy(v_hbm.at[layer, page], vbuf.at[sl, pg], sem.at[1, sl]))
        return out

    @pl.when(b == 0)
    def _():
        lfst[...] = jnp.zeros(lfst.shape, F32)
        for c in copies(0, 0):
            c.start()

    @pl.when(b + 1 < nb)
    def _():
        for c in copies(b + 1, 1 - slot):
            c.start()

    for pg in range(n_pages):
        page = tbl_ref[b * n_pages + pg]
        lfst[pg * 8:pg * 8 + FOX_HEADS, :] = lf_ref[0, page // 2, pl.ds(lax.rem(page, 2) * FOX_HEADS,
                                                                      FOX_HEADS), :]
    lf = lfst[...]
    within = _dot_hl(lf, mstrict_ref[...])
    tot = jnp.broadcast_to(jnp.sum(lf, axis=-1, keepdims=True), lf.shape)
    dsuf = within + _dot_hl_left(pgsuf_ref[...], tot)

    lane = lax.broadcasted_iota(jnp.int32, (8, FOX_W), 1)
    rowh = lax.broadcasted_iota(jnp.int32, (8, FOX_W), 0)
    headmask = (lane // FOX_HD) == rowh
    qexp, cumcol = [], []
    cum_rows = jnp.concatenate([cum_ref[t, pl.ds(b, 1), :] for t in range(n_t)]
                               + [jnp.zeros((LANES - n_t, LANES), F32)], axis=0)
    cum_t = cum_rows.T[0:8, 0:8]
    for t in range(n_t):
        qrow = q_ref[t, pl.ds(b, 1), :]
        qexp.append(jnp.where(headmask, jnp.broadcast_to(qrow, (8, FOX_W)), 0.0))
        cumcol.append(cum_t[:, t:t + 1])
    qexp = jnp.concatenate(qexp, axis=0).astype(BF16)
    cumcol = jnp.concatenate(cumcol, axis=0)
    rows = 8 * n_t

    kn = jnp.concatenate([kr_ref[t, pl.ds(b, 1), :] for t in range(n_t)]
                         + [jnp.zeros((8 - n_t, FOX_W), F32)], axis=0)
    vn = jnp.concatenate([vr_ref[t, pl.ds(b, 1), :] for t in range(n_t)]
                         + [jnp.zeros((8 - n_t, FOX_W), F32)], axis=0)
    s_new = _dot_nt(qexp, kn) + cumcol - jnp.concatenate([cum_t] * n_t, axis=0)
    tq = lax.broadcasted_iota(jnp.int32, (rows, 8), 0) // 8
    tk = lax.broadcasted_iota(jnp.int32, (rows, 8), 1)
    s_new = jnp.where(tk <= tq, s_new, -jnp.inf)

    for c in copies(b, slot):
        c.wait()

    s_pg = []
    for pg in range(n_pages):
        bias = jnp.concatenate([dsuf[pg * 8:(pg + 1) * 8, :]] * n_t, axis=0) + cumcol
        s_pg.append(_dot(qexp, kbuf[slot, pg]) + bias)
    m = s_pg[0]
    for s in s_pg[1:]:
        m = jnp.maximum(m, s)
    m = jnp.maximum(jnp.max(m, axis=-1, keepdims=True), jnp.max(s_new, axis=-1, keepdims=True))
    p_new = jnp.exp(s_new - m)
    l = jnp.sum(p_new, axis=-1, keepdims=True)
    acc = _dot(p_new, vn)
    lsum = None
    for pg in range(n_pages):
        pr = jnp.exp(s_pg[pg] - m)
        lsum = pr if lsum is None else lsum + pr
        acc = acc + _dot_nt(pr, vbuf[slot, pg])
    l = l + jnp.sum(lsum, axis=-1, keepdims=True)
    o = acc / l
    for t in range(n_t):
        ot = jnp.where(headmask, o[t * 8:(t + 1) * 8, :], 0.0)
        o_ref[t, pl.ds(b, 1), :] = jnp.sum(ot, axis=0, keepdims=True)


def _fox_s(tbl, q, kr, vr, cum, lfpool, mstrict, pgsuf, k_cache, v_cache, layer):
    n_t, nb, _ = q.shape
    n_pages = tbl.shape[0] // nb
    whole = lambda a: pl.BlockSpec(a.shape, lambda b, tbl_: (0,) * a.ndim)
    grid_spec = pltpu.PrefetchScalarGridSpec(
        num_scalar_prefetch=1, grid=(nb,),
        in_specs=[whole(q), whole(kr), whole(vr), whole(cum),
                  pl.BlockSpec((1,) + lfpool.shape[1:], lambda b, tbl_: (layer, 0, 0, 0)),
                  whole(mstrict), whole(pgsuf),
                  pl.BlockSpec(memory_space=pl.ANY), pl.BlockSpec(memory_space=pl.ANY)],
        out_specs=pl.BlockSpec((n_t, nb, FOX_W), lambda b, tbl_: (0, 0, 0)),
        scratch_shapes=[pltpu.VMEM((2, n_pages, FOX_W, LANES), F32),
                        pltpu.VMEM((2, n_pages, FOX_W, LANES), F32),
                        pltpu.VMEM((n_pages * 8, LANES), F32),
                        pltpu.SemaphoreType.DMA((2, 2))])
    return pl.pallas_call(
        functools.partial(_fox_s_kernel, layer=layer, n_pages=n_pages, n_t=n_t),
        out_shape=jax.ShapeDtypeStruct((n_t, nb, FOX_W), F32), grid_spec=grid_spec,
        compiler_params=_params(("arbitrary",)), name="fox_s")(
            tbl, q, kr, vr, cum, lfpool, mstrict, pgsuf, k_cache, v_cache)


def _ssd_p_kernel(z_ref, xbc_ref, smr_ref, smt_ref, arow_ref, acol_ref, ltri_ref, utri_ref, exph_ref,
                  dvec_ref, nrm_ref, o_ref, h_ref, ht_scr, *, grp):
    j = pl.program_id(1)
    nj = pl.num_programs(1)

    @pl.when(j == 0)
    def _():
        ht_scr[...] = jnp.zeros(ht_scr.shape, F32)

    finals = [_ssd_p_chunk(s, z_ref, xbc_ref, smr_ref, smt_ref, arow_ref, acol_ref, ltri_ref, utri_ref,
                           exph_ref, dvec_ref, nrm_ref, o_ref, ht_scr) for s in range(grp)]

    @pl.when(j == nj - 1)
    def _():
        rep = SSD_HEADS // SSD_GROUPS
        for s in range(grp):
            h_t = finals[s].T
            for hh in range(SSD_HEADS):
                g = hh // rep
                h_ref[s, hh] = h_t[hh * SSD_HD:(hh + 1) * SSD_HD, g * SSD_STATE:(g + 1) * SSD_STATE]


def _ssd_p_chunk(s, z_ref, xbc_ref, smr_ref, smt_ref, arow_ref, acol_ref, ltri_ref, utri_ref, exph_ref,
                 dvec_ref, nrm_ref, o_ref, ht_scr):
    smr = smr_ref[s]
    exph = exph_ref[...]
    cs_rows = _dot_hl_left(ltri_ref[...], smr * arow_ref[...])
    cs_t = _dot_hl(smt_ref[s] * acol_ref[...], utri_ref[...])
    dt_x = _dot_hl(smr, exph)
    c_hi, c_lo = _split(cs_rows)
    c_lo2 = (cs_rows - c_hi.astype(F32) - c_lo.astype(F32)).astype(BF16)
    cs_x = (jnp.dot(c_hi, exph, preferred_element_type=F32)
            + jnp.dot(c_lo, exph, preferred_element_type=F32)
            + jnp.dot(c_lo2, exph, preferred_element_type=F32))
    end_x = cs_x[CHUNK - 1:CHUNK, :]
    xbc = xbc_ref[s]
    x = xbc[:, 0:SSD_W]
    b_all = xbc[:, SSD_W:SSD_W + SSD_BC].astype(BF16)
    c_all = xbc[:, SSD_W + SSD_BC:SSD_W + 2 * SSD_BC].astype(BF16)
    xdt = x * dt_x
    xdt16 = xdt.astype(BF16)
    ht = ht_scr[s]
    y_state = jnp.dot(c_all, ht.astype(BF16), preferred_element_type=F32)
    upd = _dot_tn(b_all, xdt * jnp.exp(end_x - cs_x))
    cbs = [_dot_nt(c_all[:, g * SSD_STATE:(g + 1) * SSD_STATE], b_all[:, g * SSD_STATE:(g + 1) * SSD_STATE])
           for g in range(SSD_GROUPS)]

    gmask = (lax.broadcasted_iota(jnp.int32, (SSD_BC, SSD_W), 0) // SSD_STATE
             == lax.broadcasted_iota(jnp.int32, (SSD_BC, SSD_W), 1) // (SSD_W // SSD_GROUPS))
    ht_new = ht * jnp.exp(end_x) + jnp.where(gmask, upd, 0.0)
    ht_scr[s] = ht_new

    tril = (lax.broadcasted_iota(jnp.int32, (CHUNK, CHUNK), 1)
            <= lax.broadcasted_iota(jnp.int32, (CHUNK, CHUNK), 0))
    low = lax.broadcasted_iota(jnp.int32, (CHUNK, LANES), 1) < SSD_HD
    rep = SSD_HEADS // SSD_GROUPS
    pairs = []
    for pr in range(SSD_HEADS // 2):
        xp = xdt16[:, pr * LANES:(pr + 1) * LANES]
        ys = []
        for hh in (2 * pr, 2 * pr + 1):
            col = SM_DT + hh
            lm = jnp.exp(jnp.where(tril, cs_rows[:, col:col + 1] - cs_t[col:col + 1, :], -jnp.inf))
            ys.append(jnp.dot((cbs[hh // rep] * lm).astype(BF16), xp, preferred_element_type=F32))
        pairs.append(jnp.where(low, ys[0], ys[1]))
    y = jnp.concatenate(pairs, axis=1) + y_state * jnp.exp(cs_x)
    y = (y + x * dvec_ref[...]) * _silu(z_ref[s])
    gw = SSD_W // SSD_GROUPS
    for g in range(SSD_GROUPS):
        yg = y[:, g * gw:(g + 1) * gw]
        yn = yg * lax.rsqrt(jnp.mean(yg * yg, axis=-1, keepdims=True) + EPS)
        o_ref[s, :, g * gw:(g + 1) * gw] = (yn * nrm_ref[:, g * gw:(g + 1) * gw]).astype(BF16)
    return ht_new


def _ssd_p(z, xbc, smr, smt, arow, acol, ltri, utri, exph, dvec, nrm, l):
    nb, t_pad, _ = z.shape
    nj = t_pad // CHUNK
    grp = math.gcd(nb, MIX_GROUP)
    rows = lambda w_: pl.BlockSpec((grp, CHUNK, w_), lambda b, j: (b, j, 0))
    return pl.pallas_call(
        functools.partial(_ssd_p_kernel, grp=grp),
        out_shape=[jax.ShapeDtypeStruct((nb, t_pad, SSD_W), BF16),
                   jax.ShapeDtypeStruct((nb, SSD_HEADS, SSD_HD, SSD_STATE), F32)],
        grid=(nb // grp, nj),
        in_specs=[rows(SSD_W), rows(SSD_CONV_DIM), rows(LANES),
                  pl.BlockSpec((grp, 16, CHUNK), lambda b, j: (b, 0, j)),
                  _lspec((1, LANES), l), _lspec((16, 1), l), _const_spec((CHUNK, CHUNK)),
                  _const_spec((CHUNK, CHUNK)), _const_spec((LANES, SSD_W)),
                  _lspec((1, SSD_W), l), _lspec((1, SSD_W), l)],
        out_specs=[rows(SSD_W),
                   pl.BlockSpec((grp, SSD_HEADS, SSD_HD, SSD_STATE), lambda b, j: (b, 0, 0, 0))],
        scratch_shapes=[pltpu.VMEM((grp, SSD_BC, SSD_W), F32)],
        compiler_params=_params(("parallel", "arbitrary")), name="ssd_p")(
            z, xbc, smr, smt, arow, acol, ltri, utri, exph, dvec, nrm)


def _ssd_s_kernel(x_ref, b_ref, c_ref, sm_ref, xg_ref, zg_ref, a_ref, d_ref, nrm_ref, st_ref, *rest,
                  n_t, n_prev):
    if n_prev:
        prev_ref, o_ref, so_ref, ybuf = rest
        for k in range(n_prev):
            so_ref[k, 0] = prev_ref[k, 0]
    else:
        o_ref, so_ref, ybuf = rest
    hh = pl.program_id(0)
    rep = SSD_HEADS // SSD_GROUPS
    gw = SSD_W // SSD_GROUPS
    dt = [sm_ref[t, pl.ds(SM_DT + hh, 1), :] for t in range(n_t)]
    dec = [jnp.exp(dt[t] * a_ref[pl.ds(hh, 1), :]) for t in range(n_t)]

    def body(i, carry):
        for u in range(DEC_UNROLL):
            p = i * DEC_UNROLL + u
            hp = st_ref[0, 0, p]
            for t in range(n_t):
                hp = hp * dec[t] + b_ref[t] * (x_ref[t, pl.ds(p, 1), :] * dt[t])
                ybuf[t, pl.ds(hh * SSD_HD + p, 1), :] = jnp.sum(c_ref[t] * hp, axis=0, keepdims=True)
            so_ref[n_prev, 0, p] = hp
        return carry

    lax.fori_loop(0, SSD_HD // DEC_UNROLL, body, 0)

    @pl.when(lax.rem(hh, rep) == rep - 1)
    def _():
        r0 = pl.multiple_of((hh // rep) * gw, gw)
        for t in range(n_t):
            y = ybuf[t, pl.ds(r0, gw), :] + xg_ref[t] * d_ref[pl.ds(r0, gw), :]
            y = y * _silu(zg_ref[t])
            yn = y * lax.rsqrt(jnp.mean(y * y, axis=0, keepdims=True) + EPS)
            o_ref[t, pl.ds(r0, gw), :] = yn * nrm_ref[pl.ds(r0, gw), :]


def _ssd_s(pt, a_b, d_b, nrm_b, state, layer, prev):
    n_t = pt.shape[0]
    n_prev = 0 if prev is None else prev.shape[0]
    rep = SSD_HEADS // SSD_GROUPS
    gw = SSD_W // SSD_GROUPS
    blk = lambda h_, f: pl.BlockSpec((n_t, h_, LANES), lambda hh: (0, f(hh), 0))
    st_spec = pl.BlockSpec((1, 1, SSD_HD, SSD_STATE, LANES), lambda hh: (layer, hh, 0, 0, 0))
    per_head = lambda n: pl.BlockSpec((n, 1, SSD_HD, SSD_STATE, LANES), lambda hh: (0, hh, 0, 0, 0))
    return pl.pallas_call(
        functools.partial(_ssd_s_kernel, n_t=n_t, n_prev=n_prev),
        out_shape=[jax.ShapeDtypeStruct((n_t, SSD_W, LANES), F32),
                   jax.ShapeDtypeStruct((n_prev + 1, SSD_HEADS, SSD_HD, SSD_STATE, LANES), F32)],
        grid=(SSD_HEADS,),
        in_specs=[blk(SSD_HD, lambda hh: R_XBC // SSD_HD + hh),
                  blk(SSD_STATE, lambda hh: (R_XBC + SSD_W) // SSD_STATE + hh // rep),
                  blk(SSD_STATE, lambda hh: (R_XBC + SSD_W + SSD_BC) // SSD_STATE + hh // rep),
                  blk(LANES, lambda hh: R_SM // LANES),
                  blk(gw, lambda hh: R_XBC // gw + hh // rep),
                  blk(gw, lambda hh: R_Z // gw + hh // rep),
                  _lspec((SSD_HEADS, LANES), layer), _lspec((SSD_W, LANES), layer),
                  _lspec((SSD_W, LANES), layer), st_spec] + ([per_head(n_prev)] if n_prev else []),
        out_specs=[_const_spec((n_t, SSD_W, LANES)), per_head(n_prev + 1)],
        scratch_shapes=[pltpu.VMEM((n_t, SSD_W, LANES), F32)],
        compiler_params=_params(("arbitrary",)), name="ssd_s")(
            pt, pt, pt, pt, pt, pt, a_b, d_b, nrm_b, state, *([prev] if n_prev else []))


def _gla_p_kernel(gqk_ref, gv_ref, gg_ref, smr_ref, wg_ref, gb_ref, bl_ref, be_ref, g64_ref, gn_ref,
                  o_ref, s_ref, st_scr, *, t_real, grp):
    j = pl.program_id(1)
    nj = pl.num_programs(1)

    @pl.when(j == 0)
    def _():
        st_scr[...] = jnp.zeros(st_scr.shape, F32)

    finals = [_gla_p_chunk(s, j, gqk_ref, gv_ref, gg_ref, smr_ref, wg_ref, gb_ref, bl_ref, be_ref,
                           g64_ref, gn_ref, o_ref, st_scr, t_real) for s in range(grp)]

    @pl.when(j == nj - 1)
    def _():
        for s in range(grp):
            s_kv = finals[s].T
            for h in range(GLA_HEADS):
                s_ref[s, h] = s_kv[h * GLA_DK:(h + 1) * GLA_DK, h * GLA_DV:(h + 1) * GLA_DV]


def _gla_p_chunk(s, j, gqk_ref, gv_ref, gg_ref, smr_ref, wg_ref, gb_ref, bl_ref, be_ref, g64_ref,
                 gn_ref, o_ref, st_scr, t_real):
    sub = GLA_SUB
    n_sub = CHUNK // sub
    row = lax.broadcasted_iota(jnp.int32, (CHUNK, GLA_KW), 0)
    valid = j * CHUNK + row < t_real
    glog = _log_sigmoid(_dot(smr_ref[s], wg_ref[...]) + gb_ref[...]) * (1.0 / GLA_TAU)
    glog = jnp.where(valid, glog, 0.0)
    gqk = gqk_ref[s]
    gq = gqk[:, 0:GLA_KW] * (GLA_DK ** -0.5)
    gk = jnp.where(valid, gqk[:, GLA_KW:2 * GLA_KW], 0.0)
    v = gv_ref[s]
    bcl = _dot_hl_left(bl_ref[...], glog)
    tot = _dot_hl_left(be_ref[0], glog)
    mid = _dot_hl_left(be_ref[1], glog)
    qe = (gq * jnp.exp(bcl)).astype(BF16)
    qd = (gq * jnp.exp(bcl - mid)).astype(BF16)
    kp = gk * jnp.exp(mid - bcl)
    kend = (gk * jnp.exp(tot - bcl)).astype(BF16)
    dec = jnp.exp(tot)

    r_k = lax.broadcasted_iota(jnp.int32, (GLA_HEADS * sub, GLA_KW), 0) // sub
    c_k = lax.broadcasted_iota(jnp.int32, (GLA_HEADS * sub, GLA_KW), 1) // GLA_DK
    r_v = lax.broadcasted_iota(jnp.int32, (GLA_HEADS * sub, GLA_W), 0) // sub
    c_v = lax.broadcasted_iota(jnp.int32, (GLA_HEADS * sub, GLA_W), 1) // GLA_DV
    r_s = lax.broadcasted_iota(jnp.int32, (GLA_W, GLA_KW), 0) // GLA_DV
    c_s = lax.broadcasted_iota(jnp.int32, (GLA_W, GLA_KW), 1) // GLA_DK
    causal = (lax.broadcasted_iota(jnp.int32, (sub, GLA_HEADS * sub), 1) % sub
              <= lax.broadcasted_iota(jnp.int32, (sub, GLA_HEADS * sub), 0))
    sls = [slice(i * sub, (i + 1) * sub) for i in range(n_sub)]
    atts, upds = [], []
    for sl in sls:
        kbd = jnp.where(r_k == c_k, jnp.concatenate([kp[sl]] * GLA_HEADS, axis=0), 0.0)
        atts.append(_dot_nt(qd[sl], kbd))
        upds.append(_dot_tn(v[sl], kend[sl]))
    sts = [st_scr[s]]
    for i in range(n_sub):
        sts.append(sts[i] * dec[i * sub:i * sub + 1, :] + jnp.where(r_s == c_s, upds[i], 0.0))
    st_scr[s] = sts[n_sub]
    outs = []
    for i, sl in enumerate(sls):
        vbd = jnp.where(r_v == c_v, jnp.concatenate([v[sl]] * GLA_HEADS, axis=0), 0.0)
        outs.append(_dot(jnp.where(causal, atts[i], 0.0), vbd) + _dot_nt(qe[sl], sts[i]))
    o = jnp.concatenate(outs, axis=0)
    msq = _dot_hl(o * o, g64_ref[...])
    o_ref[s] = (o * lax.rsqrt(msq + EPS) * gn_ref[...] * _silu(gg_ref[s])).astype(BF16)
    return sts[n_sub]


def _gla_p(gqk, gv, gg, smr, wgp, gb, bl, be, g64, gn, t_real, l):
    nb, t_pad, _ = gv.shape
    nj = t_pad // CHUNK
    grp = math.gcd(nb, MIX_GROUP)
    rows = lambda w_: pl.BlockSpec((grp, CHUNK, w_), lambda b, j: (b, j, 0))
    return pl.pallas_call(
        functools.partial(_gla_p_kernel, t_real=t_real, grp=grp),
        out_shape=[jax.ShapeDtypeStruct((nb, t_pad, GLA_W), BF16),
                   jax.ShapeDtypeStruct((nb, GLA_HEADS, GLA_DK, GLA_DV), F32)],
        grid=(nb // grp, nj),
        in_specs=[rows(2 * GLA_KW), rows(GLA_W), rows(GLA_W), rows(LANES),
                  _lspec((LANES, GLA_KW), l), _lspec((1, GLA_KW), l),
                  _const_spec((CHUNK, CHUNK)), _const_spec((2, CHUNK, CHUNK)),
                  _const_spec((GLA_W, GLA_W)), _lspec((1, GLA_W), l)],
        out_specs=[rows(GLA_W),
                   pl.BlockSpec((grp, GLA_HEADS, GLA_DK, GLA_DV), lambda b, j: (b, 0, 0, 0))],
        scratch_shapes=[pltpu.VMEM((grp, GLA_W, GLA_KW), F32)],
        compiler_params=_params(("parallel", "arbitrary")), name="gla_p")(
            gqk, gv, gg, smr, wgp, gb, bl, be, g64, gn)


def _gla_s_kernel(q_ref, k_ref, v_ref, gg_ref, sm_ref, wgt_ref, gb_ref, gn_ref, st_ref,
                  o_ref, so_ref, eg_scr, acc_scr, *, n_t):
    for t in range(n_t):
        glog = _log_sigmoid(_dot(wgt_ref[...], sm_ref[t]) + gb_ref[...]) * (1.0 / GLA_TAU)
        eg_scr[t] = jnp.exp(glog)
        acc_scr[t] = jnp.zeros((GLA_DV, LANES), F32)

    def body(i, carry):
        states = []
        for u in range(DEC_UNROLL):
            kk = i * DEC_UNROLL + u
            s = st_ref[0, 0, kk]
            row = []
            for t in range(n_t):
                s = s * eg_scr[t, pl.ds(kk, 1), :] + k_ref[t, pl.ds(kk, 1), :] * v_ref[t]
                row.append(q_ref[t, pl.ds(kk, 1), :] * s)
            so_ref[0, 0, kk] = s
            states.append(row)
        for t in range(n_t):
            acc = acc_scr[t]
            for u in range(DEC_UNROLL):
                acc = acc + states[u][t]
            acc_scr[t] = acc
        return carry

    lax.fori_loop(0, GLA_DK // DEC_UNROLL, body, 0)
    for t in range(n_t):
        o = acc_scr[t]
        on = o * lax.rsqrt(jnp.mean(o * o, axis=0, keepdims=True) + EPS)
        o_ref[t] = on * gn_ref[...] * _silu(gg_ref[t])


def _gla_s(pt, wgt, gb_b, gn_b, state, layer):
    n_t = pt.shape[0]
    blk = lambda h_, f: pl.BlockSpec((n_t, h_, LANES), lambda hh: (0, f(hh), 0))
    return pl.pallas_call(
        functools.partial(_gla_s_kernel, n_t=n_t),
        out_shape=[jax.ShapeDtypeStruct((n_t, GLA_W, LANES), F32),
                   jax.ShapeDtypeStruct((1, GLA_HEADS, GLA_DK, GLA_DV, LANES), F32)],
        grid=(GLA_HEADS,),
        in_specs=[blk(GLA_DK, lambda hh: R_GQ // GLA_DK + hh),
                  blk(GLA_DK, lambda hh: R_GK // GLA_DK + hh),
                  blk(GLA_DV, lambda hh: R_GV // GLA_DV + hh),
                  blk(GLA_DV, lambda hh: R_GG // GLA_DV + hh),
                  blk(LANES, lambda hh: R_SM // LANES),
                  pl.BlockSpec((None, GLA_DK, LANES), lambda hh: (layer, hh, 0)),
                  pl.BlockSpec((None, GLA_DK, LANES), lambda hh: (layer, hh, 0)),
                  _lspec((GLA_DV, LANES), layer),
                  pl.BlockSpec((1, 1, GLA_DK, GLA_DV, LANES), lambda hh: (layer, hh, 0, 0, 0))],
        out_specs=[blk(GLA_DV, lambda hh: hh),
                   pl.BlockSpec((1, 1, GLA_DK, GLA_DV, LANES), lambda hh: (0, hh, 0, 0, 0))],
        scratch_shapes=[pltpu.VMEM((n_t, GLA_DK, LANES), F32), pltpu.VMEM((n_t, GLA_DV, LANES), F32)],
        compiler_params=_params(("arbitrary",)), name="gla_s")(
            pt, pt, pt, pt, pt, wgt, gb_b, gn_b, state)


def _tri_consts(n_pages):
    i = np.arange(CHUNK)
    ltri = (i[None, :] <= i[:, None]).astype(np.float32)
    utri = ltri.T
    mstrict = (i[:, None] > i[None, :]).astype(np.float32)
    same = (i[:, None] // GLA_SUB) == (i[None, :] // GLA_SUB)
    bl = (same & (i[None, :] <= i[:, None])).astype(np.float32)
    to_mid = i[None, :] <= (i[:, None] // GLA_SUB) * GLA_SUB + GLA_SUB // 2 - 1
    be = np.stack([same, same & to_mid]).astype(np.float32)
    r = np.arange(n_pages * 8)
    pgsuf = ((r[:, None] % 8 == r[None, :] % 8) & (r[None, :] // 8 > r[:, None] // 8)).astype(np.float32)
    h = np.arange(FOX_W)
    g64 = ((h[:, None] // 64) == (h[None, :] // 64)).astype(np.float32) / 64.0
    cw = np.arange(SSD_W)
    exph = (i[:, None] == SM_DT + cw[None, :] // SSD_HD).astype(np.float32)
    c = lambda a: jnp.asarray(a, BF16)
    return dict(ltri=c(ltri), utri=c(utri), mstrict=c(mstrict), bl=c(bl), be=c(be), pgsuf=c(pgsuf),
                g64=c(g64), exph=c(exph))


def kernel(x_prompt, x_sample, cache_fox_k, cache_fox_v, cache_fox_logf, state_ssm, state_conv,
           state_gla, page_table, meta_tokens, ffn1_norm, ffn1_w_in, ffn1_w_out, mix_norm, w_mix_in,
           fox_q_norm, fox_k_norm, fox_f_bias, ssd_conv_w, ssd_conv_b, ssd_dt_bias, ssd_a_log, ssd_d,
           ssd_norm, gla_w_gate, gla_gate_bias, gla_norm, w_mix_out, ffn2_norm, ffn2_w_in, ffn2_w_out):
    nbp, seq, _ = x_prompt.shape
    nbs, n_t, _ = x_sample.shape
    depth = ffn1_norm.shape[0]
    assert nbs == LANES and seq % CHUNK == 0
    t_real = N_META + seq
    t_pad = -(-t_real // CHUNK) * CHUNK
    n_pool, page_size = cache_fox_k.shape[1], cache_fox_k.shape[2]
    assert page_size == LANES and n_pool % 2 == 0
    n_pages = page_table.shape[1]
    consts = _tri_consts(n_pages)

    xs = jnp.transpose(x_sample, (1, 0, 2)).reshape(n_t * nbs, D_MODEL)

    k_cache = jnp.transpose(cache_fox_k, (0, 1, 3, 4, 2)).reshape(depth, n_pool, FOX_W, page_size)
    v_cache = jnp.transpose(cache_fox_v, (0, 1, 3, 4, 2)).reshape(depth, n_pool, FOX_W, page_size)
    lfpool = jnp.transpose(cache_fox_logf, (0, 1, 3, 2)).reshape(depth, n_pool // 2, 8, page_size)
    ssm_in = jnp.transpose(state_ssm, (0, 2, 3, 4, 1))
    gla_in = jnp.transpose(state_gla, (0, 2, 3, 4, 1))
    conv_in = jnp.transpose(state_conv, (0, 2, 1, 3))
    tbl = page_table.reshape(-1).astype(jnp.int32)

    w1i, w1o = ffn1_w_in.astype(BF16), ffn1_w_out.astype(BF16)
    w2i, w2o = ffn2_w_in.astype(BF16), ffn2_w_out.astype(BF16)
    wmo = w_mix_out.astype(BF16)
    col = [0] + list(IN_SPLITS) + [w_mix_in.shape[-1]]
    part = lambda i: w_mix_in[:, :, col[i]:col[i + 1]]
    fq, fk, fv, ff, sz, sxbc, sdt, gq, gk, gv, glr, gg = [part(i) for i in range(len(IN_SIZES))]
    n_small = FOX_HEADS + SSD_HEADS + GLA_RANK
    wproj = jnp.concatenate([fq, fk, fv, sz, sxbc, gq, gk, gv, gg, ff, sdt, glr,
                             jnp.zeros((depth, D_MODEL, LANES - n_small), F32)], axis=-1).astype(BF16)
    n1, n2, nm = ffn1_norm[:, None], ffn2_norm[:, None], mix_norm[:, None]
    qg = jnp.tile(fox_q_norm, (1, FOX_HEADS))[:, None]
    kg = jnp.tile(fox_k_norm, (1, FOX_HEADS))[:, None]
    a = -jnp.exp(ssd_a_log)
    lanes_pad = lambda v, off: jnp.pad(v, ((0, 0), (off, LANES - off - v.shape[1])))[:, None]
    fb, dtb, arow = lanes_pad(fox_f_bias, SM_F), lanes_pad(ssd_dt_bias, SM_DT), lanes_pad(a, SM_DT)
    acol = jnp.pad(a, ((0, 0), (SM_DT, 16 - SM_DT - SSD_HEADS)))[:, :, None]
    cw = jnp.pad(ssd_conv_w, ((0, 0), (0, 8 - SSD_CONV), (0, 0)))
    cb = ssd_conv_b[:, None]
    d_rep = jnp.repeat(ssd_d, SSD_HD, axis=1)
    dvec, snrm = d_rep[:, None], ssd_norm[:, None]
    wgp = jnp.pad(gla_w_gate, ((0, 0), (SM_LR, LANES - SM_LR - GLA_RANK), (0, 0))).astype(BF16)
    gb = gla_gate_bias[:, None]
    gn = jnp.tile(gla_norm, (1, GLA_HEADS))[:, None]
    on_lanes = lambda v: jnp.broadcast_to(v[:, :, None], v.shape + (LANES,))
    a_b, d_b, snrm_b = on_lanes(a), on_lanes(d_rep), on_lanes(ssd_norm)
    wgt = jnp.pad(jnp.transpose(gla_w_gate, (0, 2, 1)),
                  ((0, 0), (0, 0), (SM_LR, LANES - SM_LR - GLA_RANK))).astype(BF16)
    gb_b, gn_b = on_lanes(gla_gate_bias), on_lanes(gla_norm)

    outs_p = [[] for _ in range(6)]
    outs_s = [[] for _ in range(6)]
    ssm_all = None
    for l in range(depth):
        last = l == depth - 1
        if l == 0:
            meta = _ffn(meta_tokens, N_META, (n1, w1i, w1o, 0))
            frame = jnp.concatenate([jnp.broadcast_to(meta[None], (nbp, N_META, D_MODEL)),
                                     jnp.zeros((nbp, t_pad - N_META, D_MODEL), F32)], axis=1)
            if seq % FFN_TM == 0:
                xp = _ffn_into(x_prompt, frame, FFN_TM, (n1, w1i, w1o, 0), N_META)
            else:
                xp = lax.dynamic_update_slice(
                    frame, _ffn(x_prompt.reshape(nbp * seq, D_MODEL), CHUNK, (n1, w1i, w1o, 0)).reshape(
                        nbp, seq, D_MODEL), (0, N_META, 0))
            xp = xp.reshape(nbp * t_pad, D_MODEL)
        else:
            xp = _ffn(xp, FFN_TM, (n1, w1i, w1o, l))
        (qat, kaug, kt, vt, vt16, z, xbc, gqk, gvv, ggg, smr, smt, convp) = _proj_p(
            xp.reshape(nbp, t_pad, D_MODEL), nm, wproj, qg, kg, fb, dtb, cw, cb, consts["g64"],
            consts["ltri"], t_real, l)
        fox_o = _fox_p(qat, kaug, vt16)
        ssd_o, ssm_p = _ssd_p(z, xbc, smr, smt, arow, acol, consts["ltri"], consts["utri"],
                              consts["exph"], dvec, snrm, l)
        gla_o, gla_p = _gla_p(gqk, gvv, ggg, smr, wgp, gb, consts["bl"], consts["be"], consts["g64"],
                              gn, t_real, l)
        mix = ("rows", fox_o.reshape(-1, FOX_W), ssd_o.reshape(-1, SSD_W), gla_o.reshape(-1, GLA_W), wmo, l)
        window = (nbp, t_pad, N_META, seq) if last and seq % FFN_TM == 0 else None
        xp = _ffn(xp, FFN_TM, (n2, w2i, w2o, l), mix=mix, window=window)
        outs_p[0].append(kt.reshape(nbp, FOX_HEADS, FOX_HD, t_pad))
        outs_p[1].append(vt.reshape(nbp, FOX_HEADS, FOX_HD, t_pad))
        outs_p[2].append(jnp.transpose(smt[:, 0:FOX_HEADS, :t_real], (0, 2, 1)))
        outs_p[3].append(ssm_p)
        outs_p[4].append(convp[:, 8 - (SSD_CONV - 1):])
        outs_p[5].append(gla_p)

        xs = _ffn(xs, xs.shape[0], (n1, w1i, w1o, l))
        qs, krs, vrs, pt, cum, convs = _proj_s(xs, nm, wproj, qg, kg, fb, dtb, cw, cb, consts["g64"],
                                               conv_in, l)
        fox_os = _fox_s(tbl, qs, krs, vrs, cum, lfpool, consts["mstrict"], consts["pgsuf"],
                        k_cache, v_cache, l)
        ssd_ot, ssm_all = _ssd_s(pt, a_b, d_b, snrm_b, ssm_in, l, ssm_all)
        gla_ot, gla_s = _gla_s(pt, wgt, gb_b, gn_b, gla_in, l)
        xs = _ffn(xs, xs.shape[0], (n2, w2i, w2o, l), mix=("features", fox_os, ssd_ot, gla_ot, wmo, l))
        kts = pt[:, R_K:R_K + FOX_W].reshape(n_t, FOX_HEADS, FOX_HD, nbs)
        vts = pt[:, R_V:R_V + FOX_W].reshape(n_t, FOX_HEADS, FOX_HD, nbs)
        outs_s[0].append(jnp.transpose(kts, (3, 0, 1, 2)))
        outs_s[1].append(jnp.transpose(vts, (3, 0, 1, 2)))
        outs_s[2].append(jnp.transpose(pt[:, R_SM + SM_F:R_SM + SM_F + FOX_HEADS], (2, 0, 1)))
        outs_s[4].append(jnp.transpose(convs, (1, 0, 2)))
        outs_s[5].append(jnp.transpose(gla_s[0], (3, 0, 1, 2)))

    y_prompt = xp if xp.ndim == 3 else xp.reshape(nbp, t_pad, D_MODEL)[:, N_META:t_real]
    y_sample = jnp.transpose(xs.reshape(n_t, nbs, D_MODEL), (1, 0, 2))
    k_p = jnp.transpose(jnp.stack(outs_p[0])[..., :t_real], (0, 1, 4, 2, 3))
    v_p = jnp.transpose(jnp.stack(outs_p[1])[..., :t_real], (0, 1, 4, 2, 3))
    lf_p, ssm_p, conv_p, gla_p = [jnp.stack(a) for a in outs_p[2:]]
    k_s, v_s, lf_s, conv_s, gla_s = [jnp.stack(outs_s[i]) for i in (0, 1, 2, 4, 5)]
    ssm_s = jnp.transpose(ssm_all, (0, 4, 1, 2, 3))
    return (y_prompt, y_sample, k_p, v_p, lf_p, ssm_p, conv_p, gla_p, k_s, v_s, lf_s, ssm_s, conv_s, gla_s)
```

```python
import functools
import math

import jax
import jax.numpy as jnp
import numpy as np
from jax import lax
from jax.experimental import pallas as pl
from jax.experimental.pallas import tpu as pltpu

F32 = jnp.float32
BF16 = jnp.bfloat16

D_MODEL = 1024
N_META = 16
D_FF = 2816
EPS = 1e-6
FOX_HEADS = 4
FOX_HD = 64
FOX_W = FOX_HEADS * FOX_HD
SSD_HEADS = 8
SSD_HD = 64
SSD_W = SSD_HEADS * SSD_HD
SSD_GROUPS = 2
SSD_STATE = 64
SSD_CONV = 4
SSD_BC = SSD_GROUPS * SSD_STATE
SSD_CONV_DIM = SSD_W + 2 * SSD_BC
GLA_HEADS = 4
GLA_DK = 32
GLA_DV = 64
GLA_KW = GLA_HEADS * GLA_DK
GLA_W = GLA_HEADS * GLA_DV
GLA_RANK = 16
GLA_TAU = 16.0
D_MIX = FOX_W + SSD_W + GLA_W
IN_SIZES = (FOX_W, FOX_W, FOX_W, FOX_HEADS, SSD_W, SSD_CONV_DIM, SSD_HEADS,
            GLA_KW, GLA_KW, GLA_W, GLA_RANK, GLA_W)
IN_SPLITS = tuple(int(v) for v in np.cumsum(IN_SIZES)[:-1])

LOG2E = 1.4426950408889634
FOX_AUG = FOX_HEADS * 128

LANES = 128
CHUNK = 128
GLA_SUB = 32
FOX_KW = 512
FOX_QW = 512
FFN_CK = 256
FFN_TM = 512
PROJ_GROUP = 8
DEC_UNROLL = 4
MIX_GROUP = 8
VMEM_LIMIT = 60 * 1024 * 1024

C_Q, C_K, C_V = 0, 256, 512
C_Z = 768
C_XBC = 1280
C_GQ, C_GK, C_GV, C_GG = 2048, 2176, 2304, 2560
C_SM = 2816
N_PROJ = 2944
SM_F = 0
SM_DT = 4
SM_LR = 12
R_K, R_V, R_Z, R_XBC = 0, 256, 512, 1024
R_GQ, R_GK, R_GV, R_GG, R_SM = 1792, 1920, 2048, 2304, 2560
N_PT = 2688


def _dot(a, b):
    return jnp.dot(a.astype(BF16), b.astype(BF16), preferred_element_type=F32)


def _dot_nt(a, b):
    return lax.dot_general(a.astype(BF16), b.astype(BF16), (((1,), (1,)), ((), ())),
                           preferred_element_type=F32)


def _dot_tn(a, b):
    return lax.dot_general(a.astype(BF16), b.astype(BF16), (((0,), (0,)), ((), ())),
                           preferred_element_type=F32)


def _split(a):
    hi = a.astype(BF16)
    lo = (a - hi.astype(F32)).astype(BF16)
    return hi, lo


def _dot_hl(a, m):
    hi, lo = _split(a)
    return (jnp.dot(hi, m, preferred_element_type=F32) + jnp.dot(lo, m, preferred_element_type=F32))


def _dot_hl_left(m, a):
    hi, lo = _split(a)
    return (jnp.dot(m, hi, preferred_element_type=F32) + jnp.dot(m, lo, preferred_element_type=F32))


def _silu(x):
    return x * jax.nn.sigmoid(x)


def _softplus(x):
    return jnp.maximum(x, 0.0) + jnp.log1p(jnp.exp(-jnp.abs(x)))


def _log_sigmoid(x):
    return jnp.minimum(x, 0.0) - jnp.log1p(jnp.exp(-jnp.abs(x)))


def _rms(x, g):
    return x * lax.rsqrt(jnp.mean(x * x, axis=-1, keepdims=True) + EPS) * g


def _const_spec(shape):
    n = len(shape)
    return pl.BlockSpec(shape, lambda *_: (0,) * n)


def _lspec(shape, l):
    n = len(shape)
    return pl.BlockSpec((None,) + tuple(shape), lambda *_: (l,) + (0,) * n)


def _wspec(shape, l):
    n = len(shape)
    return pl.BlockSpec((None,) + tuple(shape), lambda *_: (l,) + (0,) * n, pipeline_mode=pl.Buffered(1))


def _params(sem):
    return pltpu.CompilerParams(dimension_semantics=sem, vmem_limit_bytes=VMEM_LIMIT)


def _swiglu_half(x, g_ref, win_ref, wout_ref):
    h = _rms(x, g_ref[...]).astype(BF16)
    acc = None
    for c in range(D_FF // FFN_CK):
        g = jnp.dot(h, win_ref[:, c * FFN_CK:(c + 1) * FFN_CK], preferred_element_type=F32)
        u = jnp.dot(h, win_ref[:, D_FF + c * FFN_CK:D_FF + (c + 1) * FFN_CK],
                    preferred_element_type=F32)
        a = (_silu(g) * u).astype(BF16)
        part = jnp.dot(a, wout_ref[c * FFN_CK:(c + 1) * FFN_CK, :], preferred_element_type=F32)
        acc = part if acc is None else acc + part
    return x + 0.5 * acc


def _ffn_kernel(*refs, mix, lead):
    rows = (lambda r: r[0]) if lead else (lambda r: r[...])
    x = rows(refs[0])
    if mix:
        fo_ref, so_ref, go_ref, wmo_ref = refs[1:5]
        if mix == "rows":
            fo, so, go = rows(fo_ref), rows(so_ref), rows(go_ref)
        else:
            n_t = fo_ref.shape[0]
            fo = fo_ref[...].reshape(n_t * LANES, FOX_W)
            so = jnp.concatenate([so_ref[t].T for t in range(n_t)], axis=0)
            go = jnp.concatenate([go_ref[t].T for t in range(n_t)], axis=0)
        x = x + _dot(fo, wmo_ref[0:FOX_W, :])
        x = x + _dot(so, wmo_ref[FOX_W:FOX_W + SSD_W, :])
        x = x + _dot(go, wmo_ref[FOX_W + SSD_W:D_MIX, :])
    g_ref, win_ref, wout_ref, o_ref = refs[-4:]
    y = _swiglu_half(x, g_ref, win_ref, wout_ref)
    if len(o_ref.shape) == 3:
        o_ref[0] = y
    else:
        o_ref[...] = y


def _ffn_into(x3, dest, tm, ffn, row0):
    nseq, t_in, _ = x3.shape
    g, win, wout, l = ffn
    out_spec = pl.BlockSpec((pl.Element(1), pl.Element(tm), pl.Element(D_MODEL)),
                            lambda b, i: (b, pl.multiple_of(row0 + i * tm, math.gcd(row0, tm)), 0))
    return pl.pallas_call(
        functools.partial(_ffn_kernel, mix=None, lead=True),
        out_shape=jax.ShapeDtypeStruct(dest.shape, F32), grid=(nseq, t_in // tm),
        in_specs=[pl.BlockSpec((1, tm, D_MODEL), lambda b, i: (b, i, 0)), pl.BlockSpec(memory_space=pl.ANY),
                  _lspec((1, D_MODEL), l), _wspec((D_MODEL, 2 * D_FF), l), _wspec((D_FF, D_MODEL), l)],
        out_specs=out_spec, input_output_aliases={1: 0},
        compiler_params=_params(("parallel", "parallel")), name="ffn")(x3, dest, g, win, wout)


def _ffn(x, tm, ffn, mix=None, window=None):
    if window:
        nseq, t_in, row0, t_out = window
        grid = (nseq, t_out // tm)
        view = lambda a: a.reshape(nseq, t_in, a.shape[-1])
        row = lambda w: pl.BlockSpec((pl.Element(1), pl.Element(tm), pl.Element(w)),
                                     lambda b, i: (b, pl.multiple_of(row0 + i * tm, math.gcd(row0, tm)), 0))
        out_shape = jax.ShapeDtypeStruct((nseq, t_out, D_MODEL), F32)
        out_spec = pl.BlockSpec((None, tm, D_MODEL), lambda b, i: (b, i, 0))
    else:
        rows = x.shape[0]
        grid = (rows // tm,)
        view = lambda a: a
        row = lambda w: pl.BlockSpec((tm, w), lambda i: (i, 0))
        out_shape = jax.ShapeDtypeStruct((rows, D_MODEL), F32)
        out_spec = row(D_MODEL)
    args, specs = [view(x)], [row(D_MODEL)]
    if mix:
        kind, fo, so, go, wmo, l = mix
        if kind == "rows":
            args += [view(fo), view(so), view(go)]
            specs += [row(FOX_W), row(SSD_W), row(GLA_W)]
        else:
            assert not window and tm == x.shape[0]
            args += [fo, so, go]
            specs += [pl.BlockSpec(a.shape, lambda i: (0, 0, 0)) for a in (fo, so, go)]
        args.append(wmo)
        specs.append(_wspec((D_MIX, D_MODEL), l))
    g, win, wout, l = ffn
    args += [g, win, wout]
    specs += [_lspec((1, D_MODEL), l), _wspec((D_MODEL, 2 * D_FF), l), _wspec((D_FF, D_MODEL), l)]
    return pl.pallas_call(
        functools.partial(_ffn_kernel, mix=mix[0] if mix else None, lead=bool(window)),
        out_shape=out_shape, grid=grid,
        in_specs=specs, out_specs=out_spec,
        compiler_params=_params(("parallel",) * len(grid)), name="mix_ffn" if mix else "ffn")(*args)


def _head_norm(x, gain_row, g64_ref):
    msq = _dot_hl(x * x, g64_ref[...])
    return x * lax.rsqrt(msq + EPS) * gain_row


def _small_block(sm, fb_ref, dtb_ref):
    logf = _log_sigmoid(sm + fb_ref[...])
    dt = _softplus(sm + dtb_ref[...])
    return logf, dt


def _proj_p_kernel(x_ref, nrm_ref, w_ref, qg_ref, kg_ref, fb_ref, dtb_ref, cw_ref, cb_ref, g64_ref,
                   ltri_ref,
                   qat_ref, ka_ref, kt_ref, vt_ref, vt16_ref, z_ref, xbc_ref, gqk_ref, gv_ref, gg_ref,
                   smr_ref, smt_ref, convp_ref,
                   xb_scr, carry_scr, *, t_real, grp):
    j = pl.program_id(1)

    @pl.when(j == 0)
    def _():
        xb_scr[:, 0:8, :] = jnp.zeros((grp, 8, SSD_CONV_DIM), F32)
        carry_scr[...] = jnp.zeros((grp, 8, LANES), F32)

    h = _rms(x_ref[...].reshape(grp * CHUNK, D_MODEL), nrm_ref[...]).astype(BF16)
    proj = lambda c0, n: jnp.dot(h, w_ref[:, c0:c0 + n], preferred_element_type=F32)
    seq = lambda a, g: a[g * CHUNK:(g + 1) * CHUNK]

    sm_all = proj(C_SM, LANES)
    lane = lax.broadcasted_iota(jnp.int32, (CHUNK, LANES), 1)
    row = lax.broadcasted_iota(jnp.int32, (CHUNK, LANES), 0)
    logf_all, dt_all = _small_block(sm_all, fb_ref, dtb_ref)
    f_rows = []
    for g in range(grp):
        logf = jnp.where(lane < SM_DT, seq(logf_all, g), 0.0)
        dt = jnp.where(j * CHUNK + row < t_real, seq(dt_all, g), 0.0)
        f = _dot_hl_left(ltri_ref[...], logf) + carry_scr[g, 0:1, :]
        carry_scr[g] = jnp.broadcast_to(f[CHUNK - 1:CHUNK, :], (8, LANES))
        smr_ref[g] = jnp.where(lane < SM_LR, dt, seq(sm_all, g))
        smt_ref[g] = jnp.where(lane < SM_DT, logf, dt).T[0:16, :]
        f_rows.append(f)

    q_all = _head_norm(proj(C_Q, FOX_W), qg_ref[...], g64_ref) * (FOX_HD ** -0.5 * LOG2E)
    k_all = _head_norm(proj(C_K, FOX_W), kg_ref[...], g64_ref)
    v_all = proj(C_V, FOX_W)
    l64 = lax.broadcasted_iota(jnp.int32, (CHUNK, FOX_HD), 1)
    for g in range(grp):
        q, k = seq(q_all, g), seq(k_all, g)
        qa, ka = [], []
        for hh in range(FOX_HEADS):
            y = f_rows[g][:, hh:hh + 1] * LOG2E
            hi = y.astype(BF16).astype(F32)
            mid = (y - hi).astype(BF16).astype(F32)
            lo = y - hi - mid
            parts = jnp.where(l64 == 0, hi, jnp.where(l64 == 1, mid, lo))
            ext_q = jnp.where(l64 < 3, parts, jnp.where(l64 < 6, 1.0, 0.0))
            parts = jnp.where(l64 == 3, hi, jnp.where(l64 == 4, mid, lo))
            ext_k = jnp.where(l64 < 3, 1.0, jnp.where(l64 < 6, -parts, 0.0))
            qa += [q[:, hh * FOX_HD:(hh + 1) * FOX_HD], ext_q]
            ka += [k[:, hh * FOX_HD:(hh + 1) * FOX_HD], ext_k]
        qat_ref[g] = jnp.concatenate(qa, axis=1).T.astype(BF16)
        ka_ref[g] = jnp.concatenate(ka, axis=1).astype(BF16)
        kt_ref[g] = k.T
        vt = seq(v_all, g).T
        vt_ref[g] = vt
        vt16_ref[g] = vt.astype(BF16)

    xbc_all = proj(C_XBC, SSD_CONV_DIM)
    r_end = t_real - (t_real - 1) // CHUNK * CHUNK
    for g in range(grp):
        xb_scr[g, 8:8 + CHUNK, :] = seq(xbc_all, g)
        conv = cb_ref[...]
        for w in range(SSD_CONV):
            conv = conv + xb_scr[g, pl.ds(8 - (SSD_CONV - 1) + w, CHUNK), :] * cw_ref[w:w + 1, :]
        xbc_ref[g] = _silu(conv)
        convp_ref[g] = xb_scr[g, r_end:r_end + 8, :]
        xb_scr[g, 0:8, :] = xb_scr[g, CHUNK:CHUNK + 8, :]

    z_ref[...] = proj(C_Z, SSD_W).reshape(grp, CHUNK, SSD_W)
    gqk_ref[...] = proj(C_GQ, 2 * GLA_KW).reshape(grp, CHUNK, 2 * GLA_KW)
    gv_ref[...] = proj(C_GV, GLA_W).reshape(grp, CHUNK, GLA_W)
    gg_ref[...] = proj(C_GG, GLA_W).reshape(grp, CHUNK, GLA_W)


def _proj_p(x3, nrm, w, qg, kg, fb, dtb, cw, cb, g64, ltri, t_real, l):
    nb, t_pad, _ = x3.shape
    nj = t_pad // CHUNK
    grp = math.gcd(nb, PROJ_GROUP)
    rows = lambda w_: pl.BlockSpec((grp, CHUNK, w_), lambda b, j: (b, j, 0))
    cols = lambda h_: pl.BlockSpec((grp, h_, CHUNK), lambda b, j: (b, 0, j))
    out_shape = [
        jax.ShapeDtypeStruct((nb, FOX_AUG, t_pad), BF16),
        jax.ShapeDtypeStruct((nb, t_pad, FOX_AUG), BF16),
        jax.ShapeDtypeStruct((nb, FOX_W, t_pad), F32),
        jax.ShapeDtypeStruct((nb, FOX_W, t_pad), F32),
        jax.ShapeDtypeStruct((nb, FOX_W, t_pad), BF16),
        jax.ShapeDtypeStruct((nb, t_pad, SSD_W), F32),
        jax.ShapeDtypeStruct((nb, t_pad, SSD_CONV_DIM), F32),
        jax.ShapeDtypeStruct((nb, t_pad, 2 * GLA_KW), F32),
        jax.ShapeDtypeStruct((nb, t_pad, GLA_W), F32),
        jax.ShapeDtypeStruct((nb, t_pad, GLA_W), F32),
        jax.ShapeDtypeStruct((nb, t_pad, LANES), F32),
        jax.ShapeDtypeStruct((nb, 16, t_pad), F32),
        jax.ShapeDtypeStruct((nb, 8, SSD_CONV_DIM), F32),
    ]
    out_specs = [cols(FOX_AUG), rows(FOX_AUG), cols(FOX_W), cols(FOX_W), cols(FOX_W), rows(SSD_W),
                 rows(SSD_CONV_DIM), rows(2 * GLA_KW), rows(GLA_W), rows(GLA_W), rows(LANES), cols(16),
                 pl.BlockSpec((grp, 8, SSD_CONV_DIM), lambda b, j: (b, 0, 0))]
    in_specs = [rows(D_MODEL), _lspec((1, D_MODEL), l), _wspec((D_MODEL, N_PROJ), l),
                _lspec((1, FOX_W), l), _lspec((1, FOX_W), l), _lspec((1, LANES), l),
                _lspec((1, LANES), l), _lspec((8, SSD_CONV_DIM), l), _lspec((1, SSD_CONV_DIM), l),
                _const_spec((FOX_W, FOX_W)), _const_spec((CHUNK, CHUNK))]
    return pl.pallas_call(
        functools.partial(_proj_p_kernel, t_real=t_real, grp=grp), out_shape=out_shape,
        grid=(nb // grp, nj), in_specs=in_specs, out_specs=out_specs,
        scratch_shapes=[pltpu.VMEM((grp, 8 + CHUNK, SSD_CONV_DIM), F32),
                        pltpu.VMEM((grp, 8, LANES), F32)],
        compiler_params=_params(("parallel", "arbitrary")), name="proj_p")(
            x3, nrm, w, qg, kg, fb, dtb, cw, cb, g64, ltri)


def _proj_s_kernel(x_ref, nrm_ref, w_ref, qg_ref, kg_ref, fb_ref, dtb_ref, cw_ref, cb_ref, g64_ref,
                   cst_ref,
                   q_ref, kr_ref, vr_ref, pt_ref, cum_ref, convs_ref,
                   hist_scr, carry_scr, *, n_t):
    t = pl.program_id(0)

    @pl.when(t == 0)
    def _():
        hist_scr[0:SSD_CONV - 1] = cst_ref[...]
        carry_scr[...] = jnp.zeros((LANES, LANES), F32)

    h = _rms(x_ref[...], nrm_ref[...]).astype(BF16)
    p = jnp.dot(h, w_ref[...], preferred_element_type=F32)

    q_ref[0] = _head_norm(p[:, C_Q:C_Q + FOX_W], qg_ref[...], g64_ref) * (FOX_HD ** -0.5)
    k = _head_norm(p[:, C_K:C_K + FOX_W], kg_ref[...], g64_ref)
    v = p[:, C_V:C_V + FOX_W]
    kr_ref[0] = k
    vr_ref[0] = v
    for c in range(FOX_W // LANES):
        pt_ref[0, R_K + c * LANES:R_K + (c + 1) * LANES, :] = k[:, c * LANES:(c + 1) * LANES].T
        pt_ref[0, R_V + c * LANES:R_V + (c + 1) * LANES, :] = v[:, c * LANES:(c + 1) * LANES].T

    hist_scr[pl.ds(SSD_CONV - 1 + t, 1)] = p[:, C_XBC:C_XBC + SSD_CONV_DIM][None]
    conv = cb_ref[...]
    for w in range(SSD_CONV):
        conv = conv + hist_scr[t + w] * cw_ref[w:w + 1, :]
    xbc = _silu(conv)

    @pl.when(t == n_t - 1)
    def _():
        convs_ref[...] = hist_scr[n_t:n_t + SSD_CONV - 1]

    sm = p[:, C_SM:C_SM + LANES]
    lane = lax.broadcasted_iota(jnp.int32, (LANES, LANES), 1)
    logf, dt = _small_block(sm, fb_ref, dtb_ref)
    cum = carry_scr[...] + jnp.where(lane < SM_DT, logf, 0.0)
    carry_scr[...] = cum
    cum_ref[0] = cum
    smc = jnp.where(lane < SM_DT, logf, jnp.where(lane < SM_LR, dt, sm))

    def put_t(r0, val):
        for c in range(val.shape[1] // LANES):
            pt_ref[0, r0 + c * LANES:r0 + (c + 1) * LANES, :] = val[:, c * LANES:(c + 1) * LANES].T

    put_t(R_Z, p[:, C_Z:C_Z + SSD_W])
    put_t(R_XBC, xbc)
    put_t(R_GQ, p[:, C_GQ:C_GQ + GLA_KW] * (GLA_DK ** -0.5))
    put_t(R_GK, p[:, C_GK:C_GK + GLA_KW])
    put_t(R_GV, p[:, C_GV:C_GV + GLA_W])
    put_t(R_GG, p[:, C_GG:C_GG + GLA_W])
    put_t(R_SM, smc)


def _proj_s(x, nrm, w, qg, kg, fb, dtb, cw, cb, g64, conv_state, l):
    n_t = x.shape[0] // LANES
    per_t = lambda a, b: pl.BlockSpec((1, a, b), lambda t: (t, 0, 0))
    out_shape = [
        jax.ShapeDtypeStruct((n_t, LANES, FOX_W), F32),
        jax.ShapeDtypeStruct((n_t, LANES, FOX_W), F32),
        jax.ShapeDtypeStruct((n_t, LANES, FOX_W), F32),
        jax.ShapeDtypeStruct((n_t, N_PT, LANES), F32),
        jax.ShapeDtypeStruct((n_t, LANES, LANES), F32),
        jax.ShapeDtypeStruct((SSD_CONV - 1, LANES, SSD_CONV_DIM), F32),
    ]
    out_specs = [per_t(LANES, FOX_W), per_t(LANES, FOX_W), per_t(LANES, FOX_W), per_t(N_PT, LANES),
                 per_t(LANES, LANES), _const_spec((SSD_CONV - 1, LANES, SSD_CONV_DIM))]
    in_specs = [pl.BlockSpec((LANES, D_MODEL), lambda t: (t, 0)), _lspec((1, D_MODEL), l),
                _wspec((D_MODEL, N_PROJ), l), _lspec((1, FOX_W), l), _lspec((1, FOX_W), l),
                _lspec((1, LANES), l), _lspec((1, LANES), l), _lspec((8, SSD_CONV_DIM), l),
                _lspec((1, SSD_CONV_DIM), l), _const_spec((FOX_W, FOX_W)),
                _lspec((SSD_CONV - 1, LANES, SSD_CONV_DIM), l)]
    return pl.pallas_call(
        functools.partial(_proj_s_kernel, n_t=n_t), out_shape=out_shape, grid=(n_t,),
        in_specs=in_specs, out_specs=out_specs,
        scratch_shapes=[pltpu.VMEM((n_t + SSD_CONV - 1, LANES, SSD_CONV_DIM), F32),
                        pltpu.VMEM((LANES, LANES), F32)],
        compiler_params=_params(("arbitrary",)), name="proj_s")(
            x, nrm, w, qg, kg, fb, dtb, cw, cb, g64, conv_state)


def _fox_p_kernel(qat_ref, ka_ref, vt_ref, o_ref, *, kw, qw, tile0, tails):
    qi = pl.program_id(1) + tile0
    nsub = kw // qw
    n_full = qi // nsub

    def scores(start, width):
        return tuple(jnp.dot(ka_ref[0, pl.ds(start, width), h * LANES:(h + 1) * LANES],
                             qat_ref[0, h * LANES:(h + 1) * LANES, :], preferred_element_type=F32)
                     for h in range(FOX_HEADS))

    def update(ss, start, width, carry, masked):
        prs, stats = [], []
        for h in range(FOX_HEADS):
            m, l, _ = carry[h]
            s = ss[h]
            if masked:
                kpos = lax.broadcasted_iota(jnp.int32, (width, qw), 0) + start
                qpos = lax.broadcasted_iota(jnp.int32, (width, qw), 1) + qi * qw
                s = jnp.where(kpos <= qpos, s, -jnp.inf)
            m_new = jnp.maximum(m, jnp.max(s, axis=0, keepdims=True))
            alpha = jnp.exp2(m - m_new)
            pr = jnp.exp2(s - m_new)
            stats.append((m_new, alpha * l + jnp.sum(pr, axis=0, keepdims=True), alpha))
            prs.append(pr.astype(BF16))
        out = []
        for h in range(FOX_HEADS):
            vt = vt_ref[0, h * FOX_HD:(h + 1) * FOX_HD, pl.ds(start, width)]
            m_new, l, alpha = stats[h]
            out.append((m_new, l, alpha * carry[h][2] + jnp.dot(vt, prs[h], preferred_element_type=F32)))
        return tuple(out)

    init = tuple((jnp.full((1, qw), -jnp.inf, F32), jnp.zeros((1, qw), F32),
                  jnp.zeros((FOX_HD, qw), F32)) for _ in range(FOX_HEADS))

    def body(i, carry):
        at = pl.multiple_of(i * kw, kw)
        return update(scores(at, kw), at, kw, carry, False)

    carry = lax.fori_loop(0, n_full, body, init)
    start = pl.multiple_of(n_full * kw, kw)
    for r in tails:
        @pl.when(lax.rem(qi, nsub) == r)
        def _(r=r):
            width = (r + 1) * qw
            fin = update(scores(start, width), start, width, carry, True)
            o_t = jnp.concatenate([acc / l for (_, l, acc) in fin], axis=0)
            o_ref[0] = o_t.T.astype(BF16)


def _fox_p_tiles(qat, ka, vt16, qw, tile0, n_tiles, kw):
    nb, _, t_pad = qat.shape
    tails = sorted({(tile0 + i) % (kw // qw) for i in range(n_tiles)})
    return pl.pallas_call(
        functools.partial(_fox_p_kernel, kw=kw, qw=qw, tile0=tile0, tails=tails),
        out_shape=jax.ShapeDtypeStruct((nb, n_tiles * qw, FOX_W), BF16), grid=(nb, n_tiles),
        in_specs=[pl.BlockSpec((1, FOX_AUG, qw), lambda b, i: (b, 0, i + tile0)),
                  pl.BlockSpec((1, t_pad, FOX_AUG), lambda b, i: (b, 0, 0)),
                  pl.BlockSpec((1, FOX_W, t_pad), lambda b, i: (b, 0, 0))],
        out_specs=pl.BlockSpec((1, qw, FOX_W), lambda b, i: (b, i, 0)),
        compiler_params=_params(("parallel", "arbitrary")), name="fox_p")(qat, ka, vt16)


def _fox_p(qat, ka, vt16):
    t_pad = qat.shape[2]
    n_wide = t_pad // FOX_QW
    rest = (t_pad - n_wide * FOX_QW) // CHUNK
    parts = []
    if n_wide:
        parts.append(_fox_p_tiles(qat, ka, vt16, FOX_QW, 0, n_wide, FOX_KW))
    if rest:
        kw_rest = n_wide * FOX_QW if n_wide else FOX_KW
        parts.append(_fox_p_tiles(qat, ka, vt16, CHUNK, n_wide * FOX_QW // CHUNK, rest, kw_rest))
    return parts[0] if len(parts) == 1 else jnp.concatenate(parts, axis=1)


def _fox_s_kernel(tbl_ref, q_ref, kr_ref, vr_ref, cum_ref, lf_ref, mstrict_ref, pgsuf_ref,
                  k_hbm, v_hbm, o_ref, kbuf, vbuf, lfst, sem, *, layer, n_pages, n_t):
    b = pl.program_id(0)
    nb = pl.num_programs(0)
    slot = lax.rem(b, 2)

    def copies(seq, sl):
        out = []
        for pg in range(n_pages):
            page = tbl_ref[seq * n_pages + pg]
            out.append(pltpu.make_async_copy(k_hbm.at[layer, page], kbuf.at[sl, pg], sem.at[0, sl]))
            out.append(pltpu.make_async_copy(v_hbm.at[layer, page], vbuf.at[sl, pg], sem.at[1, sl]))
        return out

    @pl.when(b == 0)
    def _():
        lfst[...] = jnp.zeros(lfst.shape, F32)
        for c in copies(0, 0):
            c.start()

    @pl.when(b + 1 < nb)
    def _():
        for c in copies(b + 1, 1 - slot):
            c.start()

    for pg in range(n_pages):
        page = tbl_ref[b * n_pages + pg]
        lfst[pg * 8:pg * 8 + FOX_HEADS, :] = lf_ref[0, page // 2, pl.ds(lax.rem(page, 2) * FOX_HEADS,
                                                                      FOX_HEADS), :]
    lf = lfst[...]
    within = _dot_hl(lf, mstrict_ref[...])
    tot = jnp.broadcast_to(jnp.sum(lf, axis=-1, keepdims=True), lf.shape)
    dsuf = within + _dot_hl_left(pgsuf_ref[...], tot)

    lane = lax.broadcasted_iota(jnp.int32, (8, FOX_W), 1)
    rowh = lax.broadcasted_iota(jnp.int32, (8, FOX_W), 0)
    headmask = (lane // FOX_HD) == rowh
    qexp, cumcol = [], []
    cum_rows = jnp.concatenate([cum_ref[t, pl.ds(b, 1), :] for t in range(n_t)]
                               + [jnp.zeros((LANES - n_t, LANES), F32)], axis=0)
    cum_t = cum_rows.T[0:8, 0:8]
    for t in range(n_t):
        qrow = q_ref[t, pl.ds(b, 1), :]
        qexp.append(jnp.where(headmask, jnp.broadcast_to(qrow, (8, FOX_W)), 0.0))
        cumcol.append(cum_t[:, t:t + 1])
    qexp = jnp.concatenate(qexp, axis=0).astype(BF16)
    cumcol = jnp.concatenate(cumcol, axis=0)
    rows = 8 * n_t

    kn = jnp.concatenate([kr_ref[t, pl.ds(b, 1), :] for t in range(n_t)]
                         + [jnp.zeros((8 - n_t, FOX_W), F32)], axis=0)
    vn = jnp.concatenate([vr_ref[t, pl.ds(b, 1), :] for t in range(n_t)]
                         + [jnp.zeros((8 - n_t, FOX_W), F32)], axis=0)
    s_new = _dot_nt(qexp, kn) + cumcol - jnp.concatenate([cum_t] * n_t, axis=0)
    tq = lax.broadcasted_iota(jnp.int32, (rows, 8), 0) // 8
    tk = lax.broadcasted_iota(jnp.int32, (rows, 8), 1)
    s_new = jnp.where(tk <= tq, s_new, -jnp.inf)

    for c in copies(b, slot):
        c.wait()

    s_pg = []
    for pg in range(n_pages):
        bias = jnp.concatenate([dsuf[pg * 8:(pg + 1) * 8, :]] * n_t, axis=0) + cumcol
        s_pg.append(_dot(qexp, kbuf[slot, pg]) + bias)
    m = s_pg[0]
    for s in s_pg[1:]:
        m = jnp.maximum(m, s)
    m = jnp.maximum(jnp.max(m, axis=-1, keepdims=True), jnp.max(s_new, axis=-1, keepdims=True))
    p_new = jnp.exp(s_new - m)
    l = jnp.sum(p_new, axis=-1, keepdims=True)
    acc = _dot(p_new, vn)
    lsum = None
    for pg in range(n_pages):
        pr = jnp.exp(s_pg[pg] - m)
        lsum = pr if lsum is None else lsum + pr
        acc = acc + _dot_nt(pr, vbuf[slot, pg])
    l = l + jnp.sum(lsum, axis=-1, keepdims=True)
    o = acc / l
    for t in range(n_t):
        ot = jnp.where(headmask, o[t * 8:(t + 1) * 8, :], 0.0)
        o_ref[t, pl.ds(b, 1), :] = jnp.sum(ot, axis=0, keepdims=True)


def _fox_s(tbl, q, kr, vr, cum, lfpool, mstrict, pgsuf, k_cache, v_cache, layer):
    n_t, nb, _ = q.shape
    n_pages = tbl.shape[0] // nb
    whole = lambda a: pl.BlockSpec(a.shape, lambda b, tbl_: (0,) * a.ndim)
    grid_spec = pltpu.PrefetchScalarGridSpec(
        num_scalar_prefetch=1, grid=(nb,),
        in_specs=[whole(q), whole(kr), whole(vr), whole(cum),
                  pl.BlockSpec((1,) + lfpool.shape[1:], lambda b, tbl_: (layer, 0, 0, 0)),
                  whole(mstrict), whole(pgsuf),
                  pl.BlockSpec(memory_space=pl.ANY), pl.BlockSpec(memory_space=pl.ANY)],
        out_specs=pl.BlockSpec((n_t, nb, FOX_W), lambda b, tbl_: (0, 0, 0)),
        scratch_shapes=[pltpu.VMEM((2, n_pages, FOX_W, LANES), F32),
                        pltpu.VMEM((2, n_pages, FOX_W, LANES), F32),
                        pltpu.VMEM((n_pages * 8, LANES), F32),
                        pltpu.SemaphoreType.DMA((2, 2))])
    return pl.pallas_call(
        functools.partial(_fox_s_kernel, layer=layer, n_pages=n_pages, n_t=n_t),
        out_shape=jax.ShapeDtypeStruct((n_t, nb, FOX_W), F32), grid_spec=grid_spec,
        compiler_params=_params(("arbitrary",)), name="fox_s")(
            tbl, q, kr, vr, cum, lfpool, mstrict, pgsuf, k_cache, v_cache)


def _ssd_p_kernel(z_ref, xbc_ref, smr_ref, smt_ref, arow_ref, acol_ref, ltri_ref, utri_ref, exph_ref,
                  dvec_ref, nrm_ref, o_ref, h_ref, ht_scr, *, grp):
    j = pl.program_id(1)
    nj = pl.num_programs(1)

    @pl.when(j == 0)
    def _():
        ht_scr[...] = jnp.zeros(ht_scr.shape, F32)

    finals = [_ssd_p_chunk(s, z_ref, xbc_ref, smr_ref, smt_ref, arow_ref, acol_ref, ltri_ref, utri_ref,
                           exph_ref, dvec_ref, nrm_ref, o_ref, ht_scr) for s in range(grp)]

    @pl.when(j == nj - 1)
    def _():
        rep = SSD_HEADS // SSD_GROUPS
        for s in range(grp):
            h_t = finals[s].T
            for hh in range(SSD_HEADS):
                g = hh // rep
                h_ref[s, hh] = h_t[hh * SSD_HD:(hh + 1) * SSD_HD, g * SSD_STATE:(g + 1) * SSD_STATE]


def _ssd_p_chunk(s, z_ref, xbc_ref, smr_ref, smt_ref, arow_ref, acol_ref, ltri_ref, utri_ref, exph_ref,
                 dvec_ref, nrm_ref, o_ref, ht_scr):
    smr = smr_ref[s]
    exph = exph_ref[...]
    cs_rows = _dot_hl_left(ltri_ref[...], smr * arow_ref[...])
    cs_t = _dot_hl(smt_ref[s] * acol_ref[...], utri_ref[...])
    dt_x = _dot_hl(smr, exph)
    c_hi, c_lo = _split(cs_rows)
    c_lo2 = (cs_rows - c_hi.astype(F32) - c_lo.astype(F32)).astype(BF16)
    cs_x = (jnp.dot(c_hi, exph, preferred_element_type=F32)
            + jnp.dot(c_lo, exph, preferred_element_type=F32)
            + jnp.dot(c_lo2, exph, preferred_element_type=F32))
    end_x = cs_x[CHUNK - 1:CHUNK, :]
    xbc = xbc_ref[s]
    x = xbc[:, 0:SSD_W]
    b_all = xbc[:, SSD_W:SSD_W + SSD_BC].astype(BF16)
    c_all = xbc[:, SSD_W + SSD_BC:SSD_W + 2 * SSD_BC].astype(BF16)
    xdt = x * dt_x
    xdt16 = xdt.astype(BF16)
    ht = ht_scr[s]
    y_state = jnp.dot(c_all, ht.astype(BF16), preferred_element_type=F32)
    upd = _dot_tn(b_all, xdt * jnp.exp(end_x - cs_x))
    cbs = [_dot_nt(c_all[:, g * SSD_STATE:(g + 1) * SSD_STATE], b_all[:, g * SSD_STATE:(g + 1) * SSD_STATE])
           for g in range(SSD_GROUPS)]

    gmask = (lax.broadcasted_iota(jnp.int32, (SSD_BC, SSD_W), 0) // SSD_STATE
             == lax.broadcasted_iota(jnp.int32, (SSD_BC, SSD_W), 1) // (SSD_W // SSD_GROUPS))
    ht_new = ht * jnp.exp(end_x) + jnp.where(gmask, upd, 0.0)
    ht_scr[s] = ht_new

    tril = (lax.broadcasted_iota(jnp.int32, (CHUNK, CHUNK), 1)
            <= lax.broadcasted_iota(jnp.int32, (CHUNK, CHUNK), 0))
    low = lax.broadcasted_iota(jnp.int32, (CHUNK, LANES), 1) < SSD_HD
    rep = SSD_HEADS // SSD_GROUPS
    pairs = []
    for pr in range(SSD_HEADS // 2):
        xp = xdt16[:, pr * LANES:(pr + 1) * LANES]
        ys = []
        for hh in (2 * pr, 2 * pr + 1):
            col = SM_DT + hh
            lm = jnp.exp(jnp.where(tril, cs_rows[:, col:col + 1] - cs_t[col:col + 1, :], -jnp.inf))
            ys.append(jnp.dot((cbs[hh // rep] * lm).astype(BF16), xp, preferred_element_type=F32))
        pairs.append(jnp.where(low, ys[0], ys[1]))
    y = jnp.concatenate(pairs, axis=1) + y_state * jnp.exp(cs_x)
    y = (y + x * dvec_ref[...]) * _silu(z_ref[s])
    gw = SSD_W // SSD_GROUPS
    for g in range(SSD_GROUPS):
        yg = y[:, g * gw:(g + 1) * gw]
        yn = yg * lax.rsqrt(jnp.mean(yg * yg, axis=-1, keepdims=True) + EPS)
        o_ref[s, :, g * gw:(g + 1) * gw] = (yn * nrm_ref[:, g * gw:(g + 1) * gw]).astype(BF16)
    return ht_new


def _ssd_p(z, xbc, smr, smt, arow, acol, ltri, utri, exph, dvec, nrm, l):
    nb, t_pad, _ = z.shape
    nj = t_pad // CHUNK
    grp = math.gcd(nb, MIX_GROUP)
    rows = lambda w_: pl.BlockSpec((grp, CHUNK, w_), lambda b, j: (b, j, 0))
    return pl.pallas_call(
        functools.partial(_ssd_p_kernel, grp=grp),
        out_shape=[jax.ShapeDtypeStruct((nb, t_pad, SSD_W), BF16),
                   jax.ShapeDtypeStruct((nb, SSD_HEADS, SSD_HD, SSD_STATE), F32)],
        grid=(nb // grp, nj),
        in_specs=[rows(SSD_W), rows(SSD_CONV_DIM), rows(LANES),
                  pl.BlockSpec((grp, 16, CHUNK), lambda b, j: (b, 0, j)),
                  _lspec((1, LANES), l), _lspec((16, 1), l), _const_spec((CHUNK, CHUNK)),
                  _const_spec((CHUNK, CHUNK)), _const_spec((LANES, SSD_W)),
                  _lspec((1, SSD_W), l), _lspec((1, SSD_W), l)],
        out_specs=[rows(SSD_W),
                   pl.BlockSpec((grp, SSD_HEADS, SSD_HD, SSD_STATE), lambda b, j: (b, 0, 0, 0))],
        scratch_shapes=[pltpu.VMEM((grp, SSD_BC, SSD_W), F32)],
        compiler_params=_params(("parallel", "arbitrary")), name="ssd_p")(
            z, xbc, smr, smt, arow, acol, ltri, utri, exph, dvec, nrm)


def _ssd_s_kernel(x_ref, b_ref, c_ref, sm_ref, xg_ref, zg_ref, a_ref, d_ref, nrm_ref, st_ref, *rest,
                  n_t, n_prev):
    if n_prev:
        prev_ref, o_ref, so_ref, ybuf = rest
        for k in range(n_prev):
            so_ref[k, 0] = prev_ref[k, 0]
    else:
        o_ref, so_ref, ybuf = rest
    hh = pl.program_id(0)
    rep = SSD_HEADS // SSD_GROUPS
    gw = SSD_W // SSD_GROUPS
    dt = [sm_ref[t, pl.ds(SM_DT + hh, 1), :] for t in range(n_t)]
    dec = [jnp.exp(dt[t] * a_ref[pl.ds(hh, 1), :]) for t in range(n_t)]

    def body(i, carry):
        for u in range(DEC_UNROLL):
            p = i * DEC_UNROLL + u
            hp = st_ref[0, 0, p]
            for t in range(n_t):
                hp = hp * dec[t] + b_ref[t] * (x_ref[t, pl.ds(p, 1), :] * dt[t])
                ybuf[t, pl.ds(hh * SSD_HD + p, 1), :] = jnp.sum(c_ref[t] * hp, axis=0, keepdims=True)
            so_ref[n_prev, 0, p] = hp
        return carry

    lax.fori_loop(0, SSD_HD // DEC_UNROLL, body, 0)

    @pl.when(lax.rem(hh, rep) == rep - 1)
    def _():
        r0 = pl.multiple_of((hh // rep) * gw, gw)
        for t in range(n_t):
            y = ybuf[t, pl.ds(r0, gw), :] + xg_ref[t] * d_ref[pl.ds(r0, gw), :]
            y = y * _silu(zg_ref[t])
            yn = y * lax.rsqrt(jnp.mean(y * y, axis=0, keepdims=True) + EPS)
            o_ref[t, pl.ds(r0, gw), :] = yn * nrm_ref[pl.ds(r0, gw), :]


def _ssd_s(pt, a_b, d_b, nrm_b, state, layer, prev):
    n_t = pt.shape[0]
    n_prev = 0 if prev is None else prev.shape[0]
    rep = SSD_HEADS // SSD_GROUPS
    gw = SSD_W // SSD_GROUPS
    blk = lambda h_, f: pl.BlockSpec((n_t, h_, LANES), lambda hh: (0, f(hh), 0))
    st_spec = pl.BlockSpec((1, 1, SSD_HD, SSD_STATE, LANES), lambda hh: (layer, hh, 0, 0, 0))
    per_head = lambda n: pl.BlockSpec((n, 1, SSD_HD, SSD_STATE, LANES), lambda hh: (0, hh, 0, 0, 0))
    return pl.pallas_call(
        functools.partial(_ssd_s_kernel, n_t=n_t, n_prev=n_prev),
        out_shape=[jax.ShapeDtypeStruct((n_t, SSD_W, LANES), F32),
                   jax.ShapeDtypeStruct((n_prev + 1, SSD_HEADS, SSD_HD, SSD_STATE, LANES), F32)],
        grid=(SSD_HEADS,),
        in_specs=[blk(SSD_HD, lambda hh: R_XBC // SSD_HD + hh),
                  blk(SSD_STATE, lambda hh: (R_XBC + SSD_W) // SSD_STATE + hh // rep),
                  blk(SSD_STATE, lambda hh: (R_XBC + SSD_W + SSD_BC) // SSD_STATE + hh // rep),
                  blk(LANES, lambda hh: R_SM // LANES),
                  blk(gw, lambda hh: R_XBC // gw + hh // rep),
                  blk(gw, lambda hh: R_Z // gw + hh // rep),
                  _lspec((SSD_HEADS, LANES), layer), _lspec((SSD_W, LANES), layer),
                  _lspec((SSD_W, LANES), layer), st_spec] + ([per_head(n_prev)] if n_prev else []),
        out_specs=[_const_spec((n_t, SSD_W, LANES)), per_head(n_prev + 1)],
        scratch_shapes=[pltpu.VMEM((n_t, SSD_W, LANES), F32)],
        compiler_params=_params(("arbitrary",)), name="ssd_s")(
            pt, pt, pt, pt, pt, pt, a_b, d_b, nrm_b, state, *([prev] if n_prev else []))


def _gla_p_kernel(gqk_ref, gv_ref, gg_ref, smr_ref, wg_ref, gb_ref, bl_ref, be_ref, g64_ref, gn_ref,
                  o_ref, s_ref, st_scr, *, t_real, grp):
    j = pl.program_id(1)
    nj = pl.num_programs(1)

    @pl.when(j == 0)
    def _():
        st_scr[...] = jnp.zeros(st_scr.shape, F32)

    finals = [_gla_p_chunk(s, j, gqk_ref, gv_ref, gg_ref, smr_ref, wg_ref, gb_ref, bl_ref, be_ref,
                           g64_ref, gn_ref, o_ref, st_scr, t_real) for s in range(grp)]

    @pl.when(j == nj - 1)
    def _():
        for s in range(grp):
            s_kv = finals[s].T
            for h in range(GLA_HEADS):
                s_ref[s, h] = s_kv[h * GLA_DK:(h + 1) * GLA_DK, h * GLA_DV:(h + 1) * GLA_DV]


def _gla_p_chunk(s, j, gqk_ref, gv_ref, gg_ref, smr_ref, wg_ref, gb_ref, bl_ref, be_ref, g64_ref,
                 gn_ref, o_ref, st_scr, t_real):
    sub = GLA_SUB
    n_sub = CHUNK // sub
    row = lax.broadcasted_iota(jnp.int32, (CHUNK, GLA_KW), 0)
    valid = j * CHUNK + row < t_real
    glog = _log_sigmoid(_dot(smr_ref[s], wg_ref[...]) + gb_ref[...]) * (1.0 / GLA_TAU)
    glog = jnp.where(valid, glog, 0.0)
    gqk = gqk_ref[s]
    gq = gqk[:, 0:GLA_KW] * (GLA_DK ** -0.5)
    gk = jnp.where(valid, gqk[:, GLA_KW:2 * GLA_KW], 0.0)
    v = gv_ref[s]
    bcl = _dot_hl_left(bl_ref[...], glog)
    tot = _dot_hl_left(be_ref[0], glog)
    mid = _dot_hl_left(be_ref[1], glog)
    qe = (gq * jnp.exp(bcl)).astype(BF16)
    qd = (gq * jnp.exp(bcl - mid)).astype(BF16)
    kp = gk * jnp.exp(mid - bcl)
    kend = (gk * jnp.exp(tot - bcl)).astype(BF16)
    dec = jnp.exp(tot)

    r_k = lax.broadcasted_iota(jnp.int32, (GLA_HEADS * sub, GLA_KW), 0) // sub
    c_k = lax.broadcasted_iota(jnp.int32, (GLA_HEADS * sub, GLA_KW), 1) // GLA_DK
    r_v = lax.broadcasted_iota(jnp.int32, (GLA_HEADS * sub, GLA_W), 0) // sub
    c_v = lax.broadcasted_iota(jnp.int32, (GLA_HEADS * sub, GLA_W), 1) // GLA_DV
    r_s = lax.broadcasted_iota(jnp.int32, (GLA_W, GLA_KW), 0) // GLA_DV
    c_s = lax.broadcasted_iota(jnp.int32, (GLA_W, GLA_KW), 1) // GLA_DK
    causal = (lax.broadcasted_iota(jnp.int32, (sub, GLA_HEADS * sub), 1) % sub
              <= lax.broadcasted_iota(jnp.int32, (sub, GLA_HEADS * sub), 0))
    sls = [slice(i * sub, (i + 1) * sub) for i in range(n_sub)]
    atts, upds = [], []
    for sl in sls:
        kbd = jnp.where(r_k == c_k, jnp.concatenate([kp[sl]] * GLA_HEADS, axis=0), 0.0)
        atts.append(_dot_nt(qd[sl], kbd))
        upds.append(_dot_tn(v[sl], kend[sl]))
    sts = [st_scr[s]]
    for i in range(n_sub):
        sts.append(sts[i] * dec[i * sub:i * sub + 1, :] + jnp.where(r_s == c_s, upds[i], 0.0))
    st_scr[s] = sts[n_sub]
    outs = []
    for i, sl in enumerate(sls):
        vbd = jnp.where(r_v == c_v, jnp.concatenate([v[sl]] * GLA_HEADS, axis=0), 0.0)
        outs.append(_dot(jnp.where(causal, atts[i], 0.0), vbd) + _dot_nt(qe[sl], sts[i]))
    o = jnp.concatenate(outs, axis=0)
    msq = _dot_hl(o * o, g64_ref[...])
    o_ref[s] = (o * lax.rsqrt(msq + EPS) * gn_ref[...] * _silu(gg_ref[s])).astype(BF16)
    return sts[n_sub]


def _gla_p(gqk, gv, gg, smr, wgp, gb, bl, be, g64, gn, t_real, l):
    nb, t_pad, _ = gv.shape
    nj = t_pad // CHUNK
    grp = math.gcd(nb, MIX_GROUP)
    rows = lambda w_: pl.BlockSpec((grp, CHUNK, w_), lambda b, j: (b, j, 0))
    return pl.pallas_call(
        functools.partial(_gla_p_kernel, t_real=t_real, grp=grp),
        out_shape=[jax.ShapeDtypeStruct((nb, t_pad, GLA_W), BF16),
                   jax.ShapeDtypeStruct((nb, GLA_HEADS, GLA_DK, GLA_DV), F32)],
        grid=(nb // grp, nj),
        in_specs=[rows(2 * GLA_KW), rows(GLA_W), rows(GLA_W), rows(LANES),
                  _lspec((LANES, GLA_KW), l), _lspec((1, GLA_KW), l),
                  _const_spec((CHUNK, CHUNK)), _const_spec((2, CHUNK, CHUNK)),
                  _const_spec((GLA_W, GLA_W)), _lspec((1, GLA_W), l)],
        out_specs=[rows(GLA_W),
                   pl.BlockSpec((grp, GLA_HEADS, GLA_DK, GLA_DV), lambda b, j: (b, 0, 0, 0))],
        scratch_shapes=[pltpu.VMEM((grp, GLA_W, GLA_KW), F32)],
        compiler_params=_params(("parallel", "arbitrary")), name="gla_p")(
            gqk, gv, gg, smr, wgp, gb, bl, be, g64, gn)


def _gla_s_kernel(q_ref, k_ref, v_ref, gg_ref, sm_ref, wgt_ref, gb_ref, gn_ref, st_ref,
                  o_ref, so_ref, eg_scr, acc_scr, *, n_t):
    for t in range(n_t):
        glog = _log_sigmoid(_dot(wgt_ref[...], sm_ref[t]) + gb_ref[...]) * (1.0 / GLA_TAU)
        eg_scr[t] = jnp.exp(glog)
        acc_scr[t] = jnp.zeros((GLA_DV, LANES), F32)

    def body(i, carry):
        states = []
        for u in range(DEC_UNROLL):
            kk = i * DEC_UNROLL + u
            s = st_ref[0, 0, kk]
            row = []
            for t in range(n_t):
                s = s * eg_scr[t, pl.ds(kk, 1), :] + k_ref[t, pl.ds(kk, 1), :] * v_ref[t]
                row.append(q_ref[t, pl.ds(kk, 1), :] * s)
            so_ref[0, 0, kk] = s
            states.append(row)
        for t in range(n_t):
            acc = acc_scr[t]
            for u in range(DEC_UNROLL):
                acc = acc + states[u][t]
            acc_scr[t] = acc
        return carry

    lax.fori_loop(0, GLA_DK // DEC_UNROLL, body, 0)
    for t in range(n_t):
        o = acc_scr[t]
        on = o * lax.rsqrt(jnp.mean(o * o, axis=0, keepdims=True) + EPS)
        o_ref[t] = on * gn_ref[...] * _silu(gg_ref[t])


def _gla_s(pt, wgt, gb_b, gn_b, state, layer):
    n_t = pt.shape[0]
    blk = lambda h_, f: pl.BlockSpec((n_t, h_, LANES), lambda hh: (0, f(hh), 0))
    return pl.pallas_call(
        functools.partial(_gla_s_kernel, n_t=n_t),
        out_shape=[jax.ShapeDtypeStruct((n_t, GLA_W, LANES), F32),
                   jax.ShapeDtypeStruct((1, GLA_HEADS, GLA_DK, GLA_DV, LANES), F32)],
        grid=(GLA_HEADS,),
        in_specs=[blk(GLA_DK, lambda hh: R_GQ // GLA_DK + hh),
                  blk(GLA_DK, lambda hh: R_GK // GLA_DK + hh),
                  blk(GLA_DV, lambda hh: R_GV // GLA_DV + hh),
                  blk(GLA_DV, lambda hh: R_GG // GLA_DV + hh),
                  blk(LANES, lambda hh: R_SM // LANES),
                  pl.BlockSpec((None, GLA_DK, LANES), lambda hh: (layer, hh, 0)),
                  pl.BlockSpec((None, GLA_DK, LANES), lambda hh: (layer, hh, 0)),
                  _lspec((GLA_DV, LANES), layer),
                  pl.BlockSpec((1, 1, GLA_DK, GLA_DV, LANES), lambda hh: (layer, hh, 0, 0, 0))],
        out_specs=[blk(GLA_DV, lambda hh: hh),
                   pl.BlockSpec((1, 1, GLA_DK, GLA_DV, LANES), lambda hh: (0, hh, 0, 0, 0))],
        scratch_shapes=[pltpu.VMEM((n_t, GLA_DK, LANES), F32), pltpu.VMEM((n_t, GLA_DV, LANES), F32)],
        compiler_params=_params(("arbitrary",)), name="gla_s")(
            pt, pt, pt, pt, pt, wgt, gb_b, gn_b, state)


def _tri_consts(n_pages):
    i = np.arange(CHUNK)
    ltri = (i[None, :] <= i[:, None]).astype(np.float32)
    utri = ltri.T
    mstrict = (i[:, None] > i[None, :]).astype(np.float32)
    same = (i[:, None] // GLA_SUB) == (i[None, :] // GLA_SUB)
    bl = (same & (i[None, :] <= i[:, None])).astype(np.float32)
    to_mid = i[None, :] <= (i[:, None] // GLA_SUB) * GLA_SUB + GLA_SUB // 2 - 1
    be = np.stack([same, same & to_mid]).astype(np.float32)
    r = np.arange(n_pages * 8)
    pgsuf = ((r[:, None] % 8 == r[None, :] % 8) & (r[None, :] // 8 > r[:, None] // 8)).astype(np.float32)
    h = np.arange(FOX_W)
    g64 = ((h[:, None] // 64) == (h[None, :] // 64)).astype(np.float32) / 64.0
    cw = np.arange(SSD_W)
    exph = (i[:, None] == SM_DT + cw[None, :] // SSD_HD).astype(np.float32)
    c = lambda a: jnp.asarray(a, BF16)
    return dict(ltri=c(ltri), utri=c(utri), mstrict=c(mstrict), bl=c(bl), be=c(be), pgsuf=c(pgsuf),
                g64=c(g64), exph=c(exph))


def kernel(x_prompt, x_sample, cache_fox_k, cache_fox_v, cache_fox_logf, state_ssm, state_conv,
           state_gla, page_table, meta_tokens, ffn1_norm, ffn1_w_in, ffn1_w_out, mix_norm, w_mix_in,
           fox_q_norm, fox_k_norm, fox_f_bias, ssd_conv_w, ssd_conv_b, ssd_dt_bias, ssd_a_log, ssd_d,
           ssd_norm, gla_w_gate, gla_gate_bias, gla_norm, w_mix_out, ffn2_norm, ffn2_w_in, ffn2_w_out):
    nbp, seq, _ = x_prompt.shape
    nbs, n_t, _ = x_sample.shape
    depth = ffn1_norm.shape[0]
    assert nbs == LANES and seq % CHUNK == 0
    t_real = N_META + seq
    t_pad = -(-t_real // CHUNK) * CHUNK
    n_pool, page_size = cache_fox_k.shape[1], cache_fox_k.shape[2]
    assert page_size == LANES and n_pool % 2 == 0
    n_pages = page_table.shape[1]
    consts = _tri_consts(n_pages)

    xs = jnp.transpose(x_sample, (1, 0, 2)).reshape(n_t * nbs, D_MODEL)

    k_cache = jnp.transpose(cache_fox_k, (0, 1, 3, 4, 2)).reshape(depth, n_pool, FOX_W, page_size)
    v_cache = jnp.transpose(cache_fox_v, (0, 1, 3, 4, 2)).reshape(depth, n_pool, FOX_W, page_size)
    lfpool = jnp.transpose(cache_fox_logf, (0, 1, 3, 2)).reshape(depth, n_pool // 2, 8, page_size)
    ssm_in = jnp.transpose(state_ssm, (0, 2, 3, 4, 1))
    gla_in = jnp.transpose(state_gla, (0, 2, 3, 4, 1))
    conv_in = jnp.transpose(state_conv, (0, 2, 1, 3))
    tbl = page_table.reshape(-1).astype(jnp.int32)

    w1i, w1o = ffn1_w_in.astype(BF16), ffn1_w_out.astype(BF16)
    w2i, w2o = ffn2_w_in.astype(BF16), ffn2_w_out.astype(BF16)
    wmo = w_mix_out.astype(BF16)
    col = [0] + list(IN_SPLITS) + [w_mix_in.shape[-1]]
    part = lambda i: w_mix_in[:, :, col[i]:col[i + 1]]
    fq, fk, fv, ff, sz, sxbc, sdt, gq, gk, gv, glr, gg = [part(i) for i in range(len(IN_SIZES))]
    n_small = FOX_HEADS + SSD_HEADS + GLA_RANK
    wproj = jnp.concatenate([fq, fk, fv, sz, sxbc, gq, gk, gv, gg, ff, sdt, glr,
                             jnp.zeros((depth, D_MODEL, LANES - n_small), F32)], axis=-1).astype(BF16)
    n1, n2, nm = ffn1_norm[:, None], ffn2_norm[:, None], mix_norm[:, None]
    qg = jnp.tile(fox_q_norm, (1, FOX_HEADS))[:, None]
    kg = jnp.tile(fox_k_norm, (1, FOX_HEADS))[:, None]
    a = -jnp.exp(ssd_a_log)
    lanes_pad = lambda v, off: jnp.pad(v, ((0, 0), (off, LANES - off - v.shape[1])))[:, None]
    fb, dtb, arow = lanes_pad(fox_f_bias, SM_F), lanes_pad(ssd_dt_bias, SM_DT), lanes_pad(a, SM_DT)
    acol = jnp.pad(a, ((0, 0), (SM_DT, 16 - SM_DT - SSD_HEADS)))[:, :, None]
    cw = jnp.pad(ssd_conv_w, ((0, 0), (0, 8 - SSD_CONV), (0, 0)))
    cb = ssd_conv_b[:, None]
    d_rep = jnp.repeat(ssd_d, SSD_HD, axis=1)
    dvec, snrm = d_rep[:, None], ssd_norm[:, None]
    wgp = jnp.pad(gla_w_gate, ((0, 0), (SM_LR, LANES - SM_LR - GLA_RANK), (0, 0))).astype(BF16)
    gb = gla_gate_bias[:, None]
    gn = jnp.tile(gla_norm, (1, GLA_HEADS))[:, None]
    on_lanes = lambda v: jnp.broadcast_to(v[:, :, None], v.shape + (LANES,))
    a_b, d_b, snrm_b = on_lanes(a), on_lanes(d_rep), on_lanes(ssd_norm)
    wgt = jnp.pad(jnp.transpose(gla_w_gate, (0, 2, 1)),
                  ((0, 0), (0, 0), (SM_LR, LANES - SM_LR - GLA_RANK))).astype(BF16)
    gb_b, gn_b = on_lanes(gla_gate_bias), on_lanes(gla_norm)

    outs_p = [[] for _ in range(6)]
    outs_s = [[] for _ in range(6)]
    ssm_all = None
    for l in range(depth):
        last = l == depth - 1
        if l == 0:
            meta = _ffn(meta_tokens, N_META, (n1, w1i, w1o, 0))
            frame = jnp.concatenate([jnp.broadcast_to(meta[None], (nbp, N_META, D_MODEL)),
                                     jnp.zeros((nbp, t_pad - N_META, D_MODEL), F32)], axis=1)
            if seq % FFN_TM == 0:
                xp = _ffn_into(x_prompt, frame, FFN_TM, (n1, w1i, w1o, 0), N_META)
            else:
                xp = lax.dynamic_update_slice(
                    frame, _ffn(x_prompt.reshape(nbp * seq, D_MODEL), CHUNK, (n1, w1i, w1o, 0)).reshape(
                        nbp, seq, D_MODEL), (0, N_META, 0))
            xp = xp.reshape(nbp * t_pad, D_MODEL)
        else:
            xp = _ffn(xp, FFN_TM, (n1, w1i, w1o, l))
        (qat, kaug, kt, vt, vt16, z, xbc, gqk, gvv, ggg, smr, smt, convp) = _proj_p(
            xp.reshape(nbp, t_pad, D_MODEL), nm, wproj, qg, kg, fb, dtb, cw, cb, consts["g64"],
            consts["ltri"], t_real, l)
        fox_o = _fox_p(qat, kaug, vt16)
        ssd_o, ssm_p = _ssd_p(z, xbc, smr, smt, arow, acol, consts["ltri"], consts["utri"],
                              consts["exph"], dvec, snrm, l)
        gla_o, gla_p = _gla_p(gqk, gvv, ggg, smr, wgp, gb, consts["bl"], consts["be"], consts["g64"],
                              gn, t_real, l)
        mix = ("rows", fox_o.reshape(-1, FOX_W), ssd_o.reshape(-1, SSD_W), gla_o.reshape(-1, GLA_W), wmo, l)
        window = (nbp, t_pad, N_META, seq) if last and seq % FFN_TM == 0 else None
        xp = _ffn(xp, FFN_TM, (n2, w2i, w2o, l), mix=mix, window=window)
        outs_p[0].append(kt.reshape(nbp, FOX_HEADS, FOX_HD, t_pad))
        outs_p[1].append(vt.reshape(nbp, FOX_HEADS, FOX_HD, t_pad))
        outs_p[2].append(jnp.transpose(smt[:, 0:FOX_HEADS, :t_real], (0, 2, 1)))
        outs_p[3].append(ssm_p)
        outs_p[4].append(convp[:, 8 - (SSD_CONV - 1):])
        outs_p[5].append(gla_p)

        xs = _ffn(xs, xs.shape[0], (n1, w1i, w1o, l))
        qs, krs, vrs, pt, cum, convs = _proj_s(xs, nm, wproj, qg, kg, fb, dtb, cw, cb, consts["g64"],
                                               conv_in, l)
        fox_os = _fox_s(tbl, qs, krs, vrs, cum, lfpool, consts["mstrict"], consts["pgsuf"],
                        k_cache, v_cache, l)
        ssd_ot, ssm_all = _ssd_s(pt, a_b, d_b, snrm_b, ssm_in, l, ssm_all)
        gla_ot, gla_s = _gla_s(pt, wgt, gb_b, gn_b, gla_in, l)
        xs = _ffn(xs, xs.shape[0], (n2, w2i, w2o, l), mix=("features", fox_os, ssd_ot, gla_ot, wmo, l))
        kts = pt[:, R_K:R_K + FOX_W].reshape(n_t, FOX_HEADS, FOX_HD, nbs)
        vts = pt[:, R_V:R_V + FOX_W].reshape(n_t, FOX_HEADS, FOX_HD, nbs)
        outs_s[0].append(jnp.transpose(kts, (3, 0, 1, 2)))
        outs_s[1].append(jnp.transpose(vts, (3, 0, 1, 2)))
        outs_s[2].append(jnp.transpose(pt[:, R_SM + SM_F:R_SM + SM_F + FOX_HEADS], (2, 0, 1)))
        outs_s[4].append(jnp.transpose(convs, (1, 0, 2)))
        outs_s[5].append(jnp.transpose(gla_s[0], (3, 0, 1, 2)))

    y_prompt = xp if xp.ndim == 3 else xp.reshape(nbp, t_pad, D_MODEL)[:, N_META:t_real]
    y_sample = jnp.transpose(xs.reshape(n_t, nbs, D_MODEL), (1, 0, 2))
    k_p = jnp.transpose(jnp.stack(outs_p[0])[..., :t_real], (0, 1, 4, 2, 3))
    v_p = jnp.transpose(jnp.stack(outs_p[1])[..., :t_real], (0, 1, 4, 2, 3))
    lf_p, ssm_p, conv_p, gla_p = [jnp.stack(a) for a in outs_p[2:]]
    k_s, v_s, lf_s, conv_s, gla_s = [jnp.stack(outs_s[i]) for i in (0, 1, 2, 4, 5)]
    ssm_s = jnp.transpose(ssm_all, (0, 4, 1, 2, 3))
    return (y_prompt, y_sample, k_p, v_p, lf_p, ssm_p, conv_p, gla_p, k_s, v_s, lf_s, ssm_s, conv_s, gla_s)
```

```python
import functools
import math

import jax
import jax.numpy as jnp
import numpy as np
from jax import lax
from jax.experimental import pallas as pl
from jax.experimental.pallas import tpu as pltpu

F32 = jnp.float32
BF16 = jnp.bfloat16

D_MODEL = 1024
N_META = 16
D_FF = 2816
EPS = 1e-6
FOX_HEADS = 4
FOX_HD = 64
FOX_W = FOX_HEADS * FOX_HD
SSD_HEADS = 8
SSD_HD = 64
SSD_W = SSD_HEADS * SSD_HD
SSD_GROUPS = 2
SSD_STATE = 64
SSD_CONV = 4
SSD_BC = SSD_GROUPS * SSD_STATE
SSD_CONV_DIM = SSD_W + 2 * SSD_BC
GLA_HEADS = 4
GLA_DK = 32
GLA_DV = 64
GLA_KW = GLA_HEADS * GLA_DK
GLA_W = GLA_HEADS * GLA_DV
GLA_RANK = 16
GLA_TAU = 16.0
D_MIX = FOX_W + SSD_W + GLA_W
IN_SIZES = (FOX_W, FOX_W, FOX_W, FOX_HEADS, SSD_W, SSD_CONV_DIM, SSD_HEADS,
            GLA_KW, GLA_KW, GLA_W, GLA_RANK, GLA_W)
IN_SPLITS = tuple(int(v) for v in np.cumsum(IN_SIZES)[:-1])

LOG2E = 1.4426950408889634
FOX_AUG = FOX_HEADS * 128

LANES = 128
CHUNK = 128
GLA_SUB = 32
FOX_KW = 512
FOX_QW = 512
FFN_CK = 256
FFN_TM = 512
PROJ_GROUP = 8
DEC_UNROLL = 4
MIX_GROUP = 8
VMEM_LIMIT = 60 * 1024 * 1024

C_Q, C_K, C_V = 0, 256, 512
C_Z = 768
C_XBC = 1280
C_GQ, C_GK, C_GV, C_GG = 2048, 2176, 2304, 2560
C_SM = 2816
N_PROJ = 2944
SM_F = 0
SM_DT = 4
SM_LR = 12
R_K, R_V, R_Z, R_XBC = 0, 256, 512, 1024
R_GQ, R_GK, R_GV, R_GG, R_SM = 1792, 1920, 2048, 2304, 2560
N_PT = 2688


def _dot(a, b):
    return jnp.dot(a.astype(BF16), b.astype(BF16), preferred_element_type=F32)


def _dot_nt(a, b):
    return lax.dot_general(a.astype(BF16), b.astype(BF16), (((1,), (1,)), ((), ())),
                           preferred_element_type=F32)


def _dot_tn(a, b):
    return lax.dot_general(a.astype(BF16), b.astype(BF16), (((0,), (0,)), ((), ())),
                           preferred_element_type=F32)


def _split(a):
    hi = a.astype(BF16)
    lo = (a - hi.astype(F32)).astype(BF16)
    return hi, lo


def _dot_hl(a, m):
    hi, lo = _split(a)
    return (jnp.dot(hi, m, preferred_element_type=F32) + jnp.dot(lo, m, preferred_element_type=F32))


def _dot_hl_left(m, a):
    hi, lo = _split(a)
    return (jnp.dot(m, hi, preferred_element_type=F32) + jnp.dot(m, lo, preferred_element_type=F32))


def _silu(x):
    return x * jax.nn.sigmoid(x)


def _softplus(x):
    return jnp.maximum(x, 0.0) + jnp.log1p(jnp.exp(-jnp.abs(x)))


def _log_sigmoid(x):
    return jnp.minimum(x, 0.0) - jnp.log1p(jnp.exp(-jnp.abs(x)))


def _rms(x, g):
    return x * lax.rsqrt(jnp.mean(x * x, axis=-1, keepdims=True) + EPS) * g


def _const_spec(shape):
    n = len(shape)
    return pl.BlockSpec(shape, lambda *_: (0,) * n)


def _lspec(shape, l):
    n = len(shape)
    return pl.BlockSpec((None,) + tuple(shape), lambda *_: (l,) + (0,) * n)


def _wspec(shape, l):
    n = len(shape)
    return pl.BlockSpec((None,) + tuple(shape), lambda *_: (l,) + (0,) * n, pipeline_mode=pl.Buffered(1))


def _params(sem):
    return pltpu.CompilerParams(dimension_semantics=sem, vmem_limit_bytes=VMEM_LIMIT)


def _swiglu_half(x, g_ref, win_ref, wout_ref):
    h = _rms(x, g_ref[...]).astype(BF16)
    acc = None
    for c in range(D_FF // FFN_CK):
        g = jnp.dot(h, win_ref[:, c * FFN_CK:(c + 1) * FFN_CK], preferred_element_type=F32)
        u = jnp.dot(h, win_ref[:, D_FF + c * FFN_CK:D_FF + (c + 1) * FFN_CK],
                    preferred_element_type=F32)
        a = (_silu(g) * u).astype(BF16)
        part = jnp.dot(a, wout_ref[c * FFN_CK:(c + 1) * FFN_CK, :], preferred_element_type=F32)
        acc = part if acc is None else acc + part
    return x + 0.5 * acc


def _ffn_kernel(*refs, mix, lead):
    rows = (lambda r: r[0]) if lead else (lambda r: r[...])
    x = rows(refs[0])
    if mix:
        fo_ref, so_ref, go_ref, wmo_ref = refs[1:5]
        if mix == "rows":
            fo, so, go = rows(fo_ref), rows(so_ref), rows(go_ref)
        else:
            n_t = fo_ref.shape[0]
            fo = fo_ref[...].reshape(n_t * LANES, FOX_W)
            so = jnp.concatenate([so_ref[t].T for t in range(n_t)], axis=0)
            go = jnp.concatenate([go_ref[t].T for t in range(n_t)], axis=0)
        x = x + _dot(fo, wmo_ref[0:FOX_W, :])
        x = x + _dot(so, wmo_ref[FOX_W:FOX_W + SSD_W, :])
        x = x + _dot(go, wmo_ref[FOX_W + SSD_W:D_MIX, :])
    g_ref, win_ref, wout_ref, o_ref = refs[-4:]
    y = _swiglu_half(x, g_ref, win_ref, wout_ref)
    if len(o_ref.shape) == 3:
        o_ref[0] = y
    else:
        o_ref[...] = y


def _ffn_into(x3, dest, tm, ffn, row0):
    nseq, t_in, _ = x3.shape
    g, win, wout, l = ffn
    out_spec = pl.BlockSpec((pl.Element(1), pl.Element(tm), pl.Element(D_MODEL)),
                            lambda b, i: (b, pl.multiple_of(row0 + i * tm, math.gcd(row0, tm)), 0))
    return pl.pallas_call(
        functools.partial(_ffn_kernel, mix=None, lead=True),
        out_shape=jax.ShapeDtypeStruct(dest.shape, F32), grid=(nseq, t_in // tm),
        in_specs=[pl.BlockSpec((1, tm, D_MODEL), lambda b, i: (b, i, 0)), pl.BlockSpec(memory_space=pl.ANY),
                  _lspec((1, D_MODEL), l), _wspec((D_MODEL, 2 * D_FF), l), _wspec((D_FF, D_MODEL), l)],
        out_specs=out_spec, input_output_aliases={1: 0},
        compiler_params=_params(("parallel", "parallel")), name="ffn")(x3, dest, g, win, wout)


def _ffn(x, tm, ffn, mix=None, window=None):
    if window:
        nseq, t_in, row0, t_out = window
        grid = (nseq, t_out // tm)
        view = lambda a: a.reshape(nseq, t_in, a.shape[-1])
        row = lambda w: pl.BlockSpec((pl.Element(1), pl.Element(tm), pl.Element(w)),
                                     lambda b, i: (b, pl.multiple_of(row0 + i * tm, math.gcd(row0, tm)), 0))
        out_shape = jax.ShapeDtypeStruct((nseq, t_out, D_MODEL), F32)
        out_spec = pl.BlockSpec((None, tm, D_MODEL), lambda b, i: (b, i, 0))
    else:
        rows = x.shape[0]
        grid = (rows // tm,)
        view = lambda a: a
        row = lambda w: pl.BlockSpec((tm, w), lambda i: (i, 0))
        out_shape = jax.ShapeDtypeStruct((rows, D_MODEL), F32)
        out_spec = row(D_MODEL)
    args, specs = [view(x)], [row(D_MODEL)]
    if mix:
        kind, fo, so, go, wmo, l = mix
        if kind == "rows":
            args += [view(fo), view(so), view(go)]
            specs += [row(FOX_W), row(SSD_W), row(GLA_W)]
        else:
            assert not window and tm == x.shape[0]
            args += [fo, so, go]
            specs += [pl.BlockSpec(a.shape, lambda i: (0, 0, 0)) for a in (fo, so, go)]
        args.append(wmo)
        specs.append(_wspec((D_MIX, D_MODEL), l))
    g, win, wout, l = ffn
    args += [g, win, wout]
    specs += [_lspec((1, D_MODEL), l), _wspec((D_MODEL, 2 * D_FF), l), _wspec((D_FF, D_MODEL), l)]
    return pl.pallas_call(
        functools.partial(_ffn_kernel, mix=mix[0] if mix else None, lead=bool(window)),
        out_shape=out_shape, grid=grid,
        in_specs=specs, out_specs=out_spec,
        compiler_params=_params(("parallel",) * len(grid)), name="mix_ffn" if mix else "ffn")(*args)


def _head_norm(x, gain_row, g64_ref):
    msq = _dot_hl(x * x, g64_ref[...])
    return x * lax.rsqrt(msq + EPS) * gain_row


def _small_block(sm, fb_ref, dtb_ref):
    logf = _log_sigmoid(sm + fb_ref[...])
    dt = _softplus(sm + dtb_ref[...])
    return logf, dt


def _proj_p_kernel(x_ref, nrm_ref, w_ref, qg_ref, kg_ref, fb_ref, dtb_ref, cw_ref, cb_ref, g64_ref,
                   ltri_ref,
                   qat_ref, ka_ref, kt_ref, vt_ref, vt16_ref, z_ref, xbc_ref, gqk_ref, gv_ref, gg_ref,
                   smr_ref, smt_ref, convp_ref,
                   xb_scr, carry_scr, *, t_real, grp):
    j = pl.program_id(1)

    @pl.when(j == 0)
    def _():
        xb_scr[:, 0:8, :] = jnp.zeros((grp, 8, SSD_CONV_DIM), F32)
        carry_scr[...] = jnp.zeros((grp, 8, LANES), F32)

    h = _rms(x_ref[...].reshape(grp * CHUNK, D_MODEL), nrm_ref[...]).astype(BF16)
    proj = lambda c0, n: jnp.dot(h, w_ref[:, c0:c0 + n], preferred_element_type=F32)
    seq = lambda a, g: a[g * CHUNK:(g + 1) * CHUNK]

    sm_all = proj(C_SM, LANES)
    lane = lax.broadcasted_iota(jnp.int32, (CHUNK, LANES), 1)
    row = lax.broadcasted_iota(jnp.int32, (CHUNK, LANES), 0)
    logf_all, dt_all = _small_block(sm_all, fb_ref, dtb_ref)
    f_rows = []
    for g in range(grp):
        logf = jnp.where(lane < SM_DT, seq(logf_all, g), 0.0)
        dt = jnp.where(j * CHUNK + row < t_real, seq(dt_all, g), 0.0)
        f = _dot_hl_left(ltri_ref[...], logf) + carry_scr[g, 0:1, :]
        carry_scr[g] = jnp.broadcast_to(f[CHUNK - 1:CHUNK, :], (8, LANES))
        smr_ref[g] = jnp.where(lane < SM_LR, dt, seq(sm_all, g))
        smt_ref[g] = jnp.where(lane < SM_DT, logf, dt).T[0:16, :]
        f_rows.append(f)

    q_all = _head_norm(proj(C_Q, FOX_W), qg_ref[...], g64_ref) * (FOX_HD ** -0.5 * LOG2E)
    k_all = _head_norm(proj(C_K, FOX_W), kg_ref[...], g64_ref)
    v_all = proj(C_V, FOX_W)
    l64 = lax.broadcasted_iota(jnp.int32, (CHUNK, FOX_HD), 1)
    for g in range(grp):
        q, k = seq(q_all, g), seq(k_all, g)
        qa, ka = [], []
        for hh in range(FOX_HEADS):
            y = f_rows[g][:, hh:hh + 1] * LOG2E
            hi = y.astype(BF16).astype(F32)
            mid = (y - hi).astype(BF16).astype(F32)
            lo = y - hi - mid
            parts = jnp.where(l64 == 0, hi, jnp.where(l64 == 1, mid, lo))
            ext_q = jnp.where(l64 < 3, parts, jnp.where(l64 < 6, 1.0, 0.0))
            parts = jnp.where(l64 == 3, hi, jnp.where(l64 == 4, mid, lo))
            ext_k = jnp.where(l64 < 3, 1.0, jnp.where(l64 < 6, -parts, 0.0))
            qa += [q[:, hh * FOX_HD:(hh + 1) * FOX_HD], ext_q]
            ka += [k[:, hh * FOX_HD:(hh + 1) * FOX_HD], ext_k]
        qat_ref[g] = jnp.concatenate(qa, axis=1).T.astype(BF16)
        ka_ref[g] = jnp.concatenate(ka, axis=1).astype(BF16)
        kt_ref[g] = k.T
        vt = seq(v_all, g).T
        vt_ref[g] = vt
        vt16_ref[g] = vt.astype(BF16)

    xbc_all = proj(C_XBC, SSD_CONV_DIM)
    r_end = t_real - (t_real - 1) // CHUNK * CHUNK
    for g in range(grp):
        xb_scr[g, 8:8 + CHUNK, :] = seq(xbc_all, g)
        conv = cb_ref[...]
        for w in range(SSD_CONV):
            conv = conv + xb_scr[g, pl.ds(8 - (SSD_CONV - 1) + w, CHUNK), :] * cw_ref[w:w + 1, :]
        xbc_ref[g] = _silu(conv)
        convp_ref[g] = xb_scr[g, r_end:r_end + 8, :]
        xb_scr[g, 0:8, :] = xb_scr[g, CHUNK:CHUNK + 8, :]

    z_ref[...] = proj(C_Z, SSD_W).reshape(grp, CHUNK, SSD_W)
    gqk_ref[...] = proj(C_GQ, 2 * GLA_KW).reshape(grp, CHUNK, 2 * GLA_KW)
    gv_ref[...] = proj(C_GV, GLA_W).reshape(grp, CHUNK, GLA_W)
    gg_ref[...] = proj(C_GG, GLA_W).reshape(grp, CHUNK, GLA_W)


def _proj_p(x3, nrm, w, qg, kg, fb, dtb, cw, cb, g64, ltri, t_real, l):
    nb, t_pad, _ = x3.shape
    nj = t_pad // CHUNK
    grp = math.gcd(nb, PROJ_GROUP)
    rows = lambda w_: pl.BlockSpec((grp, CHUNK, w_), lambda b, j: (b, j, 0))
    cols = lambda h_: pl.BlockSpec((grp, h_, CHUNK), lambda b, j: (b, 0, j))
    out_shape = [
        jax.ShapeDtypeStruct((nb, FOX_AUG, t_pad), BF16),
        jax.ShapeDtypeStruct((nb, t_pad, FOX_AUG), BF16),
        jax.ShapeDtypeStruct((nb, FOX_W, t_pad), F32),
        jax.ShapeDtypeStruct((nb, FOX_W, t_pad), F32),
        jax.ShapeDtypeStruct((nb, FOX_W, t_pad), BF16),
        jax.ShapeDtypeStruct((nb, t_pad, SSD_W), F32),
        jax.ShapeDtypeStruct((nb, t_pad, SSD_CONV_DIM), F32),
        jax.ShapeDtypeStruct((nb, t_pad, 2 * GLA_KW), F32),
        jax.ShapeDtypeStruct((nb, t_pad, GLA_W), F32),
        jax.ShapeDtypeStruct((nb, t_pad, GLA_W), F32),
        jax.ShapeDtypeStruct((nb, t_pad, LANES), F32),
        jax.ShapeDtypeStruct((nb, 16, t_pad), F32),
        jax.ShapeDtypeStruct((nb, 8, SSD_CONV_DIM), F32),
    ]
    out_specs = [cols(FOX_AUG), rows(FOX_AUG), cols(FOX_W), cols(FOX_W), cols(FOX_W), rows(SSD_W),
                 rows(SSD_CONV_DIM), rows(2 * GLA_KW), rows(GLA_W), rows(GLA_W), rows(LANES), cols(16),
                 pl.BlockSpec((grp, 8, SSD_CONV_DIM), lambda b, j: (b, 0, 0))]
    in_specs = [rows(D_MODEL), _lspec((1, D_MODEL), l), _wspec((D_MODEL, N_PROJ), l),
                _lspec((1, FOX_W), l), _lspec((1, FOX_W), l), _lspec((1, LANES), l),
                _lspec((1, LANES), l), _lspec((8, SSD_CONV_DIM), l), _lspec((1, SSD_CONV_DIM), l),
                _const_spec((FOX_W, FOX_W)), _const_spec((CHUNK, CHUNK))]
    return pl.pallas_call(
        functools.partial(_proj_p_kernel, t_real=t_real, grp=grp), out_shape=out_shape,
        grid=(nb // grp, nj), in_specs=in_specs, out_specs=out_specs,
        scratch_shapes=[pltpu.VMEM((grp, 8 + CHUNK, SSD_CONV_DIM), F32),
                        pltpu.VMEM((grp, 8, LANES), F32)],
        compiler_params=_params(("parallel", "arbitrary")), name="proj_p")(
            x3, nrm, w, qg, kg, fb, dtb, cw, cb, g64, ltri)


def _proj_s_kernel(x_ref, nrm_ref, w_ref, qg_ref, kg_ref, fb_ref, dtb_ref, cw_ref, cb_ref, g64_ref,
                   cst_ref,
                   q_ref, kr_ref, vr_ref, pt_ref, cum_ref, convs_ref,
                   hist_scr, carry_scr, *, n_t):
    t = pl.program_id(0)

    @pl.when(t == 0)
    def _():
        hist_scr[0:SSD_CONV - 1] = cst_ref[...]
        carry_scr[...] = jnp.zeros((LANES, LANES), F32)

    h = _rms(x_ref[...], nrm_ref[...]).astype(BF16)
    p = jnp.dot(h, w_ref[...], preferred_element_type=F32)

    q_ref[0] = _head_norm(p[:, C_Q:C_Q + FOX_W], qg_ref[...], g64_ref) * (FOX_HD ** -0.5)
    k = _head_norm(p[:, C_K:C_K + FOX_W], kg_ref[...], g64_ref)
    v = p[:, C_V:C_V + FOX_W]
    kr_ref[0] = k
    vr_ref[0] = v
    for c in range(FOX_W // LANES):
        pt_ref[0, R_K + c * LANES:R_K + (c + 1) * LANES, :] = k[:, c * LANES:(c + 1) * LANES].T
        pt_ref[0, R_V + c * LANES:R_V + (c + 1) * LANES, :] = v[:, c * LANES:(c + 1) * LANES].T

    hist_scr[pl.ds(SSD_CONV - 1 + t, 1)] = p[:, C_XBC:C_XBC + SSD_CONV_DIM][None]
    conv = cb_ref[...]
    for w in range(SSD_CONV):
        conv = conv + hist_scr[t + w] * cw_ref[w:w + 1, :]
    xbc = _silu(conv)

    @pl.when(t == n_t - 1)
    def _():
        convs_ref[...] = hist_scr[n_t:n_t + SSD_CONV - 1]

    sm = p[:, C_SM:C_SM + LANES]
    lane = lax.broadcasted_iota(jnp.int32, (LANES, LANES), 1)
    logf, dt = _small_block(sm, fb_ref, dtb_ref)
    cum = carry_scr[...] + jnp.where(lane < SM_DT, logf, 0.0)
    carry_scr[...] = cum
    cum_ref[0] = cum
    smc = jnp.where(lane < SM_DT, logf, jnp.where(lane < SM_LR, dt, sm))

    def put_t(r0, val):
        for c in range(val.shape[1] // LANES):
            pt_ref[0, r0 + c * LANES:r0 + (c + 1) * LANES, :] = val[:, c * LANES:(c + 1) * LANES].T

    put_t(R_Z, p[:, C_Z:C_Z + SSD_W])
    put_t(R_XBC, xbc)
    put_t(R_GQ, p[:, C_GQ:C_GQ + GLA_KW] * (GLA_DK ** -0.5))
    put_t(R_GK, p[:, C_GK:C_GK + GLA_KW])
    put_t(R_GV, p[:, C_GV:C_GV + GLA_W])
    put_t(R_GG, p[:, C_GG:C_GG + GLA_W])
    put_t(R_SM, smc)


def _proj_s(x, nrm, w, qg, kg, fb, dtb, cw, cb, g64, conv_state, l):
    n_t = x.shape[0] // LANES
    per_t = lambda a, b: pl.BlockSpec((1, a, b), lambda t: (t, 0, 0))
    out_shape = [
        jax.ShapeDtypeStruct((n_t, LANES, FOX_W), F32),
        jax.ShapeDtypeStruct((n_t, LANES, FOX_W), F32),
        jax.ShapeDtypeStruct((n_t, LANES, FOX_W), F32),
        jax.ShapeDtypeStruct((n_t, N_PT, LANES), F32),
        jax.ShapeDtypeStruct((n_t, LANES, LANES), F32),
        jax.ShapeDtypeStruct((SSD_CONV - 1, LANES, SSD_CONV_DIM), F32),
    ]
    out_specs = [per_t(LANES, FOX_W), per_t(LANES, FOX_W), per_t(LANES, FOX_W), per_t(N_PT, LANES),
                 per_t(LANES, LANES), _const_spec((SSD_CONV - 1, LANES, SSD_CONV_DIM))]
    in_specs = [pl.BlockSpec((LANES, D_MODEL), lambda t: (t, 0)), _lspec((1, D_MODEL), l),
                _wspec((D_MODEL, N_PROJ), l), _lspec((1, FOX_W), l), _lspec((1, FOX_W), l),
                _lspec((1, LANES), l), _lspec((1, LANES), l), _lspec((8, SSD_CONV_DIM), l),
                _lspec((1, SSD_CONV_DIM), l), _const_spec((FOX_W, FOX_W)),
                _lspec((SSD_CONV - 1, LANES, SSD_CONV_DIM), l)]
    return pl.pallas_call(
        functools.partial(_proj_s_kernel, n_t=n_t), out_shape=out_shape, grid=(n_t,),
        in_specs=in_specs, out_specs=out_specs,
        scratch_shapes=[pltpu.VMEM((n_t + SSD_CONV - 1, LANES, SSD_CONV_DIM), F32),
                        pltpu.VMEM((LANES, LANES), F32)],
        compiler_params=_params(("arbitrary",)), name="proj_s")(
            x, nrm, w, qg, kg, fb, dtb, cw, cb, g64, conv_state)


def _fox_p_kernel(qat_ref, ka_ref, vt_ref, o_ref, *, kw, qw, tile0, tails):
    qi = pl.program_id(1) + tile0
    nsub = kw // qw
    n_full = qi // nsub

    def scores(start, width):
        return tuple(jnp.dot(ka_ref[0, pl.ds(start, width), h * LANES:(h + 1) * LANES],
                             qat_ref[0, h * LANES:(h + 1) * LANES, :], preferred_element_type=F32)
                     for h in range(FOX_HEADS))

    def update(ss, start, width, carry, masked):
        prs, stats = [], []
        for h in range(FOX_HEADS):
            m, l, _ = carry[h]
            s = ss[h]
            if masked:
                kpos = lax.broadcasted_iota(jnp.int32, (width, qw), 0) + start
                qpos = lax.broadcasted_iota(jnp.int32, (width, qw), 1) + qi * qw
                s = jnp.where(kpos <= qpos, s, -jnp.inf)
            m_new = jnp.maximum(m, jnp.max(s, axis=0, keepdims=True))
            alpha = jnp.exp2(m - m_new)
            pr = jnp.exp2(s - m_new)
            stats.append((m_new, alpha * l + jnp.sum(pr, axis=0, keepdims=True), alpha))
            prs.append(pr.astype(BF16))
        out = []
        for h in range(FOX_HEADS):
            vt = vt_ref[0, h * FOX_HD:(h + 1) * FOX_HD, pl.ds(start, width)]
            m_new, l, alpha = stats[h]
            out.append((m_new, l, alpha * carry[h][2] + jnp.dot(vt, prs[h], preferred_element_type=F32)))
        return tuple(out)

    init = tuple((jnp.full((1, qw), -jnp.inf, F32), jnp.zeros((1, qw), F32),
                  jnp.zeros((FOX_HD, qw), F32)) for _ in range(FOX_HEADS))

    def body(i, carry):
        at = pl.multiple_of(i * kw, kw)
        return update(scores(at, kw), at, kw, carry, False)

    carry = lax.fori_loop(0, n_full, body, init)
    start = pl.multiple_of(n_full * kw, kw)
    for r in tails:
        @pl.when(lax.rem(qi, nsub) == r)
        def _(r=r):
            width = (r + 1) * qw
            fin = update(scores(start, width), start, width, carry, True)
            o_t = jnp.concatenate([acc / l for (_, l, acc) in fin], axis=0)
            o_ref[0] = o_t.T.astype(BF16)


def _fox_p_tiles(qat, ka, vt16, qw, tile0, n_tiles, kw):
    nb, _, t_pad = qat.shape
    tails = sorted({(tile0 + i) % (kw // qw) for i in range(n_tiles)})
    return pl.pallas_call(
        functools.partial(_fox_p_kernel, kw=kw, qw=qw, tile0=tile0, tails=tails),
        out_shape=jax.ShapeDtypeStruct((nb, n_tiles * qw, FOX_W), BF16), grid=(nb, n_tiles),
        in_specs=[pl.BlockSpec((1, FOX_AUG, qw), lambda b, i: (b, 0, i + tile0)),
                  pl.BlockSpec((1, t_pad, FOX_AUG), lambda b, i: (b, 0, 0)),
                  pl.BlockSpec((1, FOX_W, t_pad), lambda b, i: (b, 0, 0))],
        out_specs=pl.BlockSpec((1, qw, FOX_W), lambda b, i: (b, i, 0)),
        compiler_params=_params(("parallel", "arbitrary")), name="fox_p")(qat, ka, vt16)


def _fox_p(qat, ka, vt16):
    t_pad = qat.shape[2]
    n_wide = t_pad // FOX_QW
    rest = (t_pad - n_wide * FOX_QW) // CHUNK
    parts = []
    if n_wide:
        parts.append(_fox_p_tiles(qat, ka, vt16, FOX_QW, 0, n_wide, FOX_KW))
    if rest:
        kw_rest = n_wide * FOX_QW if n_wide else FOX_KW
        parts.append(_fox_p_tiles(qat, ka, vt16, CHUNK, n_wide * FOX_QW // CHUNK, rest, kw_rest))
    return parts[0] if len(parts) == 1 else jnp.concatenate(parts, axis=1)


def _fox_s_kernel(tbl_ref, q_ref, kr_ref, vr_ref, cum_ref, lf_ref, mstrict_ref, pgsuf_ref,
                  k_hbm, v_hbm, o_ref, kbuf, vbuf, lfst, sem, *, layer, n_pages, n_t):
    b = pl.program_id(0)
    nb = pl.num_programs(0)
    slot = lax.rem(b, 2)

    def copies(seq, sl):
        out = []
        for pg in range(n_pages):
            page = tbl_ref[seq * n_pages + pg]
            out.append(pltpu.make_async_copy(k_hbm.at[layer, page], kbuf.at[sl, pg], sem.at[0, sl]))
            out.append(pltpu.make_async_copy(v_hbm.at[layer, page], vbuf.at[sl, pg], sem.at[1, sl]))
        return out

    @pl.when(b == 0)
    def _():
        lfst[...] = jnp.zeros(lfst.shape, F32)
        for i, c in enumerate(copies(0, 0)):
            c.start(priority=i % 2)

    @pl.when(b + 1 < nb)
    def _():
        for i, c in enumerate(copies(b + 1, 1 - slot)):
            c.start(priority=i % 2)

    for pg in range(n_pages):
        page = tbl_ref[b * n_pages + pg]
        lfst[pg * 8:pg * 8 + FOX_HEADS, :] = lf_ref[0, page // 2, pl.ds(lax.rem(page, 2) * FOX_HEADS,
                                                                      FOX_HEADS), :]
    lf = lfst[...]
    within = _dot_hl(lf, mstrict_ref[...])
    tot = jnp.broadcast_to(jnp.sum(lf, axis=-1, keepdims=True), lf.shape)
    dsuf = within + _dot_hl_left(pgsuf_ref[...], tot)

    lane = lax.broadcasted_iota(jnp.int32, (8, FOX_W), 1)
    rowh = lax.broadcasted_iota(jnp.int32, (8, FOX_W), 0)
    headmask = (lane // FOX_HD) == rowh
    qexp, cumcol = [], []
    cum_rows = jnp.concatenate([cum_ref[t, pl.ds(b, 1), :] for t in range(n_t)]
                               + [jnp.zeros((LANES - n_t, LANES), F32)], axis=0)
    cum_t = cum_rows.T[0:8, 0:8]
    for t in range(n_t):
        qrow = q_ref[t, pl.ds(b, 1), :]
        qexp.append(jnp.where(headmask, jnp.broadcast_to(qrow, (8, FOX_W)), 0.0))
        cumcol.append(cum_t[:, t:t + 1])
    qexp = jnp.concatenate(qexp, axis=0).astype(BF16)
    cumcol = jnp.concatenate(cumcol, axis=0)
    rows = 8 * n_t

    kn = jnp.concatenate([kr_ref[t, pl.ds(b, 1), :] for t in range(n_t)]
                         + [jnp.zeros((8 - n_t, FOX_W), F32)], axis=0)
    vn = jnp.concatenate([vr_ref[t, pl.ds(b, 1), :] for t in range(n_t)]
                         + [jnp.zeros((8 - n_t, FOX_W), F32)], axis=0)
    s_new = _dot_nt(qexp, kn) + cumcol - jnp.concatenate([cum_t] * n_t, axis=0)
    tq = lax.broadcasted_iota(jnp.int32, (rows, 8), 0) // 8
    tk = lax.broadcasted_iota(jnp.int32, (rows, 8), 1)
    s_new = jnp.where(tk <= tq, s_new, -jnp.inf)

    for c in copies(b, slot):
        c.wait()

    s_pg = []
    for pg in range(n_pages):
        bias = jnp.concatenate([dsuf[pg * 8:(pg + 1) * 8, :]] * n_t, axis=0) + cumcol
        s_pg.append(_dot(qexp, kbuf[slot, pg]) + bias)
    m = s_pg[0]
    for s in s_pg[1:]:
        m = jnp.maximum(m, s)
    m = jnp.maximum(jnp.max(m, axis=-1, keepdims=True), jnp.max(s_new, axis=-1, keepdims=True))
    p_new = jnp.exp(s_new - m)
    l = jnp.sum(p_new, axis=-1, keepdims=True)
    acc = _dot(p_new, vn)
    lsum = None
    for pg in range(n_pages):
        pr = jnp.exp(s_pg[pg] - m)
        lsum = pr if lsum is None else lsum + pr
        acc = acc + _dot_nt(pr, vbuf[slot, pg])
    l = l + jnp.sum(lsum, axis=-1, keepdims=True)
    o = acc / l
    for t in range(n_t):
        ot = jnp.where(headmask, o[t * 8:(t + 1) * 8, :], 0.0)
        o_ref[t, pl.ds(b, 1), :] = jnp.sum(ot, axis=0, keepdims=True)


def _fox_s(tbl, q, kr, vr, cum, lfpool, mstrict, pgsuf, k_cache, v_cache, layer):
    n_t, nb, _ = q.shape
    n_pages = tbl.shape[0] // nb
    whole = lambda a: pl.BlockSpec(a.shape, lambda b, tbl_: (0,) * a.ndim)
    grid_spec = pltpu.PrefetchScalarGridSpec(
        num_scalar_prefetch=1, grid=(nb,),
        in_specs=[whole(q), whole(kr), whole(vr), whole(cum),
                  pl.BlockSpec((1,) + lfpool.shape[1:], lambda b, tbl_: (layer, 0, 0, 0)),
                  whole(mstrict), whole(pgsuf),
                  pl.BlockSpec(memory_space=pl.ANY), pl.BlockSpec(memory_space=pl.ANY)],
        out_specs=pl.BlockSpec((n_t, nb, FOX_W), lambda b, tbl_: (0, 0, 0)),
        scratch_shapes=[pltpu.VMEM((2, n_pages, FOX_W, LANES), F32),
                        pltpu.VMEM((2, n_pages, FOX_W, LANES), F32),
                        pltpu.VMEM((n_pages * 8, LANES), F32),
                        pltpu.SemaphoreType.DMA((2, 2))])
    return pl.pallas_call(
        functools.partial(_fox_s_kernel, layer=layer, n_pages=n_pages, n_t=n_t),
        out_shape=jax.ShapeDtypeStruct((n_t, nb, FOX_W), F32), grid_spec=grid_spec,
        compiler_params=_params(("arbitrary",)), name="fox_s")(
            tbl, q, kr, vr, cum, lfpool, mstrict, pgsuf, k_cache, v_cache)


def _ssd_p_kernel(z_ref, xbc_ref, smr_ref, smt_ref, arow_ref, acol_ref, ltri_ref, utri_ref, exph_ref,
                  dvec_ref, nrm_ref, o_ref, h_ref, ht_scr, *, grp):
    j = pl.program_id(1)
    nj = pl.num_programs(1)

    @pl.when(j == 0)
    def _():
        ht_scr[...] = jnp.zeros(ht_scr.shape, F32)

    finals = [_ssd_p_chunk(s, z_ref, xbc_ref, smr_ref, smt_ref, arow_ref, acol_ref, ltri_ref, utri_ref,
                           exph_ref, dvec_ref, nrm_ref, o_ref, ht_scr) for s in range(grp)]

    @pl.when(j == nj - 1)
    def _():
        rep = SSD_HEADS // SSD_GROUPS
        for s in range(grp):
            h_t = finals[s].T
            for hh in range(SSD_HEADS):
                g = hh // rep
                h_ref[s, hh] = h_t[hh * SSD_HD:(hh + 1) * SSD_HD, g * SSD_STATE:(g + 1) * SSD_STATE]


def _ssd_p_chunk(s, z_ref, xbc_ref, smr_ref, smt_ref, arow_ref, acol_ref, ltri_ref, utri_ref, exph_ref,
                 dvec_ref, nrm_ref, o_ref, ht_scr):
    smr = smr_ref[s]
    exph = exph_ref[...]
    cs_rows = _dot_hl_left(ltri_ref[...], smr * arow_ref[...])
    cs_t = _dot_hl(smt_ref[s] * acol_ref[...], utri_ref[...])
    dt_x = _dot_hl(smr, exph)
    c_hi, c_lo = _split(cs_rows)
    c_lo2 = (cs_rows - c_hi.astype(F32) - c_lo.astype(F32)).astype(BF16)
    cs_x = (jnp.dot(c_hi, exph, preferred_element_type=F32)
            + jnp.dot(c_lo, exph, preferred_element_type=F32)
            + jnp.dot(c_lo2, exph, preferred_element_type=F32))
    end_x = cs_x[CHUNK - 1:CHUNK, :]
    xbc = xbc_ref[s]
    x = xbc[:, 0:SSD_W]
    b_all = xbc[:, SSD_W:SSD_W + SSD_BC].astype(BF16)
    c_all = xbc[:, SSD_W + SSD_BC:SSD_W + 2 * SSD_BC].astype(BF16)
    xdt = x * dt_x
    xdt16 = xdt.astype(BF16)
    ht = ht_scr[s]
    y_state = jnp.dot(c_all, ht.astype(BF16), preferred_element_type=F32)
    upd = _dot_tn(b_all, xdt * jnp.exp(end_x - cs_x))
    cbs = [_dot_nt(c_all[:, g * SSD_STATE:(g + 1) * SSD_STATE], b_all[:, g * SSD_STATE:(g + 1) * SSD_STATE])
           for g in range(SSD_GROUPS)]

    gmask = (lax.broadcasted_iota(jnp.int32, (SSD_BC, SSD_W), 0) // SSD_STATE
             == lax.broadcasted_iota(jnp.int32, (SSD_BC, SSD_W), 1) // (SSD_W // SSD_GROUPS))
    ht_new = ht * jnp.exp(end_x) + jnp.where(gmask, upd, 0.0)
    ht_scr[s] = ht_new

    tril = (lax.broadcasted_iota(jnp.int32, (CHUNK, CHUNK), 1)
            <= lax.broadcasted_iota(jnp.int32, (CHUNK, CHUNK), 0))
    low = lax.broadcasted_iota(jnp.int32, (CHUNK, LANES), 1) < SSD_HD
    rep = SSD_HEADS // SSD_GROUPS
    pairs = []
    for pr in range(SSD_HEADS // 2):
        xp = xdt16[:, pr * LANES:(pr + 1) * LANES]
        ys = []
        for hh in (2 * pr, 2 * pr + 1):
            col = SM_DT + hh
            lm = jnp.exp(jnp.where(tril, cs_rows[:, col:col + 1] - cs_t[col:col + 1, :], -jnp.inf))
            ys.append(jnp.dot((cbs[hh // rep] * lm).astype(BF16), xp, preferred_element_type=F32))
        pairs.append(jnp.where(low, ys[0], ys[1]))
    y = jnp.concatenate(pairs, axis=1) + y_state * jnp.exp(cs_x)
    y = (y + x * dvec_ref[...]) * _silu(z_ref[s])
    gw = SSD_W // SSD_GROUPS
    for g in range(SSD_GROUPS):
        yg = y[:, g * gw:(g + 1) * gw]
        yn = yg * lax.rsqrt(jnp.mean(yg * yg, axis=-1, keepdims=True) + EPS)
        o_ref[s, :, g * gw:(g + 1) * gw] = (yn * nrm_ref[:, g * gw:(g + 1) * gw]).astype(BF16)
    return ht_new


def _ssd_p(z, xbc, smr, smt, arow, acol, ltri, utri, exph, dvec, nrm, l):
    nb, t_pad, _ = z.shape
    nj = t_pad // CHUNK
    grp = math.gcd(nb, MIX_GROUP)
    rows = lambda w_: pl.BlockSpec((grp, CHUNK, w_), lambda b, j: (b, j, 0))
    return pl.pallas_call(
        functools.partial(_ssd_p_kernel, grp=grp),
        out_shape=[jax.ShapeDtypeStruct((nb, t_pad, SSD_W), BF16),
                   jax.ShapeDtypeStruct((nb, SSD_HEADS, SSD_HD, SSD_STATE), F32)],
        grid=(nb // grp, nj),
        in_specs=[rows(SSD_W), rows(SSD_CONV_DIM), rows(LANES),
                  pl.BlockSpec((grp, 16, CHUNK), lambda b, j: (b, 0, j)),
                  _lspec((1, LANES), l), _lspec((16, 1), l), _const_spec((CHUNK, CHUNK)),
                  _const_spec((CHUNK, CHUNK)), _const_spec((LANES, SSD_W)),
                  _lspec((1, SSD_W), l), _lspec((1, SSD_W), l)],
        out_specs=[rows(SSD_W),
                   pl.BlockSpec((grp, SSD_HEADS, SSD_HD, SSD_STATE), lambda b, j: (b, 0, 0, 0))],
        scratch_shapes=[pltpu.VMEM((grp, SSD_BC, SSD_W), F32)],
        compiler_params=_params(("parallel", "arbitrary")), name="ssd_p")(
            z, xbc, smr, smt, arow, acol, ltri, utri, exph, dvec, nrm)


def _ssd_s_kernel(x_ref, b_ref, c_ref, sm_ref, xg_ref, zg_ref, a_ref, d_ref, nrm_ref, st_ref, *rest,
                  n_t, n_prev):
    if n_prev:
        prev_ref, o_ref, so_ref, ybuf = rest
        for k in range(n_prev):
            so_ref[k, 0] = prev_ref[k, 0]
    else:
        o_ref, so_ref, ybuf = rest
    hh = pl.program_id(0)
    rep = SSD_HEADS // SSD_GROUPS
    gw = SSD_W // SSD_GROUPS
    dt = [sm_ref[t, pl.ds(SM_DT + hh, 1), :] for t in range(n_t)]
    dec = [jnp.exp(dt[t] * a_ref[pl.ds(hh, 1), :]) for t in range(n_t)]

    def body(i, carry):
        for u in range(DEC_UNROLL):
            p = i * DEC_UNROLL + u
            hp = st_ref[0, 0, p]
            for t in range(n_t):
                hp = hp * dec[t] + b_ref[t] * (x_ref[t, pl.ds(p, 1), :] * dt[t])
                ybuf[t, pl.ds(hh * SSD_HD + p, 1), :] = jnp.sum(c_ref[t] * hp, axis=0, keepdims=True)
            so_ref[n_prev, 0, p] = hp
        return carry

    lax.fori_loop(0, SSD_HD // DEC_UNROLL, body, 0)

    @pl.when(lax.rem(hh, rep) == rep - 1)
    def _():
        r0 = pl.multiple_of((hh // rep) * gw, gw)
        for t in range(n_t):
            y = ybuf[t, pl.ds(r0, gw), :] + xg_ref[t] * d_ref[pl.ds(r0, gw), :]
            y = y * _silu(zg_ref[t])
            yn = y * lax.rsqrt(jnp.mean(y * y, axis=0, keepdims=True) + EPS)
            o_ref[t, pl.ds(r0, gw), :] = yn * nrm_ref[pl.ds(r0, gw), :]


def _ssd_s(pt, a_b, d_b, nrm_b, state, layer, prev):
    n_t = pt.shape[0]
    n_prev = 0 if prev is None else prev.shape[0]
    rep = SSD_HEADS // SSD_GROUPS
    gw = SSD_W // SSD_GROUPS
    blk = lambda h_, f: pl.BlockSpec((n_t, h_, LANES), lambda hh: (0, f(hh), 0))
    st_spec = pl.BlockSpec((1, 1, SSD_HD, SSD_STATE, LANES), lambda hh: (layer, hh, 0, 0, 0))
    per_head = lambda n: pl.BlockSpec((n, 1, SSD_HD, SSD_STATE, LANES), lambda hh: (0, hh, 0, 0, 0))
    return pl.pallas_call(
        functools.partial(_ssd_s_kernel, n_t=n_t, n_prev=n_prev),
        out_shape=[jax.ShapeDtypeStruct((n_t, SSD_W, LANES), F32),
                   jax.ShapeDtypeStruct((n_prev + 1, SSD_HEADS, SSD_HD, SSD_STATE, LANES), F32)],
        grid=(SSD_HEADS,),
        in_specs=[blk(SSD_HD, lambda hh: R_XBC // SSD_HD + hh),
                  blk(SSD_STATE, lambda hh: (R_XBC + SSD_W) // SSD_STATE + hh // rep),
                  blk(SSD_STATE, lambda hh: (R_XBC + SSD_W + SSD_BC) // SSD_STATE + hh // rep),
                  blk(LANES, lambda hh: R_SM // LANES),
                  blk(gw, lambda hh: R_XBC // gw + hh // rep),
                  blk(gw, lambda hh: R_Z // gw + hh // rep),
                  _lspec((SSD_HEADS, LANES), layer), _lspec((SSD_W, LANES), layer),
                  _lspec((SSD_W, LANES), layer), st_spec] + ([per_head(n_prev)] if n_prev else []),
        out_specs=[_const_spec((n_t, SSD_W, LANES)), per_head(n_prev + 1)],
        scratch_shapes=[pltpu.VMEM((n_t, SSD_W, LANES), F32)],
        compiler_params=_params(("arbitrary",)), name="ssd_s")(
            pt, pt, pt, pt, pt, pt, a_b, d_b, nrm_b, state, *([prev] if n_prev else []))


def _gla_p_kernel(gqk_ref, gv_ref, gg_ref, smr_ref, wg_ref, gb_ref, bl_ref, be_ref, g64_ref, gn_ref,
                  o_ref, s_ref, st_scr, *, t_real, grp):
    j = pl.program_id(1)
    nj = pl.num_programs(1)

    @pl.when(j == 0)
    def _():
        st_scr[...] = jnp.zeros(st_scr.shape, F32)

    finals = [_gla_p_chunk(s, j, gqk_ref, gv_ref, gg_ref, smr_ref, wg_ref, gb_ref, bl_ref, be_ref,
                           g64_ref, gn_ref, o_ref, st_scr, t_real) for s in range(grp)]

    @pl.when(j == nj - 1)
    def _():
        for s in range(grp):
            s_kv = finals[s].T
            for h in range(GLA_HEADS):
                s_ref[s, h] = s_kv[h * GLA_DK:(h + 1) * GLA_DK, h * GLA_DV:(h + 1) * GLA_DV]


def _gla_p_chunk(s, j, gqk_ref, gv_ref, gg_ref, smr_ref, wg_ref, gb_ref, bl_ref, be_ref, g64_ref,
                 gn_ref, o_ref, st_scr, t_real):
    sub = GLA_SUB
    n_sub = CHUNK // sub
    row = lax.broadcasted_iota(jnp.int32, (CHUNK, GLA_KW), 0)
    valid = j * CHUNK + row < t_real
    glog = _log_sigmoid(_dot(smr_ref[s], wg_ref[...]) + gb_ref[...]) * (1.0 / GLA_TAU)
    glog = jnp.where(valid, glog, 0.0)
    gqk = gqk_ref[s]
    gq = gqk[:, 0:GLA_KW] * (GLA_DK ** -0.5)
    gk = jnp.where(valid, gqk[:, GLA_KW:2 * GLA_KW], 0.0)
    v = gv_ref[s]
    bcl = _dot_hl_left(bl_ref[...], glog)
    tot = _dot_hl_left(be_ref[0], glog)
    mid = _dot_hl_left(be_ref[1], glog)
    qe = (gq * jnp.exp(bcl)).astype(BF16)
    qd = (gq * jnp.exp(bcl - mid)).astype(BF16)
    kp = gk * jnp.exp(mid - bcl)
    kend = (gk * jnp.exp(tot - bcl)).astype(BF16)
    dec = jnp.exp(tot)

    r_k = lax.broadcasted_iota(jnp.int32, (GLA_HEADS * sub, GLA_KW), 0) // sub
    c_k = lax.broadcasted_iota(jnp.int32, (GLA_HEADS * sub, GLA_KW), 1) // GLA_DK
    r_v = lax.broadcasted_iota(jnp.int32, (GLA_HEADS * sub, GLA_W), 0) // sub
    c_v = lax.broadcasted_iota(jnp.int32, (GLA_HEADS * sub, GLA_W), 1) // GLA_DV
    r_s = lax.broadcasted_iota(jnp.int32, (GLA_W, GLA_KW), 0) // GLA_DV
    c_s = lax.broadcasted_iota(jnp.int32, (GLA_W, GLA_KW), 1) // GLA_DK
    causal = (lax.broadcasted_iota(jnp.int32, (sub, GLA_HEADS * sub), 1) % sub
              <= lax.broadcasted_iota(jnp.int32, (sub, GLA_HEADS * sub), 0))
    sls = [slice(i * sub, (i + 1) * sub) for i in range(n_sub)]
    atts, upds = [], []
    for sl in sls:
        kbd = jnp.where(r_k == c_k, jnp.concatenate([kp[sl]] * GLA_HEADS, axis=0), 0.0)
        atts.append(_dot_nt(qd[sl], kbd))
        upds.append(_dot_tn(v[sl], kend[sl]))
    sts = [st_scr[s]]
    for i in range(n_sub):
        sts.append(sts[i] * dec[i * sub:i * sub + 1, :] + jnp.where(r_s == c_s, upds[i], 0.0))
    st_scr[s] = sts[n_sub]
    outs = []
    for i, sl in enumerate(sls):
        vbd = jnp.where(r_v == c_v, jnp.concatenate([v[sl]] * GLA_HEADS, axis=0), 0.0)
        outs.append(_dot(jnp.where(causal, atts[i], 0.0), vbd) + _dot_nt(qe[sl], sts[i]))
    o = jnp.concatenate(outs, axis=0)
    msq = _dot_hl(o * o, g64_ref[...])
    o_ref[s] = (o * lax.rsqrt(msq + EPS) * gn_ref[...] * _silu(gg_ref[s])).astype(BF16)
    return sts[n_sub]


def _gla_p(gqk, gv, gg, smr, wgp, gb, bl, be, g64, gn, t_real, l):
    nb, t_pad, _ = gv.shape
    nj = t_pad // CHUNK
    grp = math.gcd(nb, MIX_GROUP)
    rows = lambda w_: pl.BlockSpec((grp, CHUNK, w_), lambda b, j: (b, j, 0))
    return pl.pallas_call(
        functools.partial(_gla_p_kernel, t_real=t_real, grp=grp),
        out_shape=[jax.ShapeDtypeStruct((nb, t_pad, GLA_W), BF16),
                   jax.ShapeDtypeStruct((nb, GLA_HEADS, GLA_DK, GLA_DV), F32)],
        grid=(nb // grp, nj),
        in_specs=[rows(2 * GLA_KW), rows(GLA_W), rows(GLA_W), rows(LANES),
                  _lspec((LANES, GLA_KW), l), _lspec((1, GLA_KW), l),
                  _const_spec((CHUNK, CHUNK)), _const_spec((2, CHUNK, CHUNK)),
                  _const_spec((GLA_W, GLA_W)), _lspec((1, GLA_W), l)],
        out_specs=[rows(GLA_W),
                   pl.BlockSpec((grp, GLA_HEADS, GLA_DK, GLA_DV), lambda b, j: (b, 0, 0, 0))],
        scratch_shapes=[pltpu.VMEM((grp, GLA_W, GLA_KW), F32)],
        compiler_params=_params(("parallel", "arbitrary")), name="gla_p")(
            gqk, gv, gg, smr, wgp, gb, bl, be, g64, gn)


def _gla_s_kernel(q_ref, k_ref, v_ref, gg_ref, sm_ref, wgt_ref, gb_ref, gn_ref, st_ref,
                  o_ref, so_ref, eg_scr, acc_scr, *, n_t):
    for t in range(n_t):
        glog = _log_sigmoid(_dot(wgt_ref[...], sm_ref[t]) + gb_ref[...]) * (1.0 / GLA_TAU)
        eg_scr[t] = jnp.exp(glog)
        acc_scr[t] = jnp.zeros((GLA_DV, LANES), F32)

    def body(i, carry):
        states = []
        for u in range(DEC_UNROLL):
            kk = i * DEC_UNROLL + u
            s = st_ref[0, 0, kk]
            row = []
            for t in range(n_t):
                s = s * eg_scr[t, pl.ds(kk, 1), :] + k_ref[t, pl.ds(kk, 1), :] * v_ref[t]
                row.append(q_ref[t, pl.ds(kk, 1), :] * s)
            so_ref[0, 0, kk] = s
            states.append(row)
        for t in range(n_t):
            acc = acc_scr[t]
            for u in range(DEC_UNROLL):
                acc = acc + states[u][t]
            acc_scr[t] = acc
        return carry

    lax.fori_loop(0, GLA_DK // DEC_UNROLL, body, 0)
    for t in range(n_t):
        o = acc_scr[t]
        on = o * lax.rsqrt(jnp.mean(o * o, axis=0, keepdims=True) + EPS)
        o_ref[t] = on * gn_ref[...] * _silu(gg_ref[t])


def _gla_s(pt, wgt, gb_b, gn_b, state, layer):
    n_t = pt.shape[0]
    blk = lambda h_, f: pl.BlockSpec((n_t, h_, LANES), lambda hh: (0, f(hh), 0))
    return pl.pallas_call(
        functools.partial(_gla_s_kernel, n_t=n_t),
        out_shape=[jax.ShapeDtypeStruct((n_t, GLA_W, LANES), F32),
                   jax.ShapeDtypeStruct((1, GLA_HEADS, GLA_DK, GLA_DV, LANES), F32)],
        grid=(GLA_HEADS,),
        in_specs=[blk(GLA_DK, lambda hh: R_GQ // GLA_DK + hh),
                  blk(GLA_DK, lambda hh: R_GK // GLA_DK + hh),
                  blk(GLA_DV, lambda hh: R_GV // GLA_DV + hh),
                  blk(GLA_DV, lambda hh: R_GG // GLA_DV + hh),
                  blk(LANES, lambda hh: R_SM // LANES),
                  pl.BlockSpec((None, GLA_DK, LANES), lambda hh: (layer, hh, 0)),
                  pl.BlockSpec((None, GLA_DK, LANES), lambda hh: (layer, hh, 0)),
                  _lspec((GLA_DV, LANES), layer),
                  pl.BlockSpec((1, 1, GLA_DK, GLA_DV, LANES), lambda hh: (layer, hh, 0, 0, 0))],
        out_specs=[blk(GLA_DV, lambda hh: hh),
                   pl.BlockSpec((1, 1, GLA_DK, GLA_DV, LANES), lambda hh: (0, hh, 0, 0, 0))],
        scratch_shapes=[pltpu.VMEM((n_t, GLA_DK, LANES), F32), pltpu.VMEM((n_t, GLA_DV, LANES), F32)],
        compiler_params=_params(("arbitrary",)), name="gla_s")(
            pt, pt, pt, pt, pt, wgt, gb_b, gn_b, state)


def _tri_consts(n_pages):
    i = np.arange(CHUNK)
    ltri = (i[None, :] <= i[:, None]).astype(np.float32)
    utri = ltri.T
    mstrict = (i[:, None] > i[None, :]).astype(np.float32)
    same = (i[:, None] // GLA_SUB) == (i[None, :] // GLA_SUB)
    bl = (same & (i[None, :] <= i[:, None])).astype(np.float32)
    to_mid = i[None, :] <= (i[:, None] // GLA_SUB) * GLA_SUB + GLA_SUB // 2 - 1
    be = np.stack([same, same & to_mid]).astype(np.float32)
    r = np.arange(n_pages * 8)
    pgsuf = ((r[:, None] % 8 == r[None, :] % 8) & (r[None, :] // 8 > r[:, None] // 8)).astype(np.float32)
    h = np.arange(FOX_W)
    g64 = ((h[:, None] // 64) == (h[None, :] // 64)).astype(np.float32) / 64.0
    cw = np.arange(SSD_W)
    exph = (i[:, None] == SM_DT + cw[None, :] // SSD_HD).astype(np.float32)
    c = lambda a: jnp.asarray(a, BF16)
    return dict(ltri=c(ltri), utri=c(utri), mstrict=c(mstrict), bl=c(bl), be=c(be), pgsuf=c(pgsuf),
                g64=c(g64), exph=c(exph))


def kernel(x_prompt, x_sample, cache_fox_k, cache_fox_v, cache_fox_logf, state_ssm, state_conv,
           state_gla, page_table, meta_tokens, ffn1_norm, ffn1_w_in, ffn1_w_out, mix_norm, w_mix_in,
           fox_q_norm, fox_k_norm, fox_f_bias, ssd_conv_w, ssd_conv_b, ssd_dt_bias, ssd_a_log, ssd_d,
           ssd_norm, gla_w_gate, gla_gate_bias, gla_norm, w_mix_out, ffn2_norm, ffn2_w_in, ffn2_w_out):
    nbp, seq, _ = x_prompt.shape
    nbs, n_t, _ = x_sample.shape
    depth = ffn1_norm.shape[0]
    assert nbs == LANES and seq % CHUNK == 0
    t_real = N_META + seq
    t_pad = -(-t_real // CHUNK) * CHUNK
    n_pool, page_size = cache_fox_k.shape[1], cache_fox_k.shape[2]
    assert page_size == LANES and n_pool % 2 == 0
    n_pages = page_table.shape[1]
    consts = _tri_consts(n_pages)

    xs = jnp.transpose(x_sample, (1, 0, 2)).reshape(n_t * nbs, D_MODEL)

    k_cache = jnp.transpose(cache_fox_k, (0, 1, 3, 4, 2)).reshape(depth, n_pool, FOX_W, page_size)
    v_cache = jnp.transpose(cache_fox_v, (0, 1, 3, 4, 2)).reshape(depth, n_pool, FOX_W, page_size)
    lfpool = jnp.transpose(cache_fox_logf, (0, 1, 3, 2)).reshape(depth, n_pool // 2, 8, page_size)
    ssm_in = jnp.transpose(state_ssm, (0, 2, 3, 4, 1))
    gla_in = jnp.transpose(state_gla, (0, 2, 3, 4, 1))
    conv_in = jnp.transpose(state_conv, (0, 2, 1, 3))
    tbl = page_table.reshape(-1).astype(jnp.int32)

    w1i, w1o = ffn1_w_in.astype(BF16), ffn1_w_out.astype(BF16)
    w2i, w2o = ffn2_w_in.astype(BF16), ffn2_w_out.astype(BF16)
    wmo = w_mix_out.astype(BF16)
    col = [0] + list(IN_SPLITS) + [w_mix_in.shape[-1]]
    part = lambda i: w_mix_in[:, :, col[i]:col[i + 1]]
    fq, fk, fv, ff, sz, sxbc, sdt, gq, gk, gv, glr, gg = [part(i) for i in range(len(IN_SIZES))]
    n_small = FOX_HEADS + SSD_HEADS + GLA_RANK
    wproj = jnp.concatenate([fq, fk, fv, sz, sxbc, gq, gk, gv, gg, ff, sdt, glr,
                             jnp.zeros((depth, D_MODEL, LANES - n_small), F32)], axis=-1).astype(BF16)
    n1, n2, nm = ffn1_norm[:, None], ffn2_norm[:, None], mix_norm[:, None]
    qg = jnp.tile(fox_q_norm, (1, FOX_HEADS))[:, None]
    kg = jnp.tile(fox_k_norm, (1, FOX_HEADS))[:, None]
    a = -jnp.exp(ssd_a_log)
    lanes_pad = lambda v, off: jnp.pad(v, ((0, 0), (off, LANES - off - v.shape[1])))[:, None]
    fb, dtb, arow = lanes_pad(fox_f_bias, SM_F), lanes_pad(ssd_dt_bias, SM_DT), lanes_pad(a, SM_DT)
    acol = jnp.pad(a, ((0, 0), (SM_DT, 16 - SM_DT - SSD_HEADS)))[:, :, None]
    cw = jnp.pad(ssd_conv_w, ((0, 0), (0, 8 - SSD_CONV), (0, 0)))
    cb = ssd_conv_b[:, None]
    d_rep = jnp.repeat(ssd_d, SSD_HD, axis=1)
    dvec, snrm = d_rep[:, None], ssd_norm[:, None]
    wgp = jnp.pad(gla_w_gate, ((0, 0), (SM_LR, LANES - SM_LR - GLA_RANK), (0, 0))).astype(BF16)
    gb = gla_gate_bias[:, None]
    gn = jnp.tile(gla_norm, (1, GLA_HEADS))[:, None]
    on_lanes = lambda v: jnp.broadcast_to(v[:, :, None], v.shape + (LANES,))
    a_b, d_b, snrm_b = on_lanes(a), on_lanes(d_rep), on_lanes(ssd_norm)
    wgt = jnp.pad(jnp.transpose(gla_w_gate, (0, 2, 1)),
                  ((0, 0), (0, 0), (SM_LR, LANES - SM_LR - GLA_RANK))).astype(BF16)
    gb_b, gn_b = on_lanes(gla_gate_bias), on_lanes(gla_norm)

    outs_p = [[] for _ in range(6)]
    outs_s = [[] for _ in range(6)]
    ssm_all = None
    for l in range(depth):
        last = l == depth - 1
        if l == 0:
            meta = _ffn(meta_tokens, N_META, (n1, w1i, w1o, 0))
            frame = jnp.concatenate([jnp.broadcast_to(meta[None], (nbp, N_META, D_MODEL)),
                                     jnp.zeros((nbp, t_pad - N_META, D_MODEL), F32)], axis=1)
            if seq % FFN_TM == 0:
                xp = _ffn_into(x_prompt, frame, FFN_TM, (n1, w1i, w1o, 0), N_META)
            else:
                xp = lax.dynamic_update_slice(
                    frame, _ffn(x_prompt.reshape(nbp * seq, D_MODEL), CHUNK, (n1, w1i, w1o, 0)).reshape(
                        nbp, seq, D_MODEL), (0, N_META, 0))
            xp = xp.reshape(nbp * t_pad, D_MODEL)
        else:
            xp = _ffn(xp, FFN_TM, (n1, w1i, w1o, l))
        (qat, kaug, kt, vt, vt16, z, xbc, gqk, gvv, ggg, smr, smt, convp) = _proj_p(
            xp.reshape(nbp, t_pad, D_MODEL), nm, wproj, qg, kg, fb, dtb, cw, cb, consts["g64"],
            consts["ltri"], t_real, l)
        fox_o = _fox_p(qat, kaug, vt16)
        ssd_o, ssm_p = _ssd_p(z, xbc, smr, smt, arow, acol, consts["ltri"], consts["utri"],
                              consts["exph"], dvec, snrm, l)
        gla_o, gla_p = _gla_p(gqk, gvv, ggg, smr, wgp, gb, consts["bl"], consts["be"], consts["g64"],
                              gn, t_real, l)
        mix = ("rows", fox_o.reshape(-1, FOX_W), ssd_o.reshape(-1, SSD_W), gla_o.reshape(-1, GLA_W), wmo, l)
        window = (nbp, t_pad, N_META, seq) if last and seq % FFN_TM == 0 else None
        xp = _ffn(xp, FFN_TM, (n2, w2i, w2o, l), mix=mix, window=window)
        outs_p[0].append(kt.reshape(nbp, FOX_HEADS, FOX_HD, t_pad))
        outs_p[1].append(vt.reshape(nbp, FOX_HEADS, FOX_HD, t_pad))
        outs_p[2].append(jnp.transpose(smt[:, 0:FOX_HEADS, :t_real], (0, 2, 1)))
        outs_p[3].append(ssm_p)
        outs_p[4].append(convp[:, 8 - (SSD_CONV - 1):])
        outs_p[5].append(gla_p)

        xs = _ffn(xs, xs.shape[0], (n1, w1i, w1o, l))
        qs, krs, vrs, pt, cum, convs = _proj_s(xs, nm, wproj, qg, kg, fb, dtb, cw, cb, consts["g64"],
                                               conv_in, l)
        fox_os = _fox_s(tbl, qs, krs, vrs, cum, lfpool, consts["mstrict"], consts["pgsuf"],
                        k_cache, v_cache, l)
        ssd_ot, ssm_all = _ssd_s(pt, a_b, d_b, snrm_b, ssm_in, l, ssm_all)
        gla_ot, gla_s = _gla_s(pt, wgt, gb_b, gn_b, gla_in, l)
        xs = _ffn(xs, xs.shape[0], (n2, w2i, w2o, l), mix=("features", fox_os, ssd_ot, gla_ot, wmo, l))
        kts = pt[:, R_K:R_K + FOX_W].reshape(n_t, FOX_HEADS, FOX_HD, nbs)
        vts = pt[:, R_V:R_V + FOX_W].reshape(n_t, FOX_HEADS, FOX_HD, nbs)
        outs_s[0].append(jnp.transpose(kts, (3, 0, 1, 2)))
        outs_s[1].append(jnp.transpose(vts, (3, 0, 1, 2)))
        outs_s[2].append(jnp.transpose(pt[:, R_SM + SM_F:R_SM + SM_F + FOX_HEADS], (2, 0, 1)))
        outs_s[4].append(jnp.transpose(convs, (1, 0, 2)))
        outs_s[5].append(jnp.transpose(gla_s[0], (3, 0, 1, 2)))

    y_prompt = xp if xp.ndim == 3 else xp.reshape(nbp, t_pad, D_MODEL)[:, N_META:t_real]
    y_sample = jnp.transpose(xs.reshape(n_t, nbs, D_MODEL), (1, 0, 2))
    k_p = jnp.transpose(jnp.stack(outs_p[0])[..., :t_real], (0, 1, 4, 2, 3))
    v_p = jnp.transpose(jnp.stack(outs_p[1])[..., :t_real], (0, 1, 4, 2, 3))
    lf_p, ssm_p, conv_p, gla_p = [jnp.stack(a) for a in outs_p[2:]]
    k_s, v_s, lf_s, conv_s, gla_s = [jnp.stack(outs_s[i]) for i in (0, 1, 2, 4, 5)]
    ssm_s = jnp.transpose(ssm_all, (0, 4, 1, 2, 3))
    return (y_prompt, y_sample, k_p, v_p, lf_p, ssm_p, conv_p, gla_p, k_s, v_s, lf_s, ssm_s, conv_s, gla_s)
```
